```python
import math
import jax, jax.numpy as jnp
from jax import lax
import numpy as np

D_MODEL = 1024
BATCH = 8
SEQ = 2048
DEPTH = 1

HEAD_DIM = 64
ATTN_HEADS = 8
DILATION_GROUPS = ((128, 1), (512, 4), (2048, 16))
N_GROUPS = len(DILATION_GROUPS)
ATTN_WIDTH = ATTN_HEADS * HEAD_DIM
QKV_WIDTH = N_GROUPS * ATTN_WIDTH
BLOCK = 128
ROPE_THETA = 10000.0
NEG_INF = -1e30
SSM_GROUP = 16
SSM_GROUPS = 32
SSM_WIDTH = SSM_GROUP * SSM_GROUPS
SSM_STATE = 64
DT_MIN = 1e-3
DT_MAX = 1e-1
N_BRANCH = 2
IN_WIDTH = 3 * QKV_WIDTH + SSM_WIDTH + N_BRANCH * D_MODEL
D_FF = -(-(8 * D_MODEL) // (3 * 256)) * 256
DN_ALPHA = (2.0 * DEPTH) ** 0.25
DN_BETA = (8.0 * DEPTH) ** -0.25
LN_EPS = 1e-5

kernel_name = "dilated_attn_s5_gated_hybrid_deepnorm"


def layer_norm(x, g, b):
    xf = x.astype(jnp.float32)
    mu = jnp.mean(xf, axis=-1, keepdims=True)
    var = jnp.mean(jnp.square(xf - mu), axis=-1, keepdims=True)
    y = (xf - mu) * lax.rsqrt(var + LN_EPS)
    return (y * g.astype(jnp.float32) + b.astype(jnp.float32)).astype(x.dtype)


def apply_rope(t, pos):
    half = HEAD_DIM // 2
    inv_freq = ROPE_THETA ** (-jnp.arange(half, dtype=jnp.float32) / half)
    ang = pos[:, None] * inv_freq[None, :]
    cos = jnp.cos(ang)[None, :, None, None, :]
    sin = jnp.sin(ang)[None, :, None, None, :]
    t1 = t[..., :half].astype(jnp.float32)
    t2 = t[..., half:].astype(jnp.float32)
    return jnp.concatenate([t1 * cos - t2 * sin, t1 * sin + t2 * cos], axis=-1)


def dilated_group_attention(q, k, v, window, dilation):
    b, s, h, dh = q.shape
    n = s // dilation
    nb = -(-n // BLOCK)
    n_pad = nb * BLOCK
    back = window // dilation

    def to_phase_blocks(t):
        t = t.reshape(b, n, dilation, h, dh)
        t = jnp.pad(t, ((0, 0), (0, n_pad - n), (0, 0), (0, 0), (0, 0)))
        return t.reshape(b, nb, BLOCK, dilation, h, dh)

    def with_prev_block(t):
        prev = jnp.pad(t, ((0, 0), (1, 0), (0, 0), (0, 0), (0, 0), (0, 0)))[:, :-1]
        return jnp.concatenate([prev, t], axis=2)

    qb = to_phase_blocks(q)
    kw = with_prev_block(to_phase_blocks(k))
    vw = with_prev_block(to_phase_blocks(v))

    scores = jnp.einsum("bnqrhd,bnkrhd->bnrhqk", qb, kw,
                        preferred_element_type=jnp.float32) / math.sqrt(dh)
    a_idx = jnp.arange(BLOCK)[None, :, None]
    c_idx = jnp.arange(2 * BLOCK)[None, None, :]
    blk = jnp.arange(nb)[:, None, None]
    dist = BLOCK + a_idx - c_idx
    key_sub = (blk - 1) * BLOCK + c_idx
    valid = (dist >= 0) & (dist <= back) & (key_sub >= 0)
    scores = jnp.where(valid[None, :, None, None], scores, NEG_INF)
    lse = jax.nn.logsumexp(scores, axis=-1)
    probs = jnp.exp(scores - lse[..., None])
    out = jnp.einsum("bnrhqk,bnkrhd->bnqrhd", probs, vw.astype(jnp.float32))
    out = out.reshape(b, n_pad, dilation, h, dh)[:, :n].reshape(b, s, h, dh)
    lse = jnp.transpose(lse, (0, 1, 4, 2, 3)).reshape(b, n_pad, dilation, h)[:, :n]
    return out, lse.reshape(b, s, h)


def s5_ssm(u, a_re, a_im, log_dt, b_re, b_im, c_re, c_im, d_skip):
    f32 = jnp.float32
    bsz, s, _ = u.shape
    ug = u.astype(f32).reshape(bsz, s, SSM_GROUPS, SSM_GROUP)
    lam = lax.complex(a_re.astype(f32), a_im.astype(f32))
    dt = jnp.exp(log_dt.astype(f32))[:, None]
    a_bar = jnp.exp(lam * dt)
    b_c = lax.complex(b_re.astype(f32), b_im.astype(f32))
    b_bar = ((a_bar - 1.0) / lam)[..., None] * b_c
    bu = jnp.einsum("gph,bsgh->bsgp", b_bar, ug.astype(jnp.complex64))
    a_seq = jnp.broadcast_to(a_bar, bu.shape)

    def combine(left, right):
        a_l, x_l = left
        a_r, x_r = right
        return a_r * a_l, a_r * x_l + x_r

    _, states = lax.associative_scan(combine, (a_seq, bu), axis=1)
    c_c = lax.complex(c_re.astype(f32), c_im.astype(f32))
    y = jnp.einsum("ghp,bsgp->bsgh", c_c, states).real
    y = y + d_skip.astype(f32).reshape(SSM_GROUPS, SSM_GROUP) * ug
    return y.reshape(bsz, s, SSM_WIDTH)


def setup_inputs(seed: int = 0) -> dict:
    key = jax.random.key(seed)
    ks = jax.random.split(key, 24)
    f32 = jnp.float32
    L = DEPTH

    def nrm(k, shape, scale):
        return jax.random.normal(k, shape, f32) * scale

    x = jax.random.normal(ks[0], (BATCH, SEQ, D_MODEL), f32)
    w_in = nrm(ks[1], (L, D_MODEL, IN_WIDTH), D_MODEL ** -0.5)
    b_gate = nrm(ks[2], (L, N_BRANCH, D_MODEL), 0.02)
    w_attn_br = nrm(ks[3], (L, ATTN_WIDTH, D_MODEL), ATTN_WIDTH ** -0.5)
    w_ssm_br = nrm(ks[4], (L, SSM_WIDTH, D_MODEL), SSM_WIDTH ** -0.5)
    w_out = nrm(ks[5], (L, D_MODEL, D_MODEL), DN_BETA * D_MODEL ** -0.5)
    ssm_a_re = -0.5 + nrm(ks[6], (L, SSM_GROUPS, SSM_STATE), 0.01)
    ssm_a_im = (math.pi * jnp.arange(SSM_STATE, dtype=f32))[None, None, :] + nrm(ks[7], (L, SSM_GROUPS, SSM_STATE), 0.01)
    ssm_log_dt = jax.random.uniform(ks[8], (L, SSM_GROUPS), f32, math.log(DT_MIN), math.log(DT_MAX))
    ssm_b_re = nrm(ks[9], (L, SSM_GROUPS, SSM_STATE, SSM_GROUP), (2 * SSM_GROUP) ** -0.5)
    ssm_b_im = nrm(ks[10], (L, SSM_GROUPS, SSM_STATE, SSM_GROUP), (2 * SSM_GROUP) ** -0.5)
    ssm_c_re = nrm(ks[11], (L, SSM_GROUPS, SSM_GROUP, SSM_STATE), SSM_STATE ** -0.5)
    ssm_c_im = nrm(ks[12], (L, SSM_GROUPS, SSM_GROUP, SSM_STATE), SSM_STATE ** -0.5)
    ssm_d = nrm(ks[13], (L, SSM_WIDTH), 1.0)
    w_glu = nrm(ks[14], (L, SSM_WIDTH, 2 * SSM_WIDTH), SSM_WIDTH ** -0.5)
    ln1_g = 1.0 + nrm(ks[15], (L, D_MODEL), 0.02)
    ln1_b = nrm(ks[16], (L, D_MODEL), 0.02)
    w_ff_gate = nrm(ks[17], (L, D_MODEL, D_FF), D_MODEL ** -0.5)
    w_ff_up = nrm(ks[18], (L, D_MODEL, D_FF), D_MODEL ** -0.5)
    w_ff_down = nrm(ks[19], (L, D_FF, D_MODEL), DN_BETA * D_FF ** -0.5)
    ln2_g = 1.0 + nrm(ks[20], (L, D_MODEL), 0.02)
    ln2_b = nrm(ks[21], (L, D_MODEL), 0.02)
    return {"x": x, "w_in": w_in, "b_gate": b_gate, "w_attn_br": w_attn_br,
            "w_ssm_br": w_ssm_br, "w_out": w_out, "ssm_a_re": ssm_a_re,
            "ssm_a_im": ssm_a_im, "ssm_log_dt": ssm_log_dt, "ssm_b_re": ssm_b_re,
            "ssm_b_im": ssm_b_im, "ssm_c_re": ssm_c_re, "ssm_c_im": ssm_c_im,
            "ssm_d": ssm_d, "w_glu": w_glu, "ln1_g": ln1_g, "ln1_b": ln1_b,
            "w_ff_gate": w_ff_gate, "w_ff_up": w_ff_up, "w_ff_down": w_ff_down,
            "ln2_g": ln2_g, "ln2_b": ln2_b}


def reference(x, w_in, b_gate, w_attn_br, w_ssm_br, w_out, ssm_a_re, ssm_a_im,
              ssm_log_dt, ssm_b_re, ssm_b_im, ssm_c_re, ssm_c_im, ssm_d, w_glu,
              ln1_g, ln1_b, w_ff_gate, w_ff_up, w_ff_down, ln2_g, ln2_b):
    bsz, s, _ = x.shape
    pos = jnp.arange(s, dtype=jnp.float32)
    for layer in range(DEPTH):
        proj = x @ w_in[layer]
        q = proj[..., :QKV_WIDTH].reshape(bsz, s, N_GROUPS, ATTN_HEADS, HEAD_DIM)
        k = proj[..., QKV_WIDTH:2 * QKV_WIDTH].reshape(bsz, s, N_GROUPS, ATTN_HEADS, HEAD_DIM)
        v = proj[..., 2 * QKV_WIDTH:3 * QKV_WIDTH].reshape(bsz, s, N_GROUPS, ATTN_HEADS, HEAD_DIM)
        u = proj[..., 3 * QKV_WIDTH:3 * QKV_WIDTH + SSM_WIDTH]
        gate_logits = proj[..., 3 * QKV_WIDTH + SSM_WIDTH:].reshape(bsz, s, N_BRANCH, D_MODEL)
        q = apply_rope(q, pos)
        k = apply_rope(k, pos)

        outs, lses = [], []
        for g, (window, dilation) in enumerate(DILATION_GROUPS):
            o_g, lse_g = dilated_group_attention(q[:, :, g], k[:, :, g], v[:, :, g], window, dilation)
            outs.append(o_g)
            lses.append(lse_g)
        wts = jax.nn.softmax(jnp.stack(lses, axis=0), axis=0)
        attn = jnp.sum(wts[..., None] * jnp.stack(outs, axis=0), axis=0)
        y_attn = attn.reshape(bsz, s, ATTN_WIDTH).astype(x.dtype) @ w_attn_br[layer]

        y_s = s5_ssm(u, ssm_a_re[layer], ssm_a_im[layer], ssm_log_dt[layer], ssm_b_re[layer],
                     ssm_b_im[layer], ssm_c_re[layer], ssm_c_im[layer], ssm_d[layer])
        glu = jax.nn.gelu(y_s).astype(x.dtype) @ w_glu[layer]
        y_s = glu[..., :SSM_WIDTH] * jax.nn.sigmoid(glu[..., SSM_WIDTH:])
        y_ssm = y_s @ w_ssm_br[layer]

        gates = jax.nn.sigmoid((gate_logits + b_gate[layer]).astype(jnp.float32))
        mixed = gates[..., 0, :] * y_attn.astype(jnp.float32) + gates[..., 1, :] * y_ssm.astype(jnp.float32)
        mix_out = mixed.astype(x.dtype) @ w_out[layer]
        h = layer_norm(DN_ALPHA * x + mix_out.astype(x.dtype), ln1_g[layer], ln1_b[layer])

        ff = (jax.nn.silu(h @ w_ff_gate[layer]) * (h @ w_ff_up[layer])) @ w_ff_down[layer]
        x = layer_norm(DN_ALPHA * h + ff.astype(h.dtype), ln2_g[layer], ln2_b[layer])
    return x
```

```python
import functools
import math

import jax
import jax.numpy as jnp
from jax import lax
from jax.experimental import pallas as pl
from jax.experimental.pallas import tpu as pltpu

F32 = jnp.float32
BF16 = jnp.bfloat16

LANES = 128
HEAD_DIM = 64
ATTN_HEADS = 8
ATTN_WIDTH = ATTN_HEADS * HEAD_DIM
DILATIONS = (1, 4, 16)
WINDOWS = (128, 512, 2048)
N_GROUPS = len(DILATIONS)
QKV_WIDTH = N_GROUPS * ATTN_WIDTH
ATTN_BLOCK = 128
ROPE_THETA = 10000.0
NEG_INF = -1e30
SSM_GROUP = 16
SSM_GROUPS = 32
SSM_WIDTH = SSM_GROUP * SSM_GROUPS
SSM_STATE = 64
SSM_PAIRS = SSM_GROUPS // 2
LN_EPS = 1e-5

VMEM_LIMIT = 56 * 1024 * 1024


def _const_spec(shape):
    zeros = (0,) * len(shape)
    return pl.BlockSpec(shape, lambda *_: zeros, pipeline_mode=pl.Buffered(1))


def _phase_major(ref, tm, d):
    if d == 1:
        return ref[...]
    n = tm // d
    return jnp.concatenate([ref[pl.ds(r, n, stride=d), :] for r in range(d)], axis=0)


def _proj_kernel(*refs, tm, n_slabs):
    x_refs = refs[:n_slabs]
    cos_ref, sin_ref, wqkv_ref, wu_ref = refs[n_slabs:n_slabs + 4]
    o_refs = refs[n_slabs + 4:n_slabs + 4 + N_GROUPS]
    u_ref = refs[n_slabs + 4 + N_GROUPS]

    lane = lax.broadcasted_iota(jnp.int32, (1, LANES), 1)
    first_half = (lane % HEAD_DIM) < (HEAD_DIM // 2)

    for gi, d in enumerate(DILATIONS):
        n = tm // d
        xg = jnp.concatenate(
            [_phase_major(xr, tm, d).astype(BF16) for xr in x_refs], axis=1)
        cos = _phase_major(cos_ref, tm, d)
        sin = _phase_major(sin_ref, tm, d)
        if gi == 0:
            u_ref[...] = jnp.dot(xg, wu_ref[...], preferred_element_type=F32)
        for part in range(3):
            c0 = part * ATTN_WIDTH
            acc = jnp.dot(xg, wqkv_ref[gi, :, c0:c0 + ATTN_WIDTH],
                          preferred_element_type=F32)
            for c in range(ATTN_WIDTH // LANES):
                t = acc[:, c * LANES:(c + 1) * LANES]
                if part < 2:
                    rot = jnp.where(first_half,
                                    pltpu.roll(t, LANES - HEAD_DIM // 2, 1),
                                    pltpu.roll(t, HEAD_DIM // 2, 1))
                    t = t * cos + rot * sin
                if part == 0:
                    t = t * (1.0 / math.sqrt(HEAD_DIM))
                t = t.astype(BF16)
                col = c0 + c * LANES
                if d == 1:
                    o_refs[gi][:, col:col + LANES] = t
                else:
                    for r in range(d):
                        o_refs[gi][r, :, col:col + LANES] = t[r * n:(r + 1) * n]


def _project(x, cos_t, sin_t, wqkv, wu, tm=512):
    b, s, dm = x.shape
    n_slabs = dm // LANES
    grid = (b, s // tm)
    x_specs = [pl.BlockSpec((None, tm, LANES), functools.partial(lambda bi, i, c: (bi, i, c), c=c))
               for c in range(n_slabs)]
    tab_spec = pl.BlockSpec((tm, LANES), lambda bi, i: (i, 0))
    out_shapes = []
    out_specs = []
    for d in DILATIONS:
        if d == 1:
            out_shapes.append(jax.ShapeDtypeStruct((b, s, QKV_WIDTH), BF16))
            out_specs.append(pl.BlockSpec((None, tm, QKV_WIDTH), lambda bi, i: (bi, i, 0)))
        else:
            out_shapes.append(jax.ShapeDtypeStruct((b, d, s // d, QKV_WIDTH), BF16))
            out_specs.append(pl.BlockSpec((None, d, tm // d, QKV_WIDTH),
                                          lambda bi, i: (bi, 0, i, 0)))
    out_shapes.append(jax.ShapeDtypeStruct((b, s, SSM_WIDTH), F32))
    out_specs.append(pl.BlockSpec((None, tm, SSM_WIDTH), lambda bi, i: (bi, i, 0)))
    return pl.pallas_call(
        functools.partial(_proj_kernel, tm=tm, n_slabs=n_slabs),
        grid=grid,
        in_specs=x_specs + [tab_spec, tab_spec, _const_spec(wqkv.shape), _const_spec(wu.shape)],
        out_specs=out_specs,
        out_shape=out_shapes,
        compiler_params=pltpu.CompilerParams(
            dimension_semantics=("parallel", "parallel"), vmem_limit_bytes=VMEM_LIMIT),
    )(*([x] * n_slabs), cos_t, sin_t, wqkv, wu)


def _attn_kernel(*refs, s):
    qkv = refs[:3 * N_GROUPS]
    o_ref = refs[3 * N_GROUPS]
    acc_s, m_s, l_s = refs[3 * N_GROUPS + 1:]
    blk = ATTN_BLOCK

    lane = lax.broadcasted_iota(jnp.int32, (1, LANES), 1)
    head_a = lane < HEAD_DIM

    def attend(gi, d, row0, with_prev):
        q_ref, k_ref, v_ref = qkv[3 * gi:3 * gi + 3]
        n = s // d
        q = q_ref[pl.ds(row0, blk), :]
        if with_prev:
            k0 = pl.multiple_of(row0 - blk, blk)
            kw = k_ref[pl.ds(k0, 2 * blk), :]
            vw = v_ref[pl.ds(k0, 2 * blk), :]
        else:
            kw = k_ref[pl.ds(row0, blk), :]
            vw = v_ref[pl.ds(row0, blk), :]
        w = kw.shape[0]
        a_idx = lax.broadcasted_iota(jnp.int32, (blk, w), 0)
        c_idx = lax.broadcasted_iota(jnp.int32, (blk, w), 1)
        if with_prev:
            valid = (c_idx >= a_idx) & (c_idx <= a_idx + blk)
        else:
            valid = c_idx <= a_idx
        zero = jnp.zeros_like(q)
        outs, maxes, sums = [], [], []
        for qh in (jnp.where(head_a, q, zero), jnp.where(head_a, zero, q)):
            sc = lax.dot_general(qh, kw, (((1,), (1,)), ((), ())),
                                 preferred_element_type=F32)
            sc = jnp.where(valid, sc, NEG_INF)
            m = jnp.max(sc, axis=1, keepdims=True)
            p = jnp.exp(sc - m)
            sums.append(jnp.sum(p, axis=1, keepdims=True))
            maxes.append(m)
            outs.append(jnp.dot(p.astype(BF16), vw, preferred_element_type=F32))
        o = jnp.where(head_a, outs[0], outs[1])
        m = jnp.where(head_a, maxes[0], maxes[1])
        l = jnp.where(head_a, sums[0], sums[1])
        if d == 1:
            rows = pl.ds(row0, blk)
        else:
            r = row0 // n
            m0 = row0 - r * n
            rows = pl.ds(m0 * d + r, blk, stride=d)
        acc_s[gi, rows, :] = o
        m_s[gi, rows, :] = m
        l_s[gi, rows, :] = l

    for gi, d in enumerate(DILATIONS):
        n = s // d
        nblk = n // blk

        def first_blocks(seg, carry, gi=gi, d=d, n=n):
            attend(gi, d, pl.multiple_of(seg * n, blk), False)
            return carry
        lax.fori_loop(0, d, first_blocks, 0)

        if nblk > 1:
            def later_blocks(i, carry, gi=gi, d=d, n=n, nblk=nblk):
                seg = i // (nblk - 1)
                bi = i - seg * (nblk - 1) + 1
                attend(gi, d, pl.multiple_of(seg * n + bi * blk, blk), True)
                return carry
            lax.fori_loop(0, d * (nblk - 1), later_blocks, 0)

    chunk = 256

    def combine(i, carry):
        rows = pl.ds(pl.multiple_of(i * chunk, chunk), chunk)
        ms = [m_s[gi, rows, :] for gi in range(N_GROUPS)]
        m_all = jnp.maximum(jnp.maximum(ms[0], ms[1]), ms[2])
        num = jnp.zeros((chunk, LANES), F32)
        den = jnp.zeros((chunk, LANES), F32)
        for gi in range(N_GROUPS):
            wgt = jnp.exp(ms[gi] - m_all)
            num = num + wgt * acc_s[gi, rows, :]
            den = den + wgt * l_s[gi, rows, :]
        o_ref[rows, :] = (num / den).astype(o_ref.dtype)
        return carry
    lax.fori_loop(0, s // chunk, combine, 0)


def _attention(qkv_groups, b, s):
    n_hp = ATTN_WIDTH // LANES
    in_specs, args = [], []
    for gi, d in enumerate(DILATIONS):
        arr = qkv_groups[gi].reshape(b, s, QKV_WIDTH)
        for part in range(3):
            in_specs.append(pl.BlockSpec(
                (None, s, LANES),
                functools.partial(lambda bi, hp, part: (bi, 0, part * n_hp + hp), part=part)))
            args.append(arr)
    return pl.pallas_call(
        functools.partial(_attn_kernel, s=s),
        grid=(b, n_hp),
        in_specs=in_specs,
        out_specs=pl.BlockSpec((None, s, LANES), lambda bi, hp: (bi, 0, hp)),
        out_shape=jax.ShapeDtypeStruct((b, s, ATTN_WIDTH), BF16),
        scratch_shapes=[pltpu.VMEM((N_GROUPS, s, LANES), F32)] * 3,
        compiler_params=pltpu.CompilerParams(
            dimension_semantics=("parallel", "parallel"), vmem_limit_bytes=VMEM_LIMIT),
    )(*args)


def _gelu_tanh(x):
    return 0.5 * x * (1.0 + jnp.tanh(math.sqrt(2.0 / math.pi) * (x + 0.044715 * (x * x * x))))


def _ssm_kernel(u_ref, bm_ref, are_ref, aim_ref, cm_ref, dskip_ref, o_ref,
                sre, sim, h_ref, *, tc, bsz):
    rows = tc * bsz
    pairs_per_slab = LANES // (2 * SSM_GROUP)

    @pl.when(pl.program_id(0) == 0)
    def _():
        h_ref[...] = jnp.zeros_like(h_ref)

    u = u_ref[...]
    ub = u.astype(BF16)
    for j in range(SSM_PAIRS):
        k0 = (j // pairs_per_slab) * LANES
        bu = jnp.dot(ub[:, k0:k0 + LANES], bm_ref[j], preferred_element_type=F32)
        sre[j] = bu[:, :LANES]
        sim[j] = bu[:, LANES:]

    half = SSM_PAIRS // 2
    for jb in range(2):
        js = list(range(jb * half, (jb + 1) * half))
        a_re = [are_ref[j] for j in js]
        a_im = [aim_ref[j] for j in js]

        def step(t, carry, js=js, a_re=a_re, a_im=a_im):
            h_re, h_im = carry
            r0 = pl.multiple_of(t * bsz, bsz)
            new_re, new_im = [], []
            for i, j in enumerate(js):
                n_re = a_re[i] * h_re[i] - a_im[i] * h_im[i] + sre[j, pl.ds(r0, bsz), :]
                n_im = a_re[i] * h_im[i] + a_im[i] * h_re[i] + sim[j, pl.ds(r0, bsz), :]
                sre[j, pl.ds(r0, bsz), :] = n_re
                sim[j, pl.ds(r0, bsz), :] = n_im
                new_re.append(n_re)
                new_im.append(n_im)
            return tuple(new_re), tuple(new_im)

        init = (tuple(h_ref[0, j] for j in js), tuple(h_ref[1, j] for j in js))
        h_re, h_im = lax.fori_loop(0, tc, step, init)
        for i, j in enumerate(js):
            h_ref[0, j] = h_re[i]
            h_ref[1, j] = h_im[i]

    for qb in range(SSM_WIDTH // LANES):
        y = jnp.zeros((rows, LANES), F32)
        for j in range(qb * pairs_per_slab, (qb + 1) * pairs_per_slab):
            st = jnp.concatenate([sre[j].astype(BF16), sim[j].astype(BF16)], axis=1)
            y = y + jnp.dot(st, cm_ref[j], preferred_element_type=F32)
        cols = slice(qb * LANES, (qb + 1) * LANES)
        y = y + dskip_ref[:, cols] * u[:, cols]
        o_ref[:, cols] = _gelu_tanh(y).astype(o_ref.dtype)


def _ssm(u_tm, bm, a_re, a_im, cm, d_skip, bsz, tc=128):
    rows_total = u_tm.shape[0]
    s = rows_total // bsz
    rows = tc * bsz
    return pl.pallas_call(
        functools.partial(_ssm_kernel, tc=tc, bsz=bsz),
        grid=(s // tc,),
        in_specs=[pl.BlockSpec((rows, SSM_WIDTH), lambda i: (i, 0)),
                  _const_spec(bm.shape), _const_spec(a_re.shape), _const_spec(a_im.shape),
                  _const_spec(cm.shape), _const_spec(d_skip.shape)],
        out_specs=pl.BlockSpec((rows, SSM_WIDTH), lambda i: (i, 0)),
        out_shape=jax.ShapeDtypeStruct((rows_total, SSM_WIDTH), BF16),
        scratch_shapes=[pltpu.VMEM((SSM_PAIRS, rows, LANES), F32),
                        pltpu.VMEM((SSM_PAIRS, rows, LANES), F32),
                        pltpu.VMEM((2, SSM_PAIRS, bsz, LANES), F32)],
        compiler_params=pltpu.CompilerParams(
            dimension_semantics=("arbitrary",), vmem_limit_bytes=VMEM_LIMIT),
    )(u_tm, bm, a_re, a_im, cm, d_skip)


def _ssm_matrices(a_re, a_im, log_dt, b_re, b_im, c_re, c_im, bsz):
    g, p, h = SSM_GROUPS, SSM_STATE, SSM_GROUP
    lam = lax.complex(a_re.astype(F32), a_im.astype(F32))
    dt = jnp.exp(log_dt.astype(F32))[:, None]
    a_bar = jnp.exp(lam * dt)
    b_bar = ((a_bar - 1.0) / lam)[..., None] * lax.complex(b_re.astype(F32), b_im.astype(F32))
    groups_per_slab = LANES // h
    pairs_per_slab = groups_per_slab // 2
    eye = jnp.eye(groups_per_slab, dtype=F32)

    def in_matrix(bpart):
        bt = jnp.transpose(bpart, (0, 2, 1)).reshape(g // groups_per_slab, groups_per_slab, h, p)
        dense = bt[:, :, :, None, :] * eye[None, :, None, :, None]
        dense = dense.reshape(g // groups_per_slab, LANES, pairs_per_slab, 2 * p)
        return jnp.transpose(dense, (0, 2, 1, 3)).reshape(SSM_PAIRS, LANES, 2 * p)

    bm = jnp.concatenate([in_matrix(b_bar.real), in_matrix(b_bar.imag)], axis=2)

    def out_matrix(cpart):
        ct = jnp.transpose(cpart, (0, 2, 1)).reshape(g // groups_per_slab, groups_per_slab, p, h)
        dense = ct[:, :, :, None, :] * eye[None, :, None, :, None]
        dense = dense.reshape(g // groups_per_slab, pairs_per_slab, 2 * p, LANES)
        return dense.reshape(SSM_PAIRS, 2 * p, LANES)

    cm = jnp.concatenate([out_matrix(c_re.astype(F32)), out_matrix(-c_im.astype(F32))], axis=1)

    def bcast(apart):
        return jnp.broadcast_to(apart.reshape(SSM_PAIRS, 1, 2 * p), (SSM_PAIRS, bsz, 2 * p))

    return bm.astype(BF16), bcast(a_bar.real), bcast(a_bar.imag), cm.astype(BF16)


def _layer_norm(z, g, b):
    mu = jnp.mean(z, axis=-1, keepdims=True)
    zc = z - mu
    var = jnp.mean(zc * zc, axis=-1, keepdims=True)
    return zc * lax.rsqrt(var + LN_EPS) * g + b


def _mix_kernel(x_ref, attn_ref, g_ref, wgate_ref, bgate_ref, wattn_ref, wglu_ref,
                wssm_ref, wout_ref, lng_ref, lnb_ref, o_ref, *, alpha):
    dm = x_ref.shape[1]
    x = x_ref[...]
    logits = jnp.dot(x.astype(BF16), wgate_ref[...], preferred_element_type=F32) + bgate_ref[...]
    gates = jax.nn.sigmoid(logits)
    y_attn = jnp.dot(attn_ref[...], wattn_ref[...], preferred_element_type=F32)
    glu = jnp.dot(g_ref[...], wglu_ref[...], preferred_element_type=F32)
    y_s = glu[:, :SSM_WIDTH] * jax.nn.sigmoid(glu[:, SSM_WIDTH:])
    y_ssm = jnp.dot(y_s.astype(BF16), wssm_ref[...], preferred_element_type=F32)
    mixed = gates[:, :dm] * y_attn + gates[:, dm:] * y_ssm
    mix_out = jnp.dot(mixed.astype(BF16), wout_ref[...], preferred_element_type=F32)
    o_ref[...] = _layer_norm(alpha * x + mix_out, lng_ref[...], lnb_ref[...])


def _mix(x2, attn2, g2, wgate, bgate, wattn, wglu, wssm, wout, lng, lnb, alpha, tm=512):
    rows, dm = x2.shape
    row_spec = lambda w: pl.BlockSpec((tm, w), lambda i: (i, 0))
    consts = [wgate, bgate, wattn, wglu, wssm, wout, lng, lnb]
    return pl.pallas_call(
        functools.partial(_mix_kernel, alpha=alpha),
        grid=(rows // tm,),
        in_specs=[row_spec(dm), row_spec(ATTN_WIDTH), row_spec(SSM_WIDTH)]
                 + [_const_spec(c.shape) for c in consts],
        out_specs=row_spec(dm),
        out_shape=jax.ShapeDtypeStruct((rows, dm), F32),
        compiler_params=pltpu.CompilerParams(
            dimension_semantics=("parallel",), vmem_limit_bytes=VMEM_LIMIT),
    )(x2, attn2, g2, *consts)


def _ffn_kernel(h_ref, wg_ref, wu_ref, wd_ref, lng_ref, lnb_ref, o_ref, *, alpha):
    h = h_ref[...]
    hb = h.astype(BF16)
    gate = jnp.dot(hb, wg_ref[...], preferred_element_type=F32)
    up = jnp.dot(hb, wu_ref[...], preferred_element_type=F32)
    act = (gate * jax.nn.sigmoid(gate)) * up
    ff = jnp.dot(act.astype(BF16), wd_ref[...], preferred_element_type=F32)
    o_ref[...] = _layer_norm(alpha * h + ff, lng_ref[...], lnb_ref[...])


def _ffn(h2, wg, wu, wd, lng, lnb, alpha, tm=512):
    rows, dm = h2.shape
    row_spec = pl.BlockSpec((tm, dm), lambda i: (i, 0))
    consts = [wg, wu, wd, lng, lnb]
    return pl.pallas_call(
        functools.partial(_ffn_kernel, alpha=alpha),
        grid=(rows // tm,),
        in_specs=[row_spec] + [_const_spec(c.shape) for c in consts],
        out_specs=row_spec,
        out_shape=jax.ShapeDtypeStruct((rows, dm), F32),
        compiler_params=pltpu.CompilerParams(
            dimension_semantics=("parallel",), vmem_limit_bytes=VMEM_LIMIT),
    )(h2, *consts)


def _rope_tables(s):
    half = HEAD_DIM // 2
    pos = jnp.arange(s, dtype=F32)
    inv_freq = ROPE_THETA ** (-jnp.arange(half, dtype=F32) / half)
    ang = pos[:, None] * inv_freq[None, :]
    cos, sin = jnp.cos(ang), jnp.sin(ang)
    reps = LANES // HEAD_DIM
    cos_t = jnp.concatenate([cos, cos] * reps, axis=1)
    sin_t = jnp.concatenate([-sin, sin] * reps, axis=1)
    return cos_t, sin_t


def kernel(x, w_in, b_gate, w_attn_br, w_ssm_br, w_out, ssm_a_re, ssm_a_im, ssm_log_dt, ssm_b_re, ssm_b_im, ssm_c_re, ssm_c_im, ssm_d, w_glu, ln1_g, ln1_b, w_ff_gate, w_ff_up, w_ff_down, ln2_g, ln2_b):
    bsz, s, dm = x.shape
    depth = w_in.shape[0]
    alpha = (2.0 * depth) ** 0.25
    assert all(w // d == ATTN_BLOCK for w, d in zip(WINDOWS, DILATIONS))
    assert s % (ATTN_BLOCK * max(DILATIONS)) == 0
    cos_t, sin_t = _rope_tables(s)
    u0 = 3 * QKV_WIDTH

    for layer in range(depth):
        w = w_in[layer]
        wqkv = jnp.stack([
            jnp.concatenate([w[:, part * QKV_WIDTH + gi * ATTN_WIDTH:
                               part * QKV_WIDTH + (gi + 1) * ATTN_WIDTH] for part in range(3)], axis=1)
            for gi in range(N_GROUPS)]).astype(BF16)
        wu = w[:, u0:u0 + SSM_WIDTH].astype(BF16)
        wgate = w[:, u0 + SSM_WIDTH:].astype(BF16)

        outs = _project(x, cos_t, sin_t, wqkv, wu)
        attn = _attention(outs[:N_GROUPS], bsz, s)

        u_tm = jnp.transpose(outs[N_GROUPS], (1, 0, 2)).reshape(s * bsz, SSM_WIDTH)
        bm, a_re, a_im, cm = _ssm_matrices(
            ssm_a_re[layer], ssm_a_im[layer], ssm_log_dt[layer], ssm_b_re[layer],
            ssm_b_im[layer], ssm_c_re[layer], ssm_c_im[layer], bsz)
        g_tm = _ssm(u_tm, bm, a_re, a_im, cm, ssm_d[layer].reshape(1, SSM_WIDTH).astype(F32), bsz)
        g = jnp.transpose(g_tm.reshape(s, bsz, SSM_WIDTH), (1, 0, 2))

        h = _mix(x.reshape(bsz * s, dm), attn.reshape(bsz * s, ATTN_WIDTH),
                 g.reshape(bsz * s, SSM_WIDTH), wgate,
                 b_gate[layer].reshape(1, 2 * dm).astype(F32),
                 w_attn_br[layer].astype(BF16), w_glu[layer].astype(BF16),
                 w_ssm_br[layer].astype(BF16), w_out[layer].astype(BF16),
                 ln1_g[layer].reshape(1, dm).astype(F32), ln1_b[layer].reshape(1, dm).astype(F32),
                 alpha)
        x2 = _ffn(h, w_ff_gate[layer].astype(BF16), w_ff_up[layer].astype(BF16),
                  w_ff_down[layer].astype(BF16),
                  ln2_g[layer].reshape(1, dm).astype(F32), ln2_b[layer].reshape(1, dm).astype(F32),
                  alpha)
        x = x2.reshape(bsz, s, dm)
    return x
```

```python
import functools
import math

import jax
import jax.numpy as jnp
from jax import lax
from jax.experimental import pallas as pl
from jax.experimental.pallas import tpu as pltpu

F32 = jnp.float32
BF16 = jnp.bfloat16

LANES = 128
HEAD_DIM = 64
ATTN_HEADS = 8
ATTN_WIDTH = ATTN_HEADS * HEAD_DIM
DILATIONS = (1, 4, 16)
WINDOWS = (128, 512, 2048)
N_GROUPS = len(DILATIONS)
QKV_WIDTH = N_GROUPS * ATTN_WIDTH
ATTN_BLOCK = 128
ROPE_THETA = 10000.0
NEG_INF = -1e30
SSM_GROUP = 16
SSM_GROUPS = 32
SSM_WIDTH = SSM_GROUP * SSM_GROUPS
SSM_STATE = 64
SSM_PAIRS = SSM_GROUPS // 2
LN_EPS = 1e-5

VMEM_LIMIT = 56 * 1024 * 1024


def _const_spec(shape):
    zeros = (0,) * len(shape)
    return pl.BlockSpec(shape, lambda *_: zeros, pipeline_mode=pl.Buffered(1))


def _phase_major(ref, tm, d):
    if d == 1:
        return ref[...]
    n = tm // d
    return jnp.concatenate([ref[pl.ds(r, n, stride=d), :] for r in range(d)], axis=0)


def _proj_kernel(*refs, tm, n_slabs):
    x_refs = refs[:n_slabs]
    cos_ref, sin_ref, wqkv_ref, wu_ref = refs[n_slabs:n_slabs + 4]
    o_refs = refs[n_slabs + 4:n_slabs + 4 + N_GROUPS]
    u_ref = refs[n_slabs + 4 + N_GROUPS]

    lane = lax.broadcasted_iota(jnp.int32, (1, LANES), 1)
    first_half = (lane % HEAD_DIM) < (HEAD_DIM // 2)

    for gi, d in enumerate(DILATIONS):
        n = tm // d
        xg = jnp.concatenate(
            [_phase_major(xr, tm, d).astype(BF16) for xr in x_refs], axis=1)
        cos = _phase_major(cos_ref, tm, d)
        sin = _phase_major(sin_ref, tm, d)
        if gi == 0:
            u_ref[...] = jnp.dot(xg, wu_ref[...], preferred_element_type=F32)
        for part in range(3):
            c0 = part * ATTN_WIDTH
            acc = jnp.dot(xg, wqkv_ref[gi, :, c0:c0 + ATTN_WIDTH],
                          preferred_element_type=F32)
            for c in range(ATTN_WIDTH // LANES):
                t = acc[:, c * LANES:(c + 1) * LANES]
                if part < 2:
                    rot = jnp.where(first_half,
                                    pltpu.roll(t, LANES - HEAD_DIM // 2, 1),
                                    pltpu.roll(t, HEAD_DIM // 2, 1))
                    t = t * cos + rot * sin
                if part == 0:
                    t = t * (1.0 / math.sqrt(HEAD_DIM))
                t = t.astype(BF16)
                col = c0 + c * LANES
                if d == 1:
                    o_refs[gi][:, col:col + LANES] = t
                else:
                    for r in range(d):
                        o_refs[gi][r, :, col:col + LANES] = t[r * n:(r + 1) * n]


def _project(x, cos_t, sin_t, wqkv, wu, tm=512):
    b, s, dm = x.shape
    n_slabs = dm // LANES
    grid = (b, s // tm)
    x_specs = [pl.BlockSpec((None, tm, LANES), functools.partial(lambda bi, i, c: (bi, i, c), c=c))
               for c in range(n_slabs)]
    tab_spec = pl.BlockSpec((tm, LANES), lambda bi, i: (i, 0))
    out_shapes = []
    out_specs = []
    for d in DILATIONS:
        if d == 1:
            out_shapes.append(jax.ShapeDtypeStruct((b, s, QKV_WIDTH), BF16))
            out_specs.append(pl.BlockSpec((None, tm, QKV_WIDTH), lambda bi, i: (bi, i, 0)))
        else:
            out_shapes.append(jax.ShapeDtypeStruct((b, d, s // d, QKV_WIDTH), BF16))
            out_specs.append(pl.BlockSpec((None, d, tm // d, QKV_WIDTH),
                                          lambda bi, i: (bi, 0, i, 0)))
    out_shapes.append(jax.ShapeDtypeStruct((b, s, SSM_WIDTH), F32))
    out_specs.append(pl.BlockSpec((None, tm, SSM_WIDTH), lambda bi, i: (bi, i, 0)))
    return pl.pallas_call(
        functools.partial(_proj_kernel, tm=tm, n_slabs=n_slabs),
        grid=grid,
        in_specs=x_specs + [tab_spec, tab_spec, _const_spec(wqkv.shape), _const_spec(wu.shape)],
        out_specs=out_specs,
        out_shape=out_shapes,
        compiler_params=pltpu.CompilerParams(
            dimension_semantics=("parallel", "parallel"), vmem_limit_bytes=VMEM_LIMIT),
        name="in_proj",
    )(*([x] * n_slabs), cos_t, sin_t, wqkv, wu)


def _attn_bias():
    blk = ATTN_BLOCK
    a = (jnp.arange(2 * blk) % blk)[:, None]
    c = jnp.arange(2 * blk)[None, :]
    cur = (c >= blk) & (c - blk <= a)
    prev = (c < blk) & (c >= a)
    neg = jnp.float32(NEG_INF)
    windowed = jnp.stack([jnp.where(cur, 0.0, neg), jnp.where(cur | prev, 0.0, neg)])
    causal = jnp.where(c[:, :blk] <= a, 0.0, neg)
    return windowed.astype(F32), causal.astype(F32)


def _attn_kernel(*refs, s, unroll):
    qkv = refs[:3 * N_GROUPS]
    bias_w_ref, bias_c_ref = refs[3 * N_GROUPS:3 * N_GROUPS + 2]
    o_ref = refs[3 * N_GROUPS + 2]
    acc_s, m_s, l_s = refs[3 * N_GROUPS + 3:]
    blk = ATTN_BLOCK

    lane = lax.broadcasted_iota(jnp.int32, (1, LANES), 1)
    head_a = lane < HEAD_DIM

    def attend(gi, d, row0, has_prev):
        q_ref, k_ref, v_ref = qkv[3 * gi:3 * gi + 3]
        n = s // d
        q = q_ref[pl.ds(row0, blk), :]
        zero = jnp.zeros_like(q)
        q2 = jnp.concatenate([jnp.where(head_a, q, zero), jnp.where(head_a, zero, q)], axis=0)
        kw = k_ref[pl.ds(row0, blk), :]
        vw = v_ref[pl.ds(row0, blk), :]
        if has_prev is None:
            bias = bias_c_ref[...]
        else:
            p0 = pl.multiple_of(jnp.maximum(row0 - blk, 0), blk)
            kw = jnp.concatenate([k_ref[pl.ds(p0, blk), :], kw], axis=0)
            vw = jnp.concatenate([v_ref[pl.ds(p0, blk), :], vw], axis=0)
            bias = bias_w_ref[has_prev]
        sc = lax.dot_general(q2, kw, (((1,), (1,)), ((), ())),
                             preferred_element_type=F32) + bias
        m = jnp.max(sc, axis=1, keepdims=True)
        p = jnp.exp(sc - m).astype(BF16)
        v_ext = jnp.concatenate([vw, jnp.ones_like(vw)], axis=1)
        res = jnp.dot(p, v_ext, preferred_element_type=F32)
        o = jnp.where(head_a, res[:blk, :LANES], res[blk:, :LANES])
        l = jnp.where(head_a, res[:blk, LANES:], res[blk:, LANES:])
        m = jnp.where(head_a, m[:blk], m[blk:])
        if d == 1:
            rows = pl.ds(row0, blk)
        else:
            r = row0 // n
            m0 = row0 - r * n
            rows = pl.ds(m0 * d + r, blk, stride=d)
        acc_s[gi, rows, :] = o
        m_s[gi, rows, :] = m
        l_s[gi, rows, :] = l

    for gi, d in enumerate(DILATIONS):
        nblk = (s // d) // blk

        def body(i, carry, gi=gi, d=d, nblk=nblk):
            row0 = pl.multiple_of(i * blk, blk)
            has_prev = None if nblk == 1 else (i % nblk != 0).astype(jnp.int32)
            attend(gi, d, row0, has_prev)
            return carry
        lax.fori_loop(0, s // blk, body, 0, unroll=unroll)

    chunk = 256

    def combine(i, carry):
        rows = pl.ds(pl.multiple_of(i * chunk, chunk), chunk)
        ms = [m_s[gi, rows, :] for gi in range(N_GROUPS)]
        m_all = jnp.maximum(jnp.maximum(ms[0], ms[1]), ms[2])
        num = jnp.zeros((chunk, LANES), F32)
        den = jnp.zeros((chunk, LANES), F32)
        for gi in range(N_GROUPS):
            wgt = jnp.exp(ms[gi] - m_all)
            num = num + wgt * acc_s[gi, rows, :]
            den = den + wgt * l_s[gi, rows, :]
        o_ref[rows, :] = (num / den).astype(o_ref.dtype)
        return carry
    lax.fori_loop(0, s // chunk, combine, 0)


def _attention(qkv_groups, b, s, unroll=8):
    n_hp = ATTN_WIDTH // LANES
    in_specs, args = [], []
    for gi, d in enumerate(DILATIONS):
        arr = qkv_groups[gi].reshape(b, s, QKV_WIDTH)
        for part in range(3):
            in_specs.append(pl.BlockSpec(
                (None, s, LANES),
                functools.partial(lambda bi, hp, part: (bi, 0, part * n_hp + hp), part=part)))
            args.append(arr)
    bias_w, bias_c = _attn_bias()
    return pl.pallas_call(
        functools.partial(_attn_kernel, s=s, unroll=unroll),
        grid=(b, n_hp),
        in_specs=in_specs + [_const_spec(bias_w.shape), _const_spec(bias_c.shape)],
        out_specs=pl.BlockSpec((None, s, LANES), lambda bi, hp: (bi, 0, hp)),
        out_shape=jax.ShapeDtypeStruct((b, s, ATTN_WIDTH), BF16),
        scratch_shapes=[pltpu.VMEM((N_GROUPS, s, LANES), F32)] * 3,
        compiler_params=pltpu.CompilerParams(
            dimension_semantics=("parallel", "parallel"), vmem_limit_bytes=VMEM_LIMIT),
        name="attention",
    )(*args, bias_w, bias_c)


def _gelu_tanh(x):
    return 0.5 * x * (1.0 + jnp.tanh(math.sqrt(2.0 / math.pi) * (x + 0.044715 * (x * x * x))))


def _ssm_kernel(u_ref, bm_ref, are_ref, aim_ref, cm_ref, dskip_ref, o_ref,
                sre, sim, h_ref, *, tc, bsz):
    rows = tc * bsz
    pairs_per_slab = LANES // (2 * SSM_GROUP)

    @pl.when(pl.program_id(0) == 0)
    def _():
        h_ref[...] = jnp.zeros_like(h_ref)

    u = u_ref[...]
    ub = u.astype(BF16)
    for j in range(SSM_PAIRS):
        k0 = (j // pairs_per_slab) * LANES
        bu = jnp.dot(ub[:, k0:k0 + LANES], bm_ref[j], preferred_element_type=F32)
        sre[j] = bu[:, :LANES]
        sim[j] = bu[:, LANES:]

    half = SSM_PAIRS // 2
    for jb in range(2):
        js = list(range(jb * half, (jb + 1) * half))
        a_re = [are_ref[j] for j in js]
        a_im = [aim_ref[j] for j in js]

        def step(t, carry, js=js, a_re=a_re, a_im=a_im):
            h_re, h_im = carry
            r0 = pl.multiple_of(t * bsz, bsz)
            new_re, new_im = [], []
            for i, j in enumerate(js):
                n_re = a_re[i] * h_re[i] - a_im[i] * h_im[i] + sre[j, pl.ds(r0, bsz), :]
                n_im = a_re[i] * h_im[i] + a_im[i] * h_re[i] + sim[j, pl.ds(r0, bsz), :]
                sre[j, pl.ds(r0, bsz), :] = n_re
                sim[j, pl.ds(r0, bsz), :] = n_im
                new_re.append(n_re)
                new_im.append(n_im)
            return tuple(new_re), tuple(new_im)

        init = (tuple(h_ref[0, j] for j in js), tuple(h_ref[1, j] for j in js))
        h_re, h_im = lax.fori_loop(0, tc, step, init)
        for i, j in enumerate(js):
            h_ref[0, j] = h_re[i]
            h_ref[1, j] = h_im[i]

    for qb in range(SSM_WIDTH // LANES):
        y = jnp.zeros((rows, LANES), F32)
        for j in range(qb * pairs_per_slab, (qb + 1) * pairs_per_slab):
            st = jnp.concatenate([sre[j].astype(BF16), sim[j].astype(BF16)], axis=1)
            y = y + jnp.dot(st, cm_ref[j], preferred_element_type=F32)
        cols = slice(qb * LANES, (qb + 1) * LANES)
        y = y + dskip_ref[:, cols] * u[:, cols]
        o_ref[:, cols] = _gelu_tanh(y).astype(o_ref.dtype)


def _ssm(u_tm, bm, a_re, a_im, cm, d_skip, bsz, tc=128):
    rows_total = u_tm.shape[0]
    s = rows_total // bsz
    rows = tc * bsz
    return pl.pallas_call(
        functools.partial(_ssm_kernel, tc=tc, bsz=bsz),
        grid=(s // tc,),
        in_specs=[pl.BlockSpec((rows, SSM_WIDTH), lambda i: (i, 0)),
                  _const_spec(bm.shape), _const_spec(a_re.shape), _const_spec(a_im.shape),
                  _const_spec(cm.shape), _const_spec(d_skip.shape)],
        out_specs=pl.BlockSpec((rows, SSM_WIDTH), lambda i: (i, 0)),
        out_shape=jax.ShapeDtypeStruct((rows_total, SSM_WIDTH), BF16),
        scratch_shapes=[pltpu.VMEM((SSM_PAIRS, rows, LANES), F32),
                        pltpu.VMEM((SSM_PAIRS, rows, LANES), F32),
                        pltpu.VMEM((2, SSM_PAIRS, bsz, LANES), F32)],
        compiler_params=pltpu.CompilerParams(
            dimension_semantics=("arbitrary",), vmem_limit_bytes=VMEM_LIMIT),
        name="ssm",
    )(u_tm, bm, a_re, a_im, cm, d_skip)


def _ssm_matrices(a_re, a_im, log_dt, b_re, b_im, c_re, c_im, bsz):
    g, p, h = SSM_GROUPS, SSM_STATE, SSM_GROUP
    lam = lax.complex(a_re.astype(F32), a_im.astype(F32))
    dt = jnp.exp(log_dt.astype(F32))[:, None]
    a_bar = jnp.exp(lam * dt)
    b_bar = ((a_bar - 1.0) / lam)[..., None] * lax.complex(b_re.astype(F32), b_im.astype(F32))
    groups_per_slab = LANES // h
    pairs_per_slab = groups_per_slab // 2
    eye = jnp.eye(groups_per_slab, dtype=F32)

    def in_matrix(bpart):
        bt = jnp.transpose(bpart, (0, 2, 1)).reshape(g // groups_per_slab, groups_per_slab, h, p)
        dense = bt[:, :, :, None, :] * eye[None, :, None, :, None]
        dense = dense.reshape(g // groups_per_slab, LANES, pairs_per_slab, 2 * p)
        return jnp.transpose(dense, (0, 2, 1, 3)).reshape(SSM_PAIRS, LANES, 2 * p)

    bm = jnp.concatenate([in_matrix(b_bar.real), in_matrix(b_bar.imag)], axis=2)

    def out_matrix(cpart):
        ct = jnp.transpose(cpart, (0, 2, 1)).reshape(g // groups_per_slab, groups_per_slab, p, h)
        dense = ct[:, :, :, None, :] * eye[None, :, None, :, None]
        dense = dense.reshape(g // groups_per_slab, pairs_per_slab, 2 * p, LANES)
        return dense.reshape(SSM_PAIRS, 2 * p, LANES)

    cm = jnp.concatenate([out_matrix(c_re.astype(F32)), out_matrix(-c_im.astype(F32))], axis=1)

    def bcast(apart):
        return jnp.broadcast_to(apart.reshape(SSM_PAIRS, 1, 2 * p), (SSM_PAIRS, bsz, 2 * p))

    return bm.astype(BF16), bcast(a_bar.real), bcast(a_bar.imag), cm.astype(BF16)


def _layer_norm(z, g, b):
    mu = jnp.mean(z, axis=-1, keepdims=True)
    zc = z - mu
    var = jnp.mean(zc * zc, axis=-1, keepdims=True)
    return zc * lax.rsqrt(var + LN_EPS) * g + b


def _mix_kernel(x_ref, attn_ref, g_ref, wgate_ref, bgate_ref, wattn_ref, wglu_ref,
                wssm_ref, wout_ref, lng_ref, lnb_ref, o_ref, *, alpha):
    dm = x_ref.shape[1]
    x = x_ref[...]
    logits = jnp.dot(x.astype(BF16), wgate_ref[...], preferred_element_type=F32) + bgate_ref[...]
    gates = jax.nn.sigmoid(logits)
    y_attn = jnp.dot(attn_ref[...], wattn_ref[...], preferred_element_type=F32)
    glu = jnp.dot(g_ref[...], wglu_ref[...], preferred_element_type=F32)
    y_s = glu[:, :SSM_WIDTH] * jax.nn.sigmoid(glu[:, SSM_WIDTH:])
    y_ssm = jnp.dot(y_s.astype(BF16), wssm_ref[...], preferred_element_type=F32)
    mixed = gates[:, :dm] * y_attn + gates[:, dm:] * y_ssm
    mix_out = jnp.dot(mixed.astype(BF16), wout_ref[...], preferred_element_type=F32)
    o_ref[...] = _layer_norm(alpha * x + mix_out, lng_ref[...], lnb_ref[...])


def _mix(x2, attn2, g2, wgate, bgate, wattn, wglu, wssm, wout, lng, lnb, alpha, tm=512):
    rows, dm = x2.shape
    row_spec = lambda w: pl.BlockSpec((tm, w), lambda i: (i, 0))
    consts = [wgate, bgate, wattn, wglu, wssm, wout, lng, lnb]
    return pl.pallas_call(
        functools.partial(_mix_kernel, alpha=alpha),
        grid=(rows // tm,),
        in_specs=[row_spec(dm), row_spec(ATTN_WIDTH), row_spec(SSM_WIDTH)]
                 + [_const_spec(c.shape) for c in consts],
        out_specs=row_spec(dm),
        out_shape=jax.ShapeDtypeStruct((rows, dm), F32),
        compiler_params=pltpu.CompilerParams(
            dimension_semantics=("parallel",), vmem_limit_bytes=VMEM_LIMIT),
        name="mix",
    )(x2, attn2, g2, *consts)


def _ffn_kernel(h_ref, wg_ref, wu_ref, wd_ref, lng_ref, lnb_ref, o_ref, *, alpha):
    h = h_ref[...]
    hb = h.astype(BF16)
    gate = jnp.dot(hb, wg_ref[...], preferred_element_type=F32)
    up = jnp.dot(hb, wu_ref[...], preferred_element_type=F32)
    act = (gate * jax.nn.sigmoid(gate)) * up
    ff = jnp.dot(act.astype(BF16), wd_ref[...], preferred_element_type=F32)
    o_ref[...] = _layer_norm(alpha * h + ff, lng_ref[...], lnb_ref[...])


def _ffn(h2, wg, wu, wd, lng, lnb, alpha, tm=512):
    rows, dm = h2.shape
    row_spec = pl.BlockSpec((tm, dm), lambda i: (i, 0))
    consts = [wg, wu, wd, lng, lnb]
    return pl.pallas_call(
        functools.partial(_ffn_kernel, alpha=alpha),
        grid=(rows // tm,),
        in_specs=[row_spec] + [_const_spec(c.shape) for c in consts],
        out_specs=row_spec,
        out_shape=jax.ShapeDtypeStruct((rows, dm), F32),
        compiler_params=pltpu.CompilerParams(
            dimension_semantics=("parallel",), vmem_limit_bytes=VMEM_LIMIT),
        name="ffn",
    )(h2, *consts)


def _rope_tables(s):
    half = HEAD_DIM // 2
    pos = jnp.arange(s, dtype=F32)
    inv_freq = ROPE_THETA ** (-jnp.arange(half, dtype=F32) / half)
    ang = pos[:, None] * inv_freq[None, :]
    cos, sin = jnp.cos(ang), jnp.sin(ang)
    reps = LANES // HEAD_DIM
    cos_t = jnp.concatenate([cos, cos] * reps, axis=1)
    sin_t = jnp.concatenate([-sin, sin] * reps, axis=1)
    return cos_t, sin_t


def kernel(x, w_in, b_gate, w_attn_br, w_ssm_br, w_out, ssm_a_re, ssm_a_im, ssm_log_dt, ssm_b_re, ssm_b_im, ssm_c_re, ssm_c_im, ssm_d, w_glu, ln1_g, ln1_b, w_ff_gate, w_ff_up, w_ff_down, ln2_g, ln2_b):
    bsz, s, dm = x.shape
    depth = w_in.shape[0]
    alpha = (2.0 * depth) ** 0.25
    assert all(w // d == ATTN_BLOCK for w, d in zip(WINDOWS, DILATIONS))
    assert s % (ATTN_BLOCK * max(DILATIONS)) == 0
    cos_t, sin_t = _rope_tables(s)
    u0 = 3 * QKV_WIDTH

    for layer in range(depth):
        w = w_in[layer]
        wqkv = jnp.stack([
            jnp.concatenate([w[:, part * QKV_WIDTH + gi * ATTN_WIDTH:
                               part * QKV_WIDTH + (gi + 1) * ATTN_WIDTH] for part in range(3)], axis=1)
            for gi in range(N_GROUPS)]).astype(BF16)
        wu = w[:, u0:u0 + SSM_WIDTH].astype(BF16)
        wgate = w[:, u0 + SSM_WIDTH:].astype(BF16)

        outs = _project(x, cos_t, sin_t, wqkv, wu)
        attn = _attention(outs[:N_GROUPS], bsz, s)

        u_tm = jnp.transpose(outs[N_GROUPS], (1, 0, 2)).reshape(s * bsz, SSM_WIDTH)
        bm, a_re, a_im, cm = _ssm_matrices(
            ssm_a_re[layer], ssm_a_im[layer], ssm_log_dt[layer], ssm_b_re[layer],
            ssm_b_im[layer], ssm_c_re[layer], ssm_c_im[layer], bsz)
        g_tm = _ssm(u_tm, bm, a_re, a_im, cm, ssm_d[layer].reshape(1, SSM_WIDTH).astype(F32), bsz)
        g = jnp.transpose(g_tm.reshape(s, bsz, SSM_WIDTH), (1, 0, 2))

        h = _mix(x.reshape(bsz * s, dm), attn.reshape(bsz * s, ATTN_WIDTH),
                 g.reshape(bsz * s, SSM_WIDTH), wgate,
                 b_gate[layer].reshape(1, 2 * dm).astype(F32),
                 w_attn_br[layer].astype(BF16), w_glu[layer].astype(BF16),
                 w_ssm_br[layer].astype(BF16), w_out[layer].astype(BF16),
                 ln1_g[layer].reshape(1, dm).astype(F32), ln1_b[layer].reshape(1, dm).astype(F32),
                 alpha)
        x2 = _ffn(h, w_ff_gate[layer].astype(BF16), w_ff_up[layer].astype(BF16),
                  w_ff_down[layer].astype(BF16),
                  ln2_g[layer].reshape(1, dm).astype(F32), ln2_b[layer].reshape(1, dm).astype(F32),
                  alpha)
        x = x2.reshape(bsz, s, dm)
    return x
```

```python
import functools
import math

import jax
import jax.numpy as jnp
from jax import lax
from jax.experimental import pallas as pl
from jax.experimental.pallas import tpu as pltpu

F32 = jnp.float32
BF16 = jnp.bfloat16

LANES = 128
HEAD_DIM = 64
ATTN_HEADS = 8
ATTN_WIDTH = ATTN_HEADS * HEAD_DIM
DILATIONS = (1, 4, 16)
WINDOWS = (128, 512, 2048)
N_GROUPS = len(DILATIONS)
QKV_WIDTH = N_GROUPS * ATTN_WIDTH
ATTN_BLOCK = 128
ROPE_THETA = 10000.0
NEG_INF = -1e30
SSM_GROUP = 16
SSM_GROUPS = 32
SSM_WIDTH = SSM_GROUP * SSM_GROUPS
SSM_STATE = 64
SSM_PAIRS = SSM_GROUPS // 2
LN_EPS = 1e-5

VMEM_LIMIT = 56 * 1024 * 1024


def _const_spec(shape):
    zeros = (0,) * len(shape)
    return pl.BlockSpec(shape, lambda *_: zeros, pipeline_mode=pl.Buffered(1))


def _phase_major(ref, tm, d):
    if d == 1:
        return ref[...]
    n = tm // d
    return jnp.concatenate([ref[pl.ds(r, n, stride=d), :] for r in range(d)], axis=0)


def _proj_kernel(*refs, tm, n_slabs):
    x_refs = refs[:n_slabs]
    cos_ref, sin_ref, wqkv_ref, wu_ref = refs[n_slabs:n_slabs + 4]
    o_refs = refs[n_slabs + 4:n_slabs + 4 + N_GROUPS]
    u_ref = refs[n_slabs + 4 + N_GROUPS]

    lane = lax.broadcasted_iota(jnp.int32, (1, LANES), 1)
    first_half = (lane % HEAD_DIM) < (HEAD_DIM // 2)

    for gi, d in enumerate(DILATIONS):
        n = tm // d
        xg = jnp.concatenate(
            [_phase_major(xr, tm, d).astype(BF16) for xr in x_refs], axis=1)
        cos = _phase_major(cos_ref, tm, d)
        sin = _phase_major(sin_ref, tm, d)
        if gi == 0:
            u_ref[...] = jnp.dot(xg, wu_ref[...], preferred_element_type=F32)
        for part in range(3):
            c0 = part * ATTN_WIDTH
            acc = jnp.dot(xg, wqkv_ref[gi, :, c0:c0 + ATTN_WIDTH],
                          preferred_element_type=F32)
            for c in range(ATTN_WIDTH // LANES):
                t = acc[:, c * LANES:(c + 1) * LANES]
                if part < 2:
                    rot = jnp.where(first_half,
                                    pltpu.roll(t, LANES - HEAD_DIM // 2, 1),
                                    pltpu.roll(t, HEAD_DIM // 2, 1))
                    t = t * cos + rot * sin
                if part == 0:
                    t = t * (1.0 / math.sqrt(HEAD_DIM))
                t = t.astype(BF16)
                col = c0 + c * LANES
                if d == 1:
                    o_refs[gi][:, col:col + LANES] = t
                else:
                    for r in range(d):
                        o_refs[gi][r, :, col:col + LANES] = t[r * n:(r + 1) * n]


def _project(x, cos_t, sin_t, wqkv, wu, tm=512):
    b, s, dm = x.shape
    n_slabs = dm // LANES
    grid = (b, s // tm)
    x_specs = [pl.BlockSpec((None, tm, LANES), functools.partial(lambda bi, i, c: (bi, i, c), c=c))
               for c in range(n_slabs)]
    tab_spec = pl.BlockSpec((tm, LANES), lambda bi, i: (i, 0))
    out_shapes = []
    out_specs = []
    for d in DILATIONS:
        if d == 1:
            out_shapes.append(jax.ShapeDtypeStruct((b, s, QKV_WIDTH), BF16))
            out_specs.append(pl.BlockSpec((None, tm, QKV_WIDTH), lambda bi, i: (bi, i, 0)))
        else:
            out_shapes.append(jax.ShapeDtypeStruct((b, d, s // d, QKV_WIDTH), BF16))
            out_specs.append(pl.BlockSpec((None, d, tm // d, QKV_WIDTH),
                                          lambda bi, i: (bi, 0, i, 0)))
    out_shapes.append(jax.ShapeDtypeStruct((b, s, SSM_WIDTH), F32))
    out_specs.append(pl.BlockSpec((None, tm, SSM_WIDTH), lambda bi, i: (bi, i, 0)))
    return pl.pallas_call(
        functools.partial(_proj_kernel, tm=tm, n_slabs=n_slabs),
        grid=grid,
        in_specs=x_specs + [tab_spec, tab_spec, _const_spec(wqkv.shape), _const_spec(wu.shape)],
        out_specs=out_specs,
        out_shape=out_shapes,
        compiler_params=pltpu.CompilerParams(
            dimension_semantics=("parallel", "parallel"), vmem_limit_bytes=VMEM_LIMIT),
        name="in_proj",
    )(*([x] * n_slabs), cos_t, sin_t, wqkv, wu)


def _attn_bias():
    blk = ATTN_BLOCK
    a = (jnp.arange(2 * blk) % blk)[:, None]
    c = jnp.arange(2 * blk)[None, :]
    in_window = jnp.where(c < blk, c >= a, c - blk <= a)
    windowed = jnp.where(in_window, 0.0, NEG_INF)
    causal = jnp.where(c[:, :blk] <= a, 0.0, NEG_INF)
    return windowed.astype(F32), causal.astype(F32)


def _attn_kernel(*refs, s, lookahead):
    qkv = refs[:3 * N_GROUPS]
    bias_w_ref, bias_c_ref = refs[3 * N_GROUPS:3 * N_GROUPS + 2]
    o_ref = refs[3 * N_GROUPS + 2]
    acc_s, m_s, l_s = refs[3 * N_GROUPS + 3:]
    blk = ATTN_BLOCK
    n_blocks = s // blk

    lane = lax.broadcasted_iota(jnp.int32, (1, LANES), 1)
    head_a = lane < HEAD_DIM

    def window(ref, i, nblk):
        start = i if i % nblk == 0 else i - 1
        return ref[start * blk:(i + 1) * blk, :]

    def scores(gi, i, nblk):
        q_ref, k_ref, _ = qkv[3 * gi:3 * gi + 3]
        q = q_ref[i * blk:(i + 1) * blk, :]
        zero = jnp.zeros_like(q)
        q2 = jnp.concatenate([jnp.where(head_a, q, zero), jnp.where(head_a, zero, q)], axis=0)
        kw = window(k_ref, i, nblk)
        bias = bias_c_ref[...] if i % nblk == 0 else bias_w_ref[...]
        return lax.dot_general(q2, kw, (((1,), (1,)), ((), ())),
                               preferred_element_type=F32) + bias

    def finish(gi, d, i, nblk, sc):
        vw = window(qkv[3 * gi + 2], i, nblk)
        m = jnp.max(sc, axis=1, keepdims=True)
        p = jnp.exp(sc - m).astype(BF16)
        v_ext = jnp.concatenate([vw, jnp.ones_like(vw)], axis=1)
        res = jnp.dot(p, v_ext, preferred_element_type=F32)
        if d == 1:
            rows = pl.ds(i * blk, blk)
        else:
            r, m0 = divmod(i * blk, s // d)
            rows = pl.ds(m0 * d + r, blk, stride=d)
        acc_s[gi, rows, :] = jnp.where(head_a, res[:blk, :LANES], res[blk:, :LANES])
        l_s[gi, rows, :] = jnp.where(head_a, res[:blk, LANES:], res[blk:, LANES:])
        m_s[gi, rows, :] = jnp.where(head_a, m[:blk], m[blk:])

    work = [(gi, d, i, (s // d) // blk) for gi, d in enumerate(DILATIONS) for i in range(n_blocks)]
    pending = {}
    for step in range(len(work) + lookahead):
        if step < len(work):
            gi, d, i, nblk = work[step]
            pending[step] = scores(gi, i, nblk)
        if step >= lookahead:
            gi, d, i, nblk = work[step - lookahead]
            finish(gi, d, i, nblk, pending.pop(step - lookahead))

    chunk = 256

    def combine(i, carry):
        rows = pl.ds(pl.multiple_of(i * chunk, chunk), chunk)
        ms = [m_s[gi, rows, :] for gi in range(N_GROUPS)]
        m_all = jnp.maximum(jnp.maximum(ms[0], ms[1]), ms[2])
        num = jnp.zeros((chunk, LANES), F32)
        den = jnp.zeros((chunk, LANES), F32)
        for gi in range(N_GROUPS):
            wgt = jnp.exp(ms[gi] - m_all)
            num = num + wgt * acc_s[gi, rows, :]
            den = den + wgt * l_s[gi, rows, :]
        o_ref[rows, :] = (num / den).astype(o_ref.dtype)
        return carry
    lax.fori_loop(0, s // chunk, combine, 0)


def _attention(qkv_groups, b, s, lookahead=5):
    n_hp = ATTN_WIDTH // LANES
    in_specs, args = [], []
    for gi, d in enumerate(DILATIONS):
        arr = qkv_groups[gi].reshape(b, s, QKV_WIDTH)
        for part in range(3):
            in_specs.append(pl.BlockSpec(
                (None, s, LANES),
                functools.partial(lambda bi, hp, part: (bi, 0, part * n_hp + hp), part=part)))
            args.append(arr)
    bias_w, bias_c = _attn_bias()
    return pl.pallas_call(
        functools.partial(_attn_kernel, s=s, lookahead=lookahead),
        grid=(b, n_hp),
        in_specs=in_specs + [_const_spec(bias_w.shape), _const_spec(bias_c.shape)],
        out_specs=pl.BlockSpec((None, s, LANES), lambda bi, hp: (bi, 0, hp)),
        out_shape=jax.ShapeDtypeStruct((b, s, ATTN_WIDTH), BF16),
        scratch_shapes=[pltpu.VMEM((N_GROUPS, s, LANES), F32)] * 3,
        compiler_params=pltpu.CompilerParams(
            dimension_semantics=("parallel", "parallel"), vmem_limit_bytes=VMEM_LIMIT),
        name="attention",
    )(*args, bias_w, bias_c)


def _gelu_tanh(x):
    return 0.5 * x * (1.0 + jnp.tanh(math.sqrt(2.0 / math.pi) * (x + 0.044715 * (x * x * x))))


def _ssm_kernel(u_ref, bm_ref, are_ref, aim_ref, cm_ref, dskip_ref, o_ref,
                sre, sim, h_ref, *, tc, bsz):
    rows = tc * bsz
    pairs_per_slab = LANES // (2 * SSM_GROUP)

    @pl.when(pl.program_id(0) == 0)
    def _():
        h_ref[...] = jnp.zeros_like(h_ref)

    u = u_ref[...]
    ub = u.astype(BF16)
    for j in range(SSM_PAIRS):
        k0 = (j // pairs_per_slab) * LANES
        bu = jnp.dot(ub[:, k0:k0 + LANES], bm_ref[j], preferred_element_type=F32)
        sre[j] = bu[:, :LANES]
        sim[j] = bu[:, LANES:]

    half = SSM_PAIRS // 2
    for jb in range(2):
        js = list(range(jb * half, (jb + 1) * half))
        a_re = [are_ref[j] for j in js]
        a_im = [aim_ref[j] for j in js]

        def step(t, carry, js=js, a_re=a_re, a_im=a_im):
            h_re, h_im = carry
            r0 = pl.multiple_of(t * bsz, bsz)
            new_re, new_im = [], []
            for i, j in enumerate(js):
                n_re = a_re[i] * h_re[i] - a_im[i] * h_im[i] + sre[j, pl.ds(r0, bsz), :]
                n_im = a_re[i] * h_im[i] + a_im[i] * h_re[i] + sim[j, pl.ds(r0, bsz), :]
                sre[j, pl.ds(r0, bsz), :] = n_re
                sim[j, pl.ds(r0, bsz), :] = n_im
                new_re.append(n_re)
                new_im.append(n_im)
            return tuple(new_re), tuple(new_im)

        init = (tuple(h_ref[0, j] for j in js), tuple(h_ref[1, j] for j in js))
        h_re, h_im = lax.fori_loop(0, tc, step, init)
        for i, j in enumerate(js):
            h_ref[0, j] = h_re[i]
            h_ref[1, j] = h_im[i]

    for qb in range(SSM_WIDTH // LANES):
        y = jnp.zeros((rows, LANES), F32)
        for j in range(qb * pairs_per_slab, (qb + 1) * pairs_per_slab):
            st = jnp.concatenate([sre[j].astype(BF16), sim[j].astype(BF16)], axis=1)
            y = y + jnp.dot(st, cm_ref[j], preferred_element_type=F32)
        cols = slice(qb * LANES, (qb + 1) * LANES)
        y = y + dskip_ref[:, cols] * u[:, cols]
        o_ref[:, cols] = _gelu_tanh(y).astype(o_ref.dtype)


def _ssm(u_tm, bm, a_re, a_im, cm, d_skip, bsz, tc=128):
    rows_total = u_tm.shape[0]
    s = rows_total // bsz
    rows = tc * bsz
    return pl.pallas_call(
        functools.partial(_ssm_kernel, tc=tc, bsz=bsz),
        grid=(s // tc,),
        in_specs=[pl.BlockSpec((rows, SSM_WIDTH), lambda i: (i, 0)),
                  _const_spec(bm.shape), _const_spec(a_re.shape), _const_spec(a_im.shape),
                  _const_spec(cm.shape), _const_spec(d_skip.shape)],
        out_specs=pl.BlockSpec((rows, SSM_WIDTH), lambda i: (i, 0)),
        out_shape=jax.ShapeDtypeStruct((rows_total, SSM_WIDTH), BF16),
        scratch_shapes=[pltpu.VMEM((SSM_PAIRS, rows, LANES), F32),
                        pltpu.VMEM((SSM_PAIRS, rows, LANES), F32),
                        pltpu.VMEM((2, SSM_PAIRS, bsz, LANES), F32)],
        compiler_params=pltpu.CompilerParams(
            dimension_semantics=("arbitrary",), vmem_limit_bytes=VMEM_LIMIT),
        name="ssm",
    )(u_tm, bm, a_re, a_im, cm, d_skip)


def _ssm_matrices(a_re, a_im, log_dt, b_re, b_im, c_re, c_im, bsz):
    g, p, h = SSM_GROUPS, SSM_STATE, SSM_GROUP
    lam = lax.complex(a_re.astype(F32), a_im.astype(F32))
    dt = jnp.exp(log_dt.astype(F32))[:, None]
    a_bar = jnp.exp(lam * dt)
    b_bar = ((a_bar - 1.0) / lam)[..., None] * lax.complex(b_re.astype(F32), b_im.astype(F32))
    groups_per_slab = LANES // h
    pairs_per_slab = groups_per_slab // 2
    eye = jnp.eye(groups_per_slab, dtype=F32)

    def in_matrix(bpart):
        bt = jnp.transpose(bpart, (0, 2, 1)).reshape(g // groups_per_slab, groups_per_slab, h, p)
        dense = bt[:, :, :, None, :] * eye[None, :, None, :, None]
        dense = dense.reshape(g // groups_per_slab, LANES, pairs_per_slab, 2 * p)
        return jnp.transpose(dense, (0, 2, 1, 3)).reshape(SSM_PAIRS, LANES, 2 * p)

    bm = jnp.concatenate([in_matrix(b_bar.real), in_matrix(b_bar.imag)], axis=2)

    def out_matrix(cpart):
        ct = jnp.transpose(cpart, (0, 2, 1)).reshape(g // groups_per_slab, groups_per_slab, p, h)
        dense = ct[:, :, :, None, :] * eye[None, :, None, :, None]
        dense = dense.reshape(g // groups_per_slab, pairs_per_slab, 2 * p, LANES)
        return dense.reshape(SSM_PAIRS, 2 * p, LANES)

    cm = jnp.concatenate([out_matrix(c_re.astype(F32)), out_matrix(-c_im.astype(F32))], axis=1)

    def bcast(apart):
        return jnp.broadcast_to(apart.reshape(SSM_PAIRS, 1, 2 * p), (SSM_PAIRS, bsz, 2 * p))

    return bm.astype(BF16), bcast(a_bar.real), bcast(a_bar.imag), cm.astype(BF16)


def _layer_norm(z, g, b):
    mu = jnp.mean(z, axis=-1, keepdims=True)
    zc = z - mu
    var = jnp.mean(zc * zc, axis=-1, keepdims=True)
    return zc * lax.rsqrt(var + LN_EPS) * g + b


def _mix_kernel(x_ref, attn_ref, g_ref, wgate_ref, bgate_ref, wattn_ref, wglu_ref,
                wssm_ref, wout_ref, lng_ref, lnb_ref, o_ref, *, alpha):
    dm = x_ref.shape[1]
    x = x_ref[...]
    logits = jnp.dot(x.astype(BF16), wgate_ref[...], preferred_element_type=F32) + bgate_ref[...]
    gates = jax.nn.sigmoid(logits)
    y_attn = jnp.dot(attn_ref[...], wattn_ref[...], preferred_element_type=F32)
    glu = jnp.dot(g_ref[...], wglu_ref[...], preferred_element_type=F32)
    y_s = glu[:, :SSM_WIDTH] * jax.nn.sigmoid(glu[:, SSM_WIDTH:])
    y_ssm = jnp.dot(y_s.astype(BF16), wssm_ref[...], preferred_element_type=F32)
    mixed = gates[:, :dm] * y_attn + gates[:, dm:] * y_ssm
    mix_out = jnp.dot(mixed.astype(BF16), wout_ref[...], preferred_element_type=F32)
    o_ref[...] = _layer_norm(alpha * x + mix_out, lng_ref[...], lnb_ref[...])


def _mix(x2, attn2, g2, wgate, bgate, wattn, wglu, wssm, wout, lng, lnb, alpha, tm=512):
    rows, dm = x2.shape
    row_spec = lambda w: pl.BlockSpec((tm, w), lambda i: (i, 0))
    consts = [wgate, bgate, wattn, wglu, wssm, wout, lng, lnb]
    return pl.pallas_call(
        functools.partial(_mix_kernel, alpha=alpha),
        grid=(rows // tm,),
        in_specs=[row_spec(dm), row_spec(ATTN_WIDTH), row_spec(SSM_WIDTH)]
                 + [_const_spec(c.shape) for c in consts],
        out_specs=row_spec(dm),
        out_shape=jax.ShapeDtypeStruct((rows, dm), F32),
        compiler_params=pltpu.CompilerParams(
            dimension_semantics=("parallel",), vmem_limit_bytes=VMEM_LIMIT),
        name="mix",
    )(x2, attn2, g2, *consts)


def _ffn_kernel(h_ref, wg_ref, wu_ref, wd_ref, lng_ref, lnb_ref, o_ref, *, alpha):
    h = h_ref[...]
    hb = h.astype(BF16)
    gate = jnp.dot(hb, wg_ref[...], preferred_element_type=F32)
    up = jnp.dot(hb, wu_ref[...], preferred_element_type=F32)
    act = (gate * jax.nn.sigmoid(gate)) * up
    ff = jnp.dot(act.astype(BF16), wd_ref[...], preferred_element_type=F32)
    o_ref[...] = _layer_norm(alpha * h + ff, lng_ref[...], lnb_ref[...])


def _ffn(h2, wg, wu, wd, lng, lnb, alpha, tm=512):
    rows, dm = h2.shape
    row_spec = pl.BlockSpec((tm, dm), lambda i: (i, 0))
    consts = [wg, wu, wd, lng, lnb]
    return pl.pallas_call(
        functools.partial(_ffn_kernel, alpha=alpha),
        grid=(rows // tm,),
        in_specs=[row_spec] + [_const_spec(c.shape) for c in consts],
        out_specs=row_spec,
        out_shape=jax.ShapeDtypeStruct((rows, dm), F32),
        compiler_params=pltpu.CompilerParams(
            dimension_semantics=("parallel",), vmem_limit_bytes=VMEM_LIMIT),
        name="ffn",
    )(h2, *consts)


def _rope_tables(s):
    half = HEAD_DIM // 2
    pos = jnp.arange(s, dtype=F32)
    inv_freq = ROPE_THETA ** (-jnp.arange(half, dtype=F32) / half)
    ang = pos[:, None] * inv_freq[None, :]
    cos, sin = jnp.cos(ang), jnp.sin(ang)
    reps = LANES // HEAD_DIM
    cos_t = jnp.concatenate([cos, cos] * reps, axis=1)
    sin_t = jnp.concatenate([-sin, sin] * reps, axis=1)
    return cos_t, sin_t


def kernel(x, w_in, b_gate, w_attn_br, w_ssm_br, w_out, ssm_a_re, ssm_a_im, ssm_log_dt, ssm_b_re, ssm_b_im, ssm_c_re, ssm_c_im, ssm_d, w_glu, ln1_g, ln1_b, w_ff_gate, w_ff_up, w_ff_down, ln2_g, ln2_b):
    bsz, s, dm = x.shape
    depth = w_in.shape[0]
    alpha = (2.0 * depth) ** 0.25
    assert all(w // d == ATTN_BLOCK for w, d in zip(WINDOWS, DILATIONS))
    assert s % (ATTN_BLOCK * max(DILATIONS)) == 0
    cos_t, sin_t = _rope_tables(s)
    u0 = 3 * QKV_WIDTH

    for layer in range(depth):
        w = w_in[layer]
        wqkv = jnp.stack([
            jnp.concatenate([w[:, part * QKV_WIDTH + gi * ATTN_WIDTH:
                               part * QKV_WIDTH + (gi + 1) * ATTN_WIDTH] for part in range(3)], axis=1)
            for gi in range(N_GROUPS)]).astype(BF16)
        wu = w[:, u0:u0 + SSM_WIDTH].astype(BF16)
        wgate = w[:, u0 + SSM_WIDTH:].astype(BF16)

        outs = _project(x, cos_t, sin_t, wqkv, wu)
        attn = _attention(outs[:N_GROUPS], bsz, s)

        u_tm = jnp.transpose(outs[N_GROUPS], (1, 0, 2)).reshape(s * bsz, SSM_WIDTH)
        bm, a_re, a_im, cm = _ssm_matrices(
            ssm_a_re[layer], ssm_a_im[layer], ssm_log_dt[layer], ssm_b_re[layer],
            ssm_b_im[layer], ssm_c_re[layer], ssm_c_im[layer], bsz)
        g_tm = _ssm(u_tm, bm, a_re, a_im, cm, ssm_d[layer].reshape(1, SSM_WIDTH).astype(F32), bsz)
        g = jnp.transpose(g_tm.reshape(s, bsz, SSM_WIDTH), (1, 0, 2))

        h = _mix(x.reshape(bsz * s, dm), attn.reshape(bsz * s, ATTN_WIDTH),
                 g.reshape(bsz * s, SSM_WIDTH), wgate,
                 b_gate[layer].reshape(1, 2 * dm).astype(F32),
                 w_attn_br[layer].astype(BF16), w_glu[layer].astype(BF16),
                 w_ssm_br[layer].astype(BF16), w_out[layer].astype(BF16),
                 ln1_g[layer].reshape(1, dm).astype(F32), ln1_b[layer].reshape(1, dm).astype(F32),
                 alpha)
        x2 = _ffn(h, w_ff_gate[layer].astype(BF16), w_ff_up[layer].astype(BF16),
                  w_ff_down[layer].astype(BF16),
                  ln2_g[layer].reshape(1, dm).astype(F32), ln2_b[layer].reshape(1, dm).astype(F32),
                  alpha)
        x = x2.reshape(bsz, s, dm)
    return x
```

```python
import functools
import math

import jax
import jax.numpy as jnp
import numpy as np
from jax import lax
from jax.experimental import pallas as pl
from jax.experimental.pallas import tpu as pltpu

F32 = jnp.float32
BF16 = jnp.bfloat16

LANES = 128
HEAD_DIM = 64
ATTN_HEADS = 8
ATTN_WIDTH = ATTN_HEADS * HEAD_DIM
DILATIONS = (1, 4, 16)
WINDOWS = (128, 512, 2048)
N_GROUPS = len(DILATIONS)
QKV_WIDTH = N_GROUPS * ATTN_WIDTH
ATTN_BLOCK = 128
ROPE_THETA = 10000.0
NEG_INF = -1e30
SSM_GROUP = 16
SSM_GROUPS = 32
SSM_WIDTH = SSM_GROUP * SSM_GROUPS
SSM_STATE = 64
SSM_PAIRS = SSM_GROUPS // 2
LN_EPS = 1e-5

VMEM_LIMIT = 56 * 1024 * 1024


def _const_spec(shape):
    zeros = (0,) * len(shape)
    return pl.BlockSpec(shape, lambda *_: zeros, pipeline_mode=pl.Buffered(1))


def _phase_major(ref, tm, d):
    if d == 1:
        return ref[...]
    n = tm // d
    return jnp.concatenate([ref[pl.ds(r, n, stride=d), :] for r in range(d)], axis=0)


def _proj_kernel(*refs, tm, n_slabs):
    x_refs = refs[:n_slabs]
    cos_ref, sin_ref, w_ref = refs[n_slabs:n_slabs + 3]
    o_refs = refs[n_slabs + 3:n_slabs + 3 + N_GROUPS]
    u_ref = refs[n_slabs + 3 + N_GROUPS]

    lane = lax.broadcasted_iota(jnp.int32, (1, LANES), 1)
    first_half = (lane % HEAD_DIM) < (HEAD_DIM // 2)

    for gi, d in enumerate(DILATIONS):
        n = tm // d
        xg = jnp.concatenate(
            [_phase_major(xr, tm, d).astype(BF16) for xr in x_refs], axis=1)
        cos = _phase_major(cos_ref, tm, d)
        sin = _phase_major(sin_ref, tm, d)
        if gi == 0:
            u_ref[...] = jnp.dot(xg, w_ref[:, 3 * QKV_WIDTH:], preferred_element_type=F32)
        for part in range(3):
            c0 = part * ATTN_WIDTH
            w0 = part * QKV_WIDTH + gi * ATTN_WIDTH
            acc = jnp.dot(xg, w_ref[:, w0:w0 + ATTN_WIDTH], preferred_element_type=F32)
            for c in range(ATTN_WIDTH // LANES):
                t = acc[:, c * LANES:(c + 1) * LANES]
                if part < 2:
                    rot = jnp.where(first_half,
                                    pltpu.roll(t, LANES - HEAD_DIM // 2, 1),
                                    pltpu.roll(t, HEAD_DIM // 2, 1))
                    t = t * cos + rot * sin
                if part == 0:
                    t = t * (1.0 / math.sqrt(HEAD_DIM))
                t = t.astype(BF16)
                col = c0 + c * LANES
                if d == 1:
                    o_refs[gi][:, col:col + LANES] = t
                else:
                    for r in range(d):
                        o_refs[gi][r, :, col:col + LANES] = t[r * n:(r + 1) * n]


def _project(x, cos_t, sin_t, w, tm=512):
    b, s, dm = x.shape
    n_slabs = dm // LANES
    grid = (b, s // tm)
    x_specs = [pl.BlockSpec((None, tm, LANES), functools.partial(lambda bi, i, c: (bi, i, c), c=c))
               for c in range(n_slabs)]
    tab_spec = pl.BlockSpec((tm, LANES), lambda bi, i: (i, 0))
    out_shapes = []
    out_specs = []
    for d in DILATIONS:
        if d == 1:
            out_shapes.append(jax.ShapeDtypeStruct((b, s, QKV_WIDTH), BF16))
            out_specs.append(pl.BlockSpec((None, tm, QKV_WIDTH), lambda bi, i: (bi, i, 0)))
        else:
            out_shapes.append(jax.ShapeDtypeStruct((b, d, s // d, QKV_WIDTH), BF16))
            out_specs.append(pl.BlockSpec((None, d, tm // d, QKV_WIDTH),
                                          lambda bi, i: (bi, 0, i, 0)))
    out_shapes.append(jax.ShapeDtypeStruct((b, s, SSM_WIDTH), F32))
    out_specs.append(pl.BlockSpec((None, tm, SSM_WIDTH), lambda bi, i: (bi, i, 0)))
    return pl.pallas_call(
        functools.partial(_proj_kernel, tm=tm, n_slabs=n_slabs),
        grid=grid,
        in_specs=x_specs + [tab_spec, tab_spec, _const_spec(w.shape)],
        out_specs=out_specs,
        out_shape=out_shapes,
        compiler_params=pltpu.CompilerParams(
            dimension_semantics=("parallel", "parallel"), vmem_limit_bytes=VMEM_LIMIT),
        name="in_proj",
    )(*([x] * n_slabs), cos_t, sin_t, w)


def _attn_bias():
    blk = ATTN_BLOCK
    a = (np.arange(2 * blk) % blk)[:, None]
    c = np.arange(2 * blk)[None, :]
    in_window = np.where(c < blk, c >= a, c - blk <= a)
    windowed = np.where(in_window, 0.0, NEG_INF)
    causal = np.where(c[:, :blk] <= a, 0.0, NEG_INF)
    return windowed.astype(np.float32), causal.astype(np.float32)


def _attn_kernel(*refs, s, lookahead):
    qkv = refs[:3 * N_GROUPS]
    bias_w_ref, bias_c_ref = refs[3 * N_GROUPS:3 * N_GROUPS + 2]
    o_ref = refs[3 * N_GROUPS + 2]
    acc_s, m_s, l_s = refs[3 * N_GROUPS + 3:]
    blk = ATTN_BLOCK
    n_blocks = s // blk

    lane = lax.broadcasted_iota(jnp.int32, (1, LANES), 1)
    head_a = lane < HEAD_DIM

    def window(ref, i, nblk):
        start = i if i % nblk == 0 else i - 1
        return ref[start * blk:(i + 1) * blk, :]

    def scores(gi, i, nblk):
        q_ref, k_ref, _ = qkv[3 * gi:3 * gi + 3]
        q = q_ref[i * blk:(i + 1) * blk, :]
        zero = jnp.zeros_like(q)
        q2 = jnp.concatenate([jnp.where(head_a, q, zero), jnp.where(head_a, zero, q)], axis=0)
        kw = window(k_ref, i, nblk)
        bias = bias_c_ref[...] if i % nblk == 0 else bias_w_ref[...]
        return lax.dot_general(q2, kw, (((1,), (1,)), ((), ())),
                               preferred_element_type=F32) + bias

    def finish(gi, d, i, nblk, sc):
        vw = window(qkv[3 * gi + 2], i, nblk)
        m = jnp.max(sc, axis=1, keepdims=True)
        p = jnp.exp(sc - m).astype(BF16)
        v_ext = jnp.concatenate([vw, jnp.ones_like(vw)], axis=1)
        res = jnp.dot(p, v_ext, preferred_element_type=F32)
        if d == 1:
            rows = pl.ds(i * blk, blk)
        else:
            r, m0 = divmod(i * blk, s // d)
            rows = pl.ds(m0 * d + r, blk, stride=d)
        acc_s[gi, rows, :] = jnp.where(head_a, res[:blk, :LANES], res[blk:, :LANES])
        l_s[gi, rows, :] = jnp.where(head_a, res[:blk, LANES:], res[blk:, LANES:])
        m_s[gi, rows, :] = jnp.where(head_a, m[:blk], m[blk:])

    work = [(gi, d, i, (s // d) // blk) for gi, d in enumerate(DILATIONS) for i in range(n_blocks)]
    pending = {}
    for step in range(len(work) + lookahead):
        if step < len(work):
            gi, d, i, nblk = work[step]
            pending[step] = scores(gi, i, nblk)
        if step >= lookahead:
            gi, d, i, nblk = work[step - lookahead]
            finish(gi, d, i, nblk, pending.pop(step - lookahead))

    chunk = 256

    def combine(i, carry):
        rows = pl.ds(pl.multiple_of(i * chunk, chunk), chunk)
        ms = [m_s[gi, rows, :] for gi in range(N_GROUPS)]
        m_all = jnp.maximum(jnp.maximum(ms[0], ms[1]), ms[2])
        num = jnp.zeros((chunk, LANES), F32)
        den = jnp.zeros((chunk, LANES), F32)
        for gi in range(N_GROUPS):
            wgt = jnp.exp(ms[gi] - m_all)
            num = num + wgt * acc_s[gi, rows, :]
            den = den + wgt * l_s[gi, rows, :]
        o_ref[rows, :] = (num / den).astype(o_ref.dtype)
        return carry
    lax.fori_loop(0, s // chunk, combine, 0)


def _attention(qkv_groups, b, s, lookahead=5):
    n_hp = ATTN_WIDTH // LANES
    in_specs, args = [], []
    for gi, d in enumerate(DILATIONS):
        arr = qkv_groups[gi].reshape(b, s, QKV_WIDTH)
        for part in range(3):
            in_specs.append(pl.BlockSpec(
                (None, s, LANES),
                functools.partial(lambda bi, hp, part: (bi, 0, part * n_hp + hp), part=part)))
            args.append(arr)
    bias_w, bias_c = _attn_bias()
    return pl.pallas_call(
        functools.partial(_attn_kernel, s=s, lookahead=lookahead),
        grid=(b, n_hp),
        in_specs=in_specs + [_const_spec(bias_w.shape), _const_spec(bias_c.shape)],
        out_specs=pl.BlockSpec((None, s, LANES), lambda bi, hp: (bi, 0, hp)),
        out_shape=jax.ShapeDtypeStruct((b, s, ATTN_WIDTH), BF16),
        scratch_shapes=[pltpu.VMEM((N_GROUPS, s, LANES), F32)] * 3,
        compiler_params=pltpu.CompilerParams(
            dimension_semantics=("parallel", "parallel"), vmem_limit_bytes=VMEM_LIMIT),
        name="attention",
    )(*args, bias_w, bias_c)


def _gelu_tanh(x):
    return 0.5 * x * (1.0 + jnp.tanh(math.sqrt(2.0 / math.pi) * (x + 0.044715 * (x * x * x))))


def _ssm_kernel(u_ref, bm_ref, are_ref, aim_ref, cm_ref, dskip_ref, o_ref,
                st_a, st_b, ub_s, y_s, h_ref, *, tc, bsz):
    rows = tc * bsz
    half = SSM_PAIRS // 2
    pairs_per_slab = LANES // (2 * SSM_GROUP)
    n_slabs = SSM_WIDTH // LANES
    splits = 2
    n_slices = half * splits
    steps = tc // n_slices
    srows = rows // splits

    @pl.when(pl.program_id(0) == 0)
    def _():
        h_ref[...] = jnp.zeros_like(h_ref)

    for c in range(n_slabs):
        cols = slice(c * LANES, (c + 1) * LANES)
        for b in range(bsz):
            y_s[c, pl.ds(b, tc, stride=bsz), :] = u_ref[b, :, cols]
        u = y_s[c]
        ub_s[c] = u.astype(BF16)
        y_s[c] = dskip_ref[:, cols] * u

    def project_in(buf, j, jl, rs):
        bu = jnp.dot(ub_s[j // pairs_per_slab, rs, :], bm_ref[j], preferred_element_type=F32)
        buf[0, jl, rs, :] = bu[:, :LANES]
        buf[1, jl, rs, :] = bu[:, LANES:]

    def project_out(buf, j, jl, rs):
        st = jnp.concatenate([buf[0, jl, rs, :].astype(BF16), buf[1, jl, rs, :].astype(BF16)],
                             axis=1)
        y_s[j // pairs_per_slab, rs, :] += jnp.dot(st, cm_ref[j], preferred_element_type=F32)

    def scan(buf, j0, other):
        a_re = [are_ref[j0 + i] for i in range(half)]
        a_im = [aim_ref[j0 + i] for i in range(half)]

        def body(it, carry):
            h_re, h_im = carry
            for k in range(steps):
                r0 = pl.multiple_of((it * steps + k) * bsz, bsz)
                new_re, new_im = [], []
                for i in range(half):
                    n_re = a_re[i] * h_re[i] - a_im[i] * h_im[i] + buf[0, i, pl.ds(r0, bsz), :]
                    n_im = a_re[i] * h_im[i] + a_im[i] * h_re[i] + buf[1, i, pl.ds(r0, bsz), :]
                    buf[0, i, pl.ds(r0, bsz), :] = n_re
                    buf[1, i, pl.ds(r0, bsz), :] = n_im
                    new_re.append(n_re)
                    new_im.append(n_im)
                h_re, h_im = tuple(new_re), tuple(new_im)
            other(it // splits, pl.ds((it % splits) * srows, srows))
            return h_re, h_im

        carry = (tuple(h_ref[0, j0 + i] for i in range(half)),
                 tuple(h_ref[1, j0 + i] for i in range(half)))
        for it in range(n_slices):
            carry = body(it, carry)
        h_re, h_im = carry
        for i in range(half):
            h_ref[0, j0 + i] = h_re[i]
            h_ref[1, j0 + i] = h_im[i]

    for jl in range(half):
        project_in(st_a, jl, jl, slice(None))
    scan(st_a, 0, lambda jl, rs: project_in(st_b, half + jl, jl, rs))
    scan(st_b, half, lambda jl, rs: project_out(st_a, jl, jl, rs))
    for jl in range(half):
        project_out(st_b, half + jl, jl, slice(None))

    for c in range(n_slabs):
        for b in range(bsz):
            y = y_s[c, pl.ds(b, tc, stride=bsz), :]
            o_ref[b, :, c * LANES:(c + 1) * LANES] = _gelu_tanh(y).astype(o_ref.dtype)


def _ssm(u, bm, a_re, a_im, cm, d_skip, tc=128):
    bsz, s, _ = u.shape
    rows = tc * bsz
    half = SSM_PAIRS // 2
    chunk_spec = pl.BlockSpec((bsz, tc, SSM_WIDTH), lambda i: (0, i, 0))
    return pl.pallas_call(
        functools.partial(_ssm_kernel, tc=tc, bsz=bsz),
        grid=(s // tc,),
        in_specs=[chunk_spec,
                  _const_spec(bm.shape), _const_spec(a_re.shape), _const_spec(a_im.shape),
                  _const_spec(cm.shape), _const_spec(d_skip.shape)],
        out_specs=chunk_spec,
        out_shape=jax.ShapeDtypeStruct((bsz, s, SSM_WIDTH), BF16),
        scratch_shapes=[pltpu.VMEM((2, half, rows, LANES), F32),
                        pltpu.VMEM((2, half, rows, LANES), F32),
                        pltpu.VMEM((SSM_WIDTH // LANES, rows, LANES), BF16),
                        pltpu.VMEM((SSM_WIDTH // LANES, rows, LANES), F32),
                        pltpu.VMEM((2, SSM_PAIRS, bsz, LANES), F32)],
        compiler_params=pltpu.CompilerParams(
            dimension_semantics=("arbitrary",), vmem_limit_bytes=VMEM_LIMIT),
        name="ssm",
    )(u, bm, a_re, a_im, cm, d_skip)


def _ssm_matrices(a_re, a_im, log_dt, b_re, b_im, c_re, c_im, bsz):
    g, p, h = SSM_GROUPS, SSM_STATE, SSM_GROUP
    lam = lax.complex(a_re.astype(F32), a_im.astype(F32))
    dt = jnp.exp(log_dt.astype(F32))[:, None]
    a_bar = jnp.exp(lam * dt)
    b_bar = ((a_bar - 1.0) / lam)[..., None] * lax.complex(b_re.astype(F32), b_im.astype(F32))
    groups_per_slab = LANES // h
    pairs_per_slab = groups_per_slab // 2
    eye = jnp.eye(groups_per_slab, dtype=F32)

    def in_matrix(bpart):
        bt = jnp.transpose(bpart, (0, 2, 1)).reshape(g // groups_per_slab, groups_per_slab, h, p)
        dense = bt[:, :, :, None, :] * eye[None, :, None, :, None]
        dense = dense.reshape(g // groups_per_slab, LANES, pairs_per_slab, 2 * p)
        return jnp.transpose(dense, (0, 2, 1, 3)).reshape(SSM_PAIRS, LANES, 2 * p)

    bm = jnp.concatenate([in_matrix(b_bar.real), in_matrix(b_bar.imag)], axis=2)

    def out_matrix(cpart):
        ct = jnp.transpose(cpart, (0, 2, 1)).reshape(g // groups_per_slab, groups_per_slab, p, h)
        dense = ct[:, :, :, None, :] * eye[None, :, None, :, None]
        dense = dense.reshape(g // groups_per_slab, pairs_per_slab, 2 * p, LANES)
        return dense.reshape(SSM_PAIRS, 2 * p, LANES)

    cm = jnp.concatenate([out_matrix(c_re.astype(F32)), out_matrix(-c_im.astype(F32))], axis=1)

    def bcast(apart):
        return jnp.broadcast_to(apart.reshape(SSM_PAIRS, 1, 2 * p), (SSM_PAIRS, bsz, 2 * p))

    return bm.astype(BF16), bcast(a_bar.real), bcast(a_bar.imag), cm.astype(BF16)


def _layer_norm(z, g, b):
    mu = jnp.mean(z, axis=-1, keepdims=True)
    zc = z - mu
    var = jnp.mean(zc * zc, axis=-1, keepdims=True)
    return zc * lax.rsqrt(var + LN_EPS) * g + b


def _mix_kernel(x_ref, attn_ref, g_ref, wgate_ref, bgate_ref, wattn_ref, wglu_ref,
                wssm_ref, wout_ref, lng_ref, lnb_ref, o_ref, *, alpha):
    dm = x_ref.shape[1]
    x = x_ref[...]
    logits = jnp.dot(x.astype(BF16), wgate_ref[...], preferred_element_type=F32) + bgate_ref[...]
    gates = jax.nn.sigmoid(logits)
    y_attn = jnp.dot(attn_ref[...], wattn_ref[...], preferred_element_type=F32)
    glu = jnp.dot(g_ref[...], wglu_ref[...], preferred_element_type=F32)
    y_s = glu[:, :SSM_WIDTH] * jax.nn.sigmoid(glu[:, SSM_WIDTH:])
    y_ssm = jnp.dot(y_s.astype(BF16), wssm_ref[...], preferred_element_type=F32)
    mixed = gates[:, :dm] * y_attn + gates[:, dm:] * y_ssm
    mix_out = jnp.dot(mixed.astype(BF16), wout_ref[...], preferred_element_type=F32)
    o_ref[...] = _layer_norm(alpha * x + mix_out, lng_ref[...], lnb_ref[...])


def _mix(x2, attn2, g2, wgate, bgate, wattn, wglu, wssm, wout, lng, lnb, alpha, tm=512):
    rows, dm = x2.shape
    row_spec = lambda w: pl.BlockSpec((tm, w), lambda i: (i, 0))
    consts = [wgate, bgate, wattn, wglu, wssm, wout, lng, lnb]
    return pl.pallas_call(
        functools.partial(_mix_kernel, alpha=alpha),
        grid=(rows // tm,),
        in_specs=[row_spec(dm), row_spec(ATTN_WIDTH), row_spec(SSM_WIDTH)]
                 + [_const_spec(c.shape) for c in consts],
        out_specs=row_spec(dm),
        out_shape=jax.ShapeDtypeStruct((rows, dm), F32),
        compiler_params=pltpu.CompilerParams(
            dimension_semantics=("parallel",), vmem_limit_bytes=VMEM_LIMIT),
        name="mix",
    )(x2, attn2, g2, *consts)


def _ffn_kernel(h_ref, wg_ref, wu_ref, wd_ref, lng_ref, lnb_ref, o_ref, *, alpha):
    h = h_ref[...]
    hb = h.astype(BF16)
    gate = jnp.dot(hb, wg_ref[...], preferred_element_type=F32)
    up = jnp.dot(hb, wu_ref[...], preferred_element_type=F32)
    act = (gate * jax.nn.sigmoid(gate)) * up
    ff = jnp.dot(act.astype(BF16), wd_ref[...], preferred_element_type=F32)
    o_ref[...] = _layer_norm(alpha * h + ff, lng_ref[...], lnb_ref[...])


def _ffn(h2, wg, wu, wd, lng, lnb, alpha, tm=512):
    rows, dm = h2.shape
    row_spec = pl.BlockSpec((tm, dm), lambda i: (i, 0))
    consts = [wg, wu, wd, lng, lnb]
    return pl.pallas_call(
        functools.partial(_ffn_kernel, alpha=alpha),
        grid=(rows // tm,),
        in_specs=[row_spec] + [_const_spec(c.shape) for c in consts],
        out_specs=row_spec,
        out_shape=jax.ShapeDtypeStruct((rows, dm), F32),
        compiler_params=pltpu.CompilerParams(
            dimension_semantics=("parallel",), vmem_limit_bytes=VMEM_LIMIT),
        name="ffn",
    )(h2, *consts)


def _rope_tables(s):
    half = HEAD_DIM // 2
    pos = jnp.arange(s, dtype=F32)
    inv_freq = ROPE_THETA ** (-jnp.arange(half, dtype=F32) / half)
    ang = pos[:, None] * inv_freq[None, :]
    cos, sin = jnp.cos(ang), jnp.sin(ang)
    reps = LANES // HEAD_DIM
    cos_t = jnp.concatenate([cos, cos] * reps, axis=1)
    sin_t = jnp.concatenate([-sin, sin] * reps, axis=1)
    return cos_t, sin_t


def kernel(x, w_in, b_gate, w_attn_br, w_ssm_br, w_out, ssm_a_re, ssm_a_im, ssm_log_dt, ssm_b_re, ssm_b_im, ssm_c_re, ssm_c_im, ssm_d, w_glu, ln1_g, ln1_b, w_ff_gate, w_ff_up, w_ff_down, ln2_g, ln2_b):
    bsz, s, dm = x.shape
    depth = w_in.shape[0]
    alpha = (2.0 * depth) ** 0.25
    assert all(w // d == ATTN_BLOCK for w, d in zip(WINDOWS, DILATIONS))
    assert s % (ATTN_BLOCK * max(DILATIONS)) == 0
    cos_t, sin_t = _rope_tables(s)
    n_proj = 3 * QKV_WIDTH + SSM_WIDTH

    for layer in range(depth):
        wgate = w_in[layer, :, n_proj:].astype(BF16)

        outs = _project(x, cos_t, sin_t, w_in[layer, :, :n_proj].astype(BF16))
        attn = _attention(outs[:N_GROUPS], bsz, s)

        bm, a_re, a_im, cm = _ssm_matrices(
            ssm_a_re[layer], ssm_a_im[layer], ssm_log_dt[layer], ssm_b_re[layer],
            ssm_b_im[layer], ssm_c_re[layer], ssm_c_im[layer], bsz)
        g = _ssm(outs[N_GROUPS], bm, a_re, a_im, cm,
                 ssm_d[layer].reshape(1, SSM_WIDTH).astype(F32))

        h = _mix(x.reshape(bsz * s, dm), attn.reshape(bsz * s, ATTN_WIDTH),
                 g.reshape(bsz * s, SSM_WIDTH), wgate,
                 b_gate[layer].reshape(1, 2 * dm).astype(F32),
                 w_attn_br[layer].astype(BF16), w_glu[layer].astype(BF16),
                 w_ssm_br[layer].astype(BF16), w_out[layer].astype(BF16),
                 ln1_g[layer].reshape(1, dm).astype(F32), ln1_b[layer].reshape(1, dm).astype(F32),
                 alpha)
        x2 = _ffn(h, w_ff_gate[layer].astype(BF16), w_ff_up[layer].astype(BF16),
                  w_ff_down[layer].astype(BF16),
                  ln2_g[layer].reshape(1, dm).astype(F32), ln2_b[layer].reshape(1, dm).astype(F32),
                  alpha)
        x = x2.reshape(bsz, s, dm)
    return x
```

```python
import functools
import math

import jax
import jax.numpy as jnp
import numpy as np
from jax import lax
from jax.experimental import pallas as pl
from jax.experimental.pallas import tpu as pltpu

F32 = jnp.float32
BF16 = jnp.bfloat16

LANES = 128
HEAD_DIM = 64
ATTN_HEADS = 8
ATTN_WIDTH = ATTN_HEADS * HEAD_DIM
DILATIONS = (1, 4, 16)
WINDOWS = (128, 512, 2048)
N_GROUPS = len(DILATIONS)
QKV_WIDTH = N_GROUPS * ATTN_WIDTH
ATTN_BLOCK = 128
ROPE_THETA = 10000.0
NEG_INF = -1e30
Q_SCALE = math.log2(math.e) / math.sqrt(HEAD_DIM)
SSM_GROUP = 16
SSM_GROUPS = 32
SSM_WIDTH = SSM_GROUP * SSM_GROUPS
SSM_STATE = 64
SSM_PAIRS = SSM_GROUPS // 2
LN_EPS = 1e-5

VMEM_LIMIT = 56 * 1024 * 1024


def _const_spec(shape):
    zeros = (0,) * len(shape)
    return pl.BlockSpec(shape, lambda *_: zeros, pipeline_mode=pl.Buffered(1))


def _phase_major(ref, tm, d):
    if d == 1:
        return ref[...]
    n = tm // d
    return jnp.concatenate([ref[pl.ds(r, n, stride=d), :] for r in range(d)], axis=0)


def _proj_kernel(*refs, tm, n_slabs):
    x_refs = refs[:n_slabs]
    cos_ref, sin_ref, w_ref = refs[n_slabs:n_slabs + 3]
    o_refs = refs[n_slabs + 3:n_slabs + 3 + N_GROUPS]
    u_ref = refs[n_slabs + 3 + N_GROUPS]

    lane = lax.broadcasted_iota(jnp.int32, (1, LANES), 1)
    first_half = (lane % HEAD_DIM) < (HEAD_DIM // 2)

    for gi, d in enumerate(DILATIONS):
        n = tm // d
        xg = jnp.concatenate(
            [_phase_major(xr, tm, d).astype(BF16) for xr in x_refs], axis=1)
        cos = _phase_major(cos_ref, tm, d)
        sin = _phase_major(sin_ref, tm, d)
        if gi == 0:
            u_ref[...] = jnp.dot(xg, w_ref[:, 3 * QKV_WIDTH:], preferred_element_type=F32)
        for part in range(3):
            c0 = part * ATTN_WIDTH
            w0 = part * QKV_WIDTH + gi * ATTN_WIDTH
            acc = jnp.dot(xg, w_ref[:, w0:w0 + ATTN_WIDTH], preferred_element_type=F32)
            for c in range(ATTN_WIDTH // LANES):
                t = acc[:, c * LANES:(c + 1) * LANES]
                if part < 2:
                    rot = jnp.where(first_half,
                                    pltpu.roll(t, LANES - HEAD_DIM // 2, 1),
                                    pltpu.roll(t, HEAD_DIM // 2, 1))
                    t = t * cos + rot * sin
                if part == 0:
                    t = t * Q_SCALE
                t = t.astype(BF16)
                col = c0 + c * LANES
                if d == 1:
                    o_refs[gi][:, col:col + LANES] = t
                else:
                    for r in range(d):
                        o_refs[gi][r, :, col:col + LANES] = t[r * n:(r + 1) * n]


def _project(x, cos_t, sin_t, w, tm=512):
    b, s, dm = x.shape
    n_slabs = dm // LANES
    grid = (b, s // tm)
    x_specs = [pl.BlockSpec((None, tm, LANES), functools.partial(lambda bi, i, c: (bi, i, c), c=c))
               for c in range(n_slabs)]
    tab_spec = pl.BlockSpec((tm, LANES), lambda bi, i: (i, 0))
    out_shapes = []
    out_specs = []
    for d in DILATIONS:
        if d == 1:
            out_shapes.append(jax.ShapeDtypeStruct((b, s, QKV_WIDTH), BF16))
            out_specs.append(pl.BlockSpec((None, tm, QKV_WIDTH), lambda bi, i: (bi, i, 0)))
        else:
            out_shapes.append(jax.ShapeDtypeStruct((b, d, s // d, QKV_WIDTH), BF16))
            out_specs.append(pl.BlockSpec((None, d, tm // d, QKV_WIDTH),
                                          lambda bi, i: (bi, 0, i, 0)))
    out_shapes.append(jax.ShapeDtypeStruct((b, s, SSM_WIDTH), F32))
    out_specs.append(pl.BlockSpec((None, tm, SSM_WIDTH), lambda bi, i: (bi, i, 0)))
    return pl.pallas_call(
        functools.partial(_proj_kernel, tm=tm, n_slabs=n_slabs),
        grid=grid,
        in_specs=x_specs + [tab_spec, tab_spec, _const_spec(w.shape)],
        out_specs=out_specs,
        out_shape=out_shapes,
        compiler_params=pltpu.CompilerParams(
            dimension_semantics=("parallel", "parallel"), vmem_limit_bytes=VMEM_LIMIT),
        name="in_proj",
    )(*([x] * n_slabs), cos_t, sin_t, w)


def _attn_bias():
    blk = ATTN_BLOCK
    a = (np.arange(2 * blk) % blk)[:, None]
    c = np.arange(2 * blk)[None, :]
    in_window = np.where(c < blk, c >= a, c - blk <= a)
    windowed = np.where(in_window, 0.0, NEG_INF)
    causal = np.where(c[:, :blk] <= a, 0.0, NEG_INF)
    return windowed.astype(np.float32), causal.astype(np.float32)


def _attn_kernel(*refs, s, lags):
    qkv = refs[:3 * N_GROUPS]
    mask_w_ref, mask_c_ref = refs[3 * N_GROUPS:3 * N_GROUPS + 2]
    o_ref = refs[3 * N_GROUPS + 2]
    acc_s, m_s, l_s = refs[3 * N_GROUPS + 3:]
    blk = ATTN_BLOCK
    n_blocks = s // blk

    lane = lax.broadcasted_iota(jnp.int32, (1, LANES), 1)
    head_a = lane < HEAD_DIM

    def window(ref, i, nblk):
        start = i if i % nblk == 0 else i - 1
        return ref[start * blk:(i + 1) * blk, :]

    def scores(gi, i, nblk):
        q_ref, k_ref, _ = qkv[3 * gi:3 * gi + 3]
        q = q_ref[i * blk:(i + 1) * blk, :]
        zero = jnp.zeros_like(q)
        q2 = jnp.concatenate([jnp.where(head_a, q, zero), jnp.where(head_a, zero, q)], axis=0)
        mask = mask_c_ref[...] if i % nblk == 0 else mask_w_ref[...]
        return lax.dot_general(q2, window(k_ref, i, nblk), (((1,), (1,)), ((), ())),
                               preferred_element_type=F32) + mask

    def softmax(sc):
        m = jnp.max(sc, axis=1, keepdims=True)
        return m, jnp.exp2((sc - m).astype(BF16))

    def finish(gi, d, i, nblk, m, p):
        vw = window(qkv[3 * gi + 2], i, nblk)
        v_ext = jnp.concatenate([vw, jnp.ones_like(vw)], axis=1)
        res = jnp.dot(p, v_ext, preferred_element_type=F32)
        acc = jnp.where(head_a, res[:blk, :LANES], res[blk:, :LANES])
        l = jnp.where(head_a, res[:blk, LANES:], res[blk:, LANES:])
        m = jnp.where(head_a, m[:blk], m[blk:])
        if d > 1:
            r, m0 = divmod(i * blk, s // d)
            rows = pl.ds(m0 * d + r, blk, stride=d)
            acc_s[gi - 1, rows, :] = acc
            l_s[gi - 1, rows, :] = l
            m_s[gi - 1, rows, :] = m
            return
        rows = slice(i * blk, (i + 1) * blk)
        parts = [(m, l, acc)] + [(m_s[g, rows, :], l_s[g, rows, :], acc_s[g, rows, :])
                                 for g in range(N_GROUPS - 1)]
        m_all = functools.reduce(jnp.maximum, [pm for pm, _, _ in parts])
        num = jnp.zeros((blk, LANES), F32)
        den = jnp.zeros((blk, LANES), F32)
        for pm, pl_, pacc in parts:
            wgt = jnp.exp2(pm - m_all)
            num = num + wgt * pacc
            den = den + wgt * pl_
        o_ref[rows, :] = (num / den).astype(o_ref.dtype)

    assert DILATIONS[0] == 1
    work = [(gi, DILATIONS[gi], i, (s // DILATIONS[gi]) // blk)
            for gi in reversed(range(N_GROUPS)) for i in range(n_blocks)]
    lag_softmax, lag_values = lags
    sc_q, p_q = {}, {}
    for step in range(len(work) + lag_values):
        if step < len(work):
            gi, d, i, nblk = work[step]
            sc_q[step] = scores(gi, i, nblk)
        if 0 <= step - lag_softmax < len(work):
            p_q[step - lag_softmax] = softmax(sc_q.pop(step - lag_softmax))
        if step >= lag_values:
            gi, d, i, nblk = work[step - lag_values]
            finish(gi, d, i, nblk, *p_q.pop(step - lag_values))


def _attention(qkv_groups, b, s, lags=(1, 4)):
    n_hp = ATTN_WIDTH // LANES
    in_specs, args = [], []
    for gi, d in enumerate(DILATIONS):
        arr = qkv_groups[gi].reshape(b, s, QKV_WIDTH)
        for part in range(3):
            in_specs.append(pl.BlockSpec(
                (None, s, LANES),
                functools.partial(lambda bi, hp, part: (bi, 0, part * n_hp + hp), part=part)))
            args.append(arr)
    masks = _attn_bias()
    return pl.pallas_call(
        functools.partial(_attn_kernel, s=s, lags=lags),
        grid=(b, n_hp),
        in_specs=in_specs + [_const_spec(t.shape) for t in masks],
        out_specs=pl.BlockSpec((None, s, LANES), lambda bi, hp: (bi, 0, hp)),
        out_shape=jax.ShapeDtypeStruct((b, s, ATTN_WIDTH), BF16),
        scratch_shapes=[pltpu.VMEM((N_GROUPS - 1, s, LANES), F32)] * 3,
        compiler_params=pltpu.CompilerParams(
            dimension_semantics=("parallel", "parallel"), vmem_limit_bytes=VMEM_LIMIT),
        name="attention",
    )(*args, *masks)


def _gelu_tanh(x):
    return 0.5 * x * (1.0 + jnp.tanh(math.sqrt(2.0 / math.pi) * (x + 0.044715 * (x * x * x))))


def _ssm_kernel(u_ref, bm_ref, are_ref, aim_ref, cm_ref, dskip_ref, o_ref,
                st_a, st_b, ub_s, y_s, h_ref, *, tc, bsz):
    rows = tc * bsz
    half = SSM_PAIRS // 2
    pairs_per_slab = LANES // (2 * SSM_GROUP)
    n_slabs = SSM_WIDTH // LANES
    splits = 2
    n_slices = half * splits
    steps = tc // n_slices
    srows = rows // splits

    @pl.when(pl.program_id(0) == 0)
    def _():
        h_ref[...] = jnp.zeros_like(h_ref)

    for c in range(n_slabs):
        cols = slice(c * LANES, (c + 1) * LANES)
        for b in range(bsz):
            y_s[c, pl.ds(b, tc, stride=bsz), :] = u_ref[b, :, cols]
        u = y_s[c]
        ub_s[c] = u.astype(BF16)
        y_s[c] = dskip_ref[:, cols] * u

    def project_in(buf, j, jl, rs):
        bu = jnp.dot(ub_s[j // pairs_per_slab, rs, :], bm_ref[j], preferred_element_type=F32)
        buf[0, jl, rs, :] = bu[:, :LANES]
        buf[1, jl, rs, :] = bu[:, LANES:]

    def project_out(buf, j, jl, rs):
        st = jnp.concatenate([buf[0, jl, rs, :].astype(BF16), buf[1, jl, rs, :].astype(BF16)],
                             axis=1)
        y_s[j // pairs_per_slab, rs, :] += jnp.dot(st, cm_ref[j], preferred_element_type=F32)

    def scan(buf, j0, other):
        a_re = [are_ref[j0 + i] for i in range(half)]
        a_im = [aim_ref[j0 + i] for i in range(half)]

        def body(it, carry):
            h_re, h_im = carry
            for k in range(steps):
                r0 = pl.multiple_of((it * steps + k) * bsz, bsz)
                new_re, new_im = [], []
                for i in range(half):
                    n_re = a_re[i] * h_re[i] - a_im[i] * h_im[i] + buf[0, i, pl.ds(r0, bsz), :]
                    n_im = a_re[i] * h_im[i] + a_im[i] * h_re[i] + buf[1, i, pl.ds(r0, bsz), :]
                    buf[0, i, pl.ds(r0, bsz), :] = n_re
                    buf[1, i, pl.ds(r0, bsz), :] = n_im
                    new_re.append(n_re)
                    new_im.append(n_im)
                h_re, h_im = tuple(new_re), tuple(new_im)
            other(it // splits, pl.ds((it % splits) * srows, srows))
            return h_re, h_im

        carry = (tuple(h_ref[0, j0 + i] for i in range(half)),
                 tuple(h_ref[1, j0 + i] for i in range(half)))
        for it in range(n_slices):
            carry = body(it, carry)
        h_re, h_im = carry
        for i in range(half):
            h_ref[0, j0 + i] = h_re[i]
            h_ref[1, j0 + i] = h_im[i]

    for jl in range(half):
        project_in(st_a, jl, jl, slice(None))
    scan(st_a, 0, lambda jl, rs: project_in(st_b, half + jl, jl, rs))
    scan(st_b, half, lambda jl, rs: project_out(st_a, jl, jl, rs))
    for jl in range(half):
        project_out(st_b, half + jl, jl, slice(None))

    for c in range(n_slabs):
        for b in range(bsz):
            y = y_s[c, pl.ds(b, tc, stride=bsz), :]
            o_ref[b, :, c * LANES:(c + 1) * LANES] = _gelu_tanh(y).astype(o_ref.dtype)


def _ssm(u, bm, a_re, a_im, cm, d_skip, tc=128):
    bsz, s, _ = u.shape
    rows = tc * bsz
    half = SSM_PAIRS // 2
    chunk_spec = pl.BlockSpec((bsz, tc, SSM_WIDTH), lambda i: (0, i, 0))
    return pl.pallas_call(
        functools.partial(_ssm_kernel, tc=tc, bsz=bsz),
        grid=(s // tc,),
        in_specs=[chunk_spec,
                  _const_spec(bm.shape), _const_spec(a_re.shape), _const_spec(a_im.shape),
                  _const_spec(cm.shape), _const_spec(d_skip.shape)],
        out_specs=chunk_spec,
        out_shape=jax.ShapeDtypeStruct((bsz, s, SSM_WIDTH), BF16),
        scratch_shapes=[pltpu.VMEM((2, half, rows, LANES), F32),
                        pltpu.VMEM((2, half, rows, LANES), F32),
                        pltpu.VMEM((SSM_WIDTH // LANES, rows, LANES), BF16),
                        pltpu.VMEM((SSM_WIDTH // LANES, rows, LANES), F32),
                        pltpu.VMEM((2, SSM_PAIRS, bsz, LANES), F32)],
        compiler_params=pltpu.CompilerParams(
            dimension_semantics=("arbitrary",), vmem_limit_bytes=VMEM_LIMIT),
        name="ssm",
    )(u, bm, a_re, a_im, cm, d_skip)


def _ssm_matrices(a_re, a_im, log_dt, b_re, b_im, c_re, c_im, bsz):
    g, p, h = SSM_GROUPS, SSM_STATE, SSM_GROUP
    lam = lax.complex(a_re.astype(F32), a_im.astype(F32))
    dt = jnp.exp(log_dt.astype(F32))[:, None]
    a_bar = jnp.exp(lam * dt)
    b_bar = ((a_bar - 1.0) / lam)[..., None] * lax.complex(b_re.astype(F32), b_im.astype(F32))
    groups_per_slab = LANES // h
    pairs_per_slab = groups_per_slab // 2
    eye = jnp.eye(groups_per_slab, dtype=F32)

    def in_matrix(bpart):
        bt = jnp.transpose(bpart, (0, 2, 1)).reshape(g // groups_per_slab, groups_per_slab, h, p)
        dense = bt[:, :, :, None, :] * eye[None, :, None, :, None]
        dense = dense.reshape(g // groups_per_slab, LANES, pairs_per_slab, 2 * p)
        return jnp.transpose(dense, (0, 2, 1, 3)).reshape(SSM_PAIRS, LANES, 2 * p)

    bm = jnp.concatenate([in_matrix(b_bar.real), in_matrix(b_bar.imag)], axis=2)

    def out_matrix(cpart):
        ct = jnp.transpose(cpart, (0, 2, 1)).reshape(g // groups_per_slab, groups_per_slab, p, h)
        dense = ct[:, :, :, None, :] * eye[None, :, None, :, None]
        dense = dense.reshape(g // groups_per_slab, pairs_per_slab, 2 * p, LANES)
        return dense.reshape(SSM_PAIRS, 2 * p, LANES)

    cm = jnp.concatenate([out_matrix(c_re.astype(F32)), out_matrix(-c_im.astype(F32))], axis=1)

    def bcast(apart):
        return jnp.broadcast_to(apart.reshape(SSM_PAIRS, 1, 2 * p), (SSM_PAIRS, bsz, 2 * p))

    return bm.astype(BF16), bcast(a_bar.real), bcast(a_bar.imag), cm.astype(BF16)


def _layer_norm(z, g, b):
    mu = jnp.mean(z, axis=-1, keepdims=True)
    zc = z - mu
    var = jnp.mean(zc * zc, axis=-1, keepdims=True)
    return zc * lax.rsqrt(var + LN_EPS) * g + b


def _mix_kernel(x_ref, attn_ref, g_ref, wgate_ref, bgate_ref, wattn_ref, wglu_ref,
                wssm_ref, wout_ref, lng_ref, lnb_ref, o_ref, *, alpha, sub):
    dm = x_ref.shape[1]
    for r0 in range(0, x_ref.shape[0], sub):
        rows = slice(r0, r0 + sub)
        x = x_ref[rows, :]
        logits = jnp.dot(x.astype(BF16), wgate_ref[...], preferred_element_type=F32) + bgate_ref[...]
        gates = jax.nn.sigmoid(logits)
        y_attn = jnp.dot(attn_ref[rows, :], wattn_ref[...], preferred_element_type=F32)
        glu = jnp.dot(g_ref[rows, :], wglu_ref[...], preferred_element_type=F32)
        y_s = glu[:, :SSM_WIDTH] * jax.nn.sigmoid(glu[:, SSM_WIDTH:])
        y_ssm = jnp.dot(y_s.astype(BF16), wssm_ref[...], preferred_element_type=F32)
        mixed = gates[:, :dm] * y_attn + gates[:, dm:] * y_ssm
        mix_out = jnp.dot(mixed.astype(BF16), wout_ref[...], preferred_element_type=F32)
        o_ref[rows, :] = _layer_norm(alpha * x + mix_out, lng_ref[...], lnb_ref[...])


def _mix(x2, attn2, g2, wgate, bgate, wattn, wglu, wssm, wout, lng, lnb, alpha, tm=1024, sub=512):
    rows, dm = x2.shape
    row_spec = lambda w: pl.BlockSpec((tm, w), lambda i: (i, 0))
    consts = [wgate, bgate, wattn, wglu, wssm, wout, lng, lnb]
    return pl.pallas_call(
        functools.partial(_mix_kernel, alpha=alpha, sub=sub),
        grid=(rows // tm,),
        in_specs=[row_spec(dm), row_spec(ATTN_WIDTH), row_spec(SSM_WIDTH)]
                 + [_const_spec(c.shape) for c in consts],
        out_specs=row_spec(dm),
        out_shape=jax.ShapeDtypeStruct((rows, dm), F32),
        compiler_params=pltpu.CompilerParams(
            dimension_semantics=("parallel",), vmem_limit_bytes=VMEM_LIMIT),
        name="mix",
    )(x2, attn2, g2, *consts)


def _ffn_kernel(h_ref, wg_ref, wu_ref, wd_ref, lng_ref, lnb_ref, o_ref, *, alpha, sub):
    for r0 in range(0, h_ref.shape[0], sub):
        rows = slice(r0, r0 + sub)
        h = h_ref[rows, :]
        hb = h.astype(BF16)
        gate = jnp.dot(hb, wg_ref[...], preferred_element_type=F32)
        up = jnp.dot(hb, wu_ref[...], preferred_element_type=F32)
        act = (gate * jax.nn.sigmoid(gate)) * up
        ff = jnp.dot(act.astype(BF16), wd_ref[...], preferred_element_type=F32)
        o_ref[rows, :] = _layer_norm(alpha * h + ff, lng_ref[...], lnb_ref[...])


def _ffn(h2, wg, wu, wd, lng, lnb, alpha, tm=1024, sub=512):
    rows, dm = h2.shape
    row_spec = pl.BlockSpec((tm, dm), lambda i: (i, 0))
    consts = [wg, wu, wd, lng, lnb]
    return pl.pallas_call(
        functools.partial(_ffn_kernel, alpha=alpha, sub=sub),
        grid=(rows // tm,),
        in_specs=[row_spec] + [_const_spec(c.shape) for c in consts],
        out_specs=row_spec,
        out_shape=jax.ShapeDtypeStruct((rows, dm), F32),
        compiler_params=pltpu.CompilerParams(
            dimension_semantics=("parallel",), vmem_limit_bytes=VMEM_LIMIT),
        name="ffn",
    )(h2, *consts)


def _rope_tables(s):
    half = HEAD_DIM // 2
    pos = jnp.arange(s, dtype=F32)
    inv_freq = ROPE_THETA ** (-jnp.arange(half, dtype=F32) / half)
    ang = pos[:, None] * inv_freq[None, :]
    cos, sin = jnp.cos(ang), jnp.sin(ang)
    reps = LANES // HEAD_DIM
    cos_t = jnp.concatenate([cos, cos] * reps, axis=1)
    sin_t = jnp.concatenate([-sin, sin] * reps, axis=1)
    return cos_t, sin_t


def kernel(x, w_in, b_gate, w_attn_br, w_ssm_br, w_out, ssm_a_re, ssm_a_im, ssm_log_dt, ssm_b_re, ssm_b_im, ssm_c_re, ssm_c_im, ssm_d, w_glu, ln1_g, ln1_b, w_ff_gate, w_ff_up, w_ff_down, ln2_g, ln2_b):
    bsz, s, dm = x.shape
    depth = w_in.shape[0]
    alpha = (2.0 * depth) ** 0.25
    assert all(w // d == ATTN_BLOCK for w, d in zip(WINDOWS, DILATIONS))
    assert s % (ATTN_BLOCK * max(DILATIONS)) == 0
    cos_t, sin_t = _rope_tables(s)
    n_proj = 3 * QKV_WIDTH + SSM_WIDTH

    for layer in range(depth):
        wgate = w_in[layer, :, n_proj:].astype(BF16)

        outs = _project(x, cos_t, sin_t, w_in[layer, :, :n_proj].astype(BF16))
        attn = _attention(outs[:N_GROUPS], bsz, s)

        bm, a_re, a_im, cm = _ssm_matrices(
            ssm_a_re[layer], ssm_a_im[layer], ssm_log_dt[layer], ssm_b_re[layer],
            ssm_b_im[layer], ssm_c_re[layer], ssm_c_im[layer], bsz)
        g = _ssm(outs[N_GROUPS], bm, a_re, a_im, cm,
                 ssm_d[layer].reshape(1, SSM_WIDTH).astype(F32))

        h = _mix(x.reshape(bsz * s, dm), attn.reshape(bsz * s, ATTN_WIDTH),
                 g.reshape(bsz * s, SSM_WIDTH), wgate,
                 b_gate[layer].reshape(1, 2 * dm).astype(F32),
                 w_attn_br[layer].astype(BF16), w_glu[layer].astype(BF16),
                 w_ssm_br[layer].astype(BF16), w_out[layer].astype(BF16),
                 ln1_g[layer].reshape(1, dm).astype(F32), ln1_b[layer].reshape(1, dm).astype(F32),
                 alpha)
        x2 = _ffn(h, w_ff_gate[layer].astype(BF16), w_ff_up[layer].astype(BF16),
                  w_ff_down[layer].astype(BF16),
                  ln2_g[layer].reshape(1, dm).astype(F32), ln2_b[layer].reshape(1, dm).astype(F32),
                  alpha)
        x = x2.reshape(bsz, s, dm)
    return x
```

```python
import functools
import math

import jax
import jax.numpy as jnp
import numpy as np
from jax import lax
from jax.experimental import pallas as pl
from jax.experimental.pallas import tpu as pltpu

F32 = jnp.float32
BF16 = jnp.bfloat16

LANES = 128
HEAD_DIM = 64
ATTN_HEADS = 8
ATTN_WIDTH = ATTN_HEADS * HEAD_DIM
DILATIONS = (1, 4, 16)
WINDOWS = (128, 512, 2048)
N_GROUPS = len(DILATIONS)
QKV_WIDTH = N_GROUPS * ATTN_WIDTH
ATTN_BLOCK = 128
ROPE_THETA = 10000.0
NEG_INF = -1e30
Q_SCALE = math.log2(math.e) / math.sqrt(HEAD_DIM)
SSM_GROUP = 16
SSM_GROUPS = 32
SSM_WIDTH = SSM_GROUP * SSM_GROUPS
SSM_STATE = 64
SSM_PAIRS = SSM_GROUPS // 2
LN_EPS = 1e-5

VMEM_LIMIT = 56 * 1024 * 1024


def _const_spec(shape):
    zeros = (0,) * len(shape)
    return pl.BlockSpec(shape, lambda *_: zeros, pipeline_mode=pl.Buffered(1))


def _phase_major(ref, tm, d):
    if d == 1:
        return ref[...]
    n = tm // d
    return jnp.concatenate([ref[pl.ds(r, n, stride=d), :] for r in range(d)], axis=0)


def _proj_kernel(*refs, tm, n_slabs):
    x_refs = refs[:n_slabs]
    cos_ref, sin_ref, w_ref = refs[n_slabs:n_slabs + 3]
    o_refs = refs[n_slabs + 3:n_slabs + 3 + N_GROUPS]
    u_ref = refs[n_slabs + 3 + N_GROUPS]

    lane = lax.broadcasted_iota(jnp.int32, (1, LANES), 1)
    first_half = (lane % HEAD_DIM) < (HEAD_DIM // 2)

    for gi, d in enumerate(DILATIONS):
        n = tm // d
        xg = jnp.concatenate(
            [_phase_major(xr, tm, d).astype(BF16) for xr in x_refs], axis=1)
        cos = _phase_major(cos_ref, tm, d)
        sin = _phase_major(sin_ref, tm, d)
        if gi == 0:
            u_ref[...] = jnp.dot(xg, w_ref[:, 3 * QKV_WIDTH:], preferred_element_type=F32)
        for part in range(3):
            c0 = part * ATTN_WIDTH
            w0 = part * QKV_WIDTH + gi * ATTN_WIDTH
            acc = jnp.dot(xg, w_ref[:, w0:w0 + ATTN_WIDTH], preferred_element_type=F32)
            for c in range(ATTN_WIDTH // LANES):
                t = acc[:, c * LANES:(c + 1) * LANES]
                if part < 2:
                    rot = jnp.where(first_half,
                                    pltpu.roll(t, LANES - HEAD_DIM // 2, 1),
                                    pltpu.roll(t, HEAD_DIM // 2, 1))
                    t = t * cos + rot * sin
                if part == 0:
                    t = t * Q_SCALE
                t = t.astype(BF16)
                col = c0 + c * LANES
                if d == 1:
                    o_refs[gi][:, col:col + LANES] = t
                else:
                    for r in range(d):
                        o_refs[gi][r, :, col:col + LANES] = t[r * n:(r + 1) * n]


def _project(x, cos_t, sin_t, w, tm=512):
    b, s, dm = x.shape
    n_slabs = dm // LANES
    grid = (b, s // tm)
    x_specs = [pl.BlockSpec((None, tm, LANES), functools.partial(lambda bi, i, c: (bi, i, c), c=c))
               for c in range(n_slabs)]
    tab_spec = pl.BlockSpec((tm, LANES), lambda bi, i: (i, 0))
    out_shapes = []
    out_specs = []
    for d in DILATIONS:
        if d == 1:
            out_shapes.append(jax.ShapeDtypeStruct((b, s, QKV_WIDTH), BF16))
            out_specs.append(pl.BlockSpec((None, tm, QKV_WIDTH), lambda bi, i: (bi, i, 0)))
        else:
            out_shapes.append(jax.ShapeDtypeStruct((b, d, s // d, QKV_WIDTH), BF16))
            out_specs.append(pl.BlockSpec((None, d, tm // d, QKV_WIDTH),
                                          lambda bi, i: (bi, 0, i, 0)))
    out_shapes.append(jax.ShapeDtypeStruct((b, s, SSM_WIDTH), F32))
    out_specs.append(pl.BlockSpec((None, tm, SSM_WIDTH), lambda bi, i: (bi, i, 0)))
    return pl.pallas_call(
        functools.partial(_proj_kernel, tm=tm, n_slabs=n_slabs),
        grid=grid,
        in_specs=x_specs + [tab_spec, tab_spec,
                            pl.BlockSpec((dm, 3 * QKV_WIDTH + SSM_WIDTH), lambda bi, i: (0, 0),
                                         pipeline_mode=pl.Buffered(1))],
        out_specs=out_specs,
        out_shape=out_shapes,
        compiler_params=pltpu.CompilerParams(
            dimension_semantics=("parallel", "parallel"), vmem_limit_bytes=VMEM_LIMIT),
        name="in_proj",
    )(*([x] * n_slabs), cos_t, sin_t, w)


def _attn_bias():
    blk = ATTN_BLOCK
    a = (np.arange(2 * blk) % blk)[:, None]
    c = np.arange(2 * blk)[None, :]
    in_window = np.where(c < blk, c >= a, c - blk <= a)
    windowed = np.where(in_window, 0.0, NEG_INF)
    causal = np.where(c[:, :blk] <= a, 0.0, NEG_INF)
    return windowed.astype(np.float32), causal.astype(np.float32)


def _attn_kernel(*refs, s, lags):
    qkv = refs[:3 * N_GROUPS]
    mask_w_ref, mask_c_ref = refs[3 * N_GROUPS:3 * N_GROUPS + 2]
    o_ref = refs[3 * N_GROUPS + 2]
    acc_s, m_s, l_s = refs[3 * N_GROUPS + 3:]
    blk = ATTN_BLOCK
    n_blocks = s // blk

    lane = lax.broadcasted_iota(jnp.int32, (1, LANES), 1)
    head_a = lane < HEAD_DIM

    def window(ref, i, nblk):
        start = i if i % nblk == 0 else i - 1
        return ref[start * blk:(i + 1) * blk, :]

    def scores(gi, i, nblk):
        q_ref, k_ref, _ = qkv[3 * gi:3 * gi + 3]
        q = q_ref[i * blk:(i + 1) * blk, :]
        zero = jnp.zeros_like(q)
        q2 = jnp.concatenate([jnp.where(head_a, q, zero), jnp.where(head_a, zero, q)], axis=0)
        mask = mask_c_ref[...] if i % nblk == 0 else mask_w_ref[...]
        return lax.dot_general(q2, window(k_ref, i, nblk), (((1,), (1,)), ((), ())),
                               preferred_element_type=F32) + mask

    def softmax(sc):
        m = jnp.max(sc, axis=1, keepdims=True)
        return m, jnp.exp2((sc - m).astype(BF16))

    def finish(gi, d, i, nblk, m, p):
        vw = window(qkv[3 * gi + 2], i, nblk)
        v_ext = jnp.concatenate([vw, jnp.ones_like(vw)], axis=1)
        res = jnp.dot(p, v_ext, preferred_element_type=F32)
        acc = jnp.where(head_a, res[:blk, :LANES], res[blk:, :LANES])
        l = jnp.where(head_a, res[:blk, LANES:], res[blk:, LANES:])
        m = jnp.where(head_a, m[:blk], m[blk:])
        if d > 1:
            r, m0 = divmod(i * blk, s // d)
            rows = pl.ds(m0 * d + r, blk, stride=d)
            acc_s[gi - 1, rows, :] = acc
            l_s[gi - 1, rows, :] = l
            m_s[gi - 1, rows, :] = m
            return
        rows = slice(i * blk, (i + 1) * blk)
        parts = [(m, l, acc)] + [(m_s[g, rows, :], l_s[g, rows, :], acc_s[g, rows, :])
                                 for g in range(N_GROUPS - 1)]
        m_all = functools.reduce(jnp.maximum, [pm for pm, _, _ in parts])
        num = jnp.zeros((blk, LANES), F32)
        den = jnp.zeros((blk, LANES), F32)
        for pm, pl_, pacc in parts:
            wgt = jnp.exp2(pm - m_all)
            num = num + wgt * pacc
            den = den + wgt * pl_
        o_ref[rows, :] = (num / den).astype(o_ref.dtype)

    assert DILATIONS[0] == 1
    work = [(gi, DILATIONS[gi], i, (s // DILATIONS[gi]) // blk)
            for gi in reversed(range(N_GROUPS)) for i in range(n_blocks)]
    lag_softmax, lag_values = lags
    sc_q, p_q = {}, {}
    for step in range(len(work) + lag_values):
        if step < len(work):
            gi, d, i, nblk = work[step]
            sc_q[step] = scores(gi, i, nblk)
        if 0 <= step - lag_softmax < len(work):
            p_q[step - lag_softmax] = softmax(sc_q.pop(step - lag_softmax))
        if step >= lag_values:
            gi, d, i, nblk = work[step - lag_values]
            finish(gi, d, i, nblk, *p_q.pop(step - lag_values))


def _attention(qkv_groups, b, s, lags=(1, 4)):
    n_hp = ATTN_WIDTH // LANES
    in_specs, args = [], []
    for gi, d in enumerate(DILATIONS):
        arr = qkv_groups[gi].reshape(b, s, QKV_WIDTH)
        for part in range(3):
            in_specs.append(pl.BlockSpec(
                (None, s, LANES),
                functools.partial(lambda bi, hp, part: (bi, 0, part * n_hp + hp), part=part)))
            args.append(arr)
    masks = _attn_bias()
    return pl.pallas_call(
        functools.partial(_attn_kernel, s=s, lags=lags),
        grid=(b, n_hp),
        in_specs=in_specs + [_const_spec(t.shape) for t in masks],
        out_specs=pl.BlockSpec((None, s, LANES), lambda bi, hp: (bi, 0, hp)),
        out_shape=jax.ShapeDtypeStruct((b, s, ATTN_WIDTH), BF16),
        scratch_shapes=[pltpu.VMEM((N_GROUPS - 1, s, LANES), F32)] * 3,
        compiler_params=pltpu.CompilerParams(
            dimension_semantics=("parallel", "parallel"), vmem_limit_bytes=VMEM_LIMIT),
        name="attention",
    )(*args, *masks)


def _gelu_tanh(x):
    return 0.5 * x * (1.0 + jnp.tanh(math.sqrt(2.0 / math.pi) * (x + 0.044715 * (x * x * x))))


def _ssm_kernel(u_ref, bm_ref, are_ref, aim_ref, cm_ref, dskip_ref, o_ref,
                st_a, st_b, ub_s, y_s, h_ref, *, tc, bsz):
    rows = tc * bsz
    half = SSM_PAIRS // 2
    pairs_per_slab = LANES // (2 * SSM_GROUP)
    n_slabs = SSM_WIDTH // LANES
    splits = 2
    n_slices = half * splits
    steps = tc // n_slices
    srows = rows // splits

    @pl.when(pl.program_id(0) == 0)
    def _():
        h_ref[...] = jnp.zeros_like(h_ref)

    for c in range(n_slabs):
        cols = slice(c * LANES, (c + 1) * LANES)
        for b in range(bsz):
            y_s[c, pl.ds(b, tc, stride=bsz), :] = u_ref[b, :, cols]
        u = y_s[c]
        ub_s[c] = u.astype(BF16)
        y_s[c] = dskip_ref[:, cols] * u

    def project_in(buf, j, jl, rs):
        bu = jnp.dot(ub_s[j // pairs_per_slab, rs, :], bm_ref[j], preferred_element_type=F32)
        buf[0, jl, rs, :] = bu[:, :LANES]
        buf[1, jl, rs, :] = bu[:, LANES:]

    def project_out(buf, j, jl, rs):
        st = jnp.concatenate([buf[0, jl, rs, :].astype(BF16), buf[1, jl, rs, :].astype(BF16)],
                             axis=1)
        y_s[j // pairs_per_slab, rs, :] += jnp.dot(st, cm_ref[j], preferred_element_type=F32)

    def scan(buf, j0, other):
        a_re = [are_ref[j0 + i] for i in range(half)]
        a_im = [aim_ref[j0 + i] for i in range(half)]

        def body(it, carry):
            h_re, h_im = carry
            for k in range(steps):
                r0 = pl.multiple_of((it * steps + k) * bsz, bsz)
                new_re, new_im = [], []
                for i in range(half):
                    n_re = a_re[i] * h_re[i] - a_im[i] * h_im[i] + buf[0, i, pl.ds(r0, bsz), :]
                    n_im = a_re[i] * h_im[i] + a_im[i] * h_re[i] + buf[1, i, pl.ds(r0, bsz), :]
                    buf[0, i, pl.ds(r0, bsz), :] = n_re
                    buf[1, i, pl.ds(r0, bsz), :] = n_im
                    new_re.append(n_re)
                    new_im.append(n_im)
                h_re, h_im = tuple(new_re), tuple(new_im)
            other(it // splits, pl.ds((it % splits) * srows, srows))
            return h_re, h_im

        carry = (tuple(h_ref[0, j0 + i] for i in range(half)),
                 tuple(h_ref[1, j0 + i] for i in range(half)))
        for it in range(n_slices):
            carry = body(it, carry)
        h_re, h_im = carry
        for i in range(half):
            h_ref[0, j0 + i] = h_re[i]
            h_ref[1, j0 + i] = h_im[i]

    for jl in range(half):
        project_in(st_a, jl, jl, slice(None))
    scan(st_a, 0, lambda jl, rs: project_in(st_b, half + jl, jl, rs))
    scan(st_b, half, lambda jl, rs: project_out(st_a, jl, jl, rs))
    for jl in range(half):
        project_out(st_b, half + jl, jl, slice(None))

    for c in range(n_slabs):
        for b in range(bsz):
            y = y_s[c, pl.ds(b, tc, stride=bsz), :]
            o_ref[b, :, c * LANES:(c + 1) * LANES] = _gelu_tanh(y).astype(o_ref.dtype)


def _ssm(u, bm, a_re, a_im, cm, d_skip, tc=128):
    bsz, s, _ = u.shape
    rows = tc * bsz
    half = SSM_PAIRS // 2
    chunk_spec = pl.BlockSpec((bsz, tc, SSM_WIDTH), lambda i: (0, i, 0))
    return pl.pallas_call(
        functools.partial(_ssm_kernel, tc=tc, bsz=bsz),
        grid=(s // tc,),
        in_specs=[chunk_spec,
                  _const_spec(bm.shape), _const_spec(a_re.shape), _const_spec(a_im.shape),
                  _const_spec(cm.shape), _const_spec(d_skip.shape)],
        out_specs=chunk_spec,
        out_shape=jax.ShapeDtypeStruct((bsz, s, SSM_WIDTH), BF16),
        scratch_shapes=[pltpu.VMEM((2, half, rows, LANES), F32),
                        pltpu.VMEM((2, half, rows, LANES), F32),
                        pltpu.VMEM((SSM_WIDTH // LANES, rows, LANES), BF16),
                        pltpu.VMEM((SSM_WIDTH // LANES, rows, LANES), F32),
                        pltpu.VMEM((2, SSM_PAIRS, bsz, LANES), F32)],
        compiler_params=pltpu.CompilerParams(
            dimension_semantics=("arbitrary",), vmem_limit_bytes=VMEM_LIMIT),
        name="ssm",
    )(u, bm, a_re, a_im, cm, d_skip)


def _ssm_matrices(a_re, a_im, log_dt, b_re, b_im, c_re, c_im, bsz):
    g, p, h = SSM_GROUPS, SSM_STATE, SSM_GROUP
    lam = lax.complex(a_re.astype(F32), a_im.astype(F32))
    dt = jnp.exp(log_dt.astype(F32))[:, None]
    a_bar = jnp.exp(lam * dt)
    b_bar = ((a_bar - 1.0) / lam)[..., None] * lax.complex(b_re.astype(F32), b_im.astype(F32))
    groups_per_slab = LANES // h
    pairs_per_slab = groups_per_slab // 2
    eye = jnp.eye(groups_per_slab, dtype=F32)

    def in_matrix(bpart):
        bt = jnp.transpose(bpart, (0, 2, 1)).reshape(g // groups_per_slab, groups_per_slab, h, p)
        dense = bt[:, :, :, None, :] * eye[None, :, None, :, None]
        dense = dense.reshape(g // groups_per_slab, LANES, pairs_per_slab, 2 * p)
        return jnp.transpose(dense, (0, 2, 1, 3)).reshape(SSM_PAIRS, LANES, 2 * p)

    bm = jnp.concatenate([in_matrix(b_bar.real), in_matrix(b_bar.imag)], axis=2)

    def out_matrix(cpart):
        ct = jnp.transpose(cpart, (0, 2, 1)).reshape(g // groups_per_slab, groups_per_slab, p, h)
        dense = ct[:, :, :, None, :] * eye[None, :, None, :, None]
        dense = dense.reshape(g // groups_per_slab, pairs_per_slab, 2 * p, LANES)
        return dense.reshape(SSM_PAIRS, 2 * p, LANES)

    cm = jnp.concatenate([out_matrix(c_re.astype(F32)), out_matrix(-c_im.astype(F32))], axis=1)

    def bcast(apart):
        return jnp.broadcast_to(apart.reshape(SSM_PAIRS, 1, 2 * p), (SSM_PAIRS, bsz, 2 * p))

    return bm.astype(BF16), bcast(a_bar.real), bcast(a_bar.imag), cm.astype(BF16)


def _layer_norm(z, g, b):
    mu = jnp.mean(z, axis=-1, keepdims=True)
    zc = z - mu
    var = jnp.mean(zc * zc, axis=-1, keepdims=True)
    return zc * lax.rsqrt(var + LN_EPS) * g + b


def _mix_kernel(x_ref, attn_ref, g_ref, wga_ref, wgs_ref, bgate_ref, wattn_ref, wglu_ref,
                wssm_ref, wout_ref, lng_ref, lnb_ref, o_ref, *, alpha, sub):
    dm = x_ref.shape[1]
    for r0 in range(0, x_ref.shape[0], sub):
        rows = slice(r0, r0 + sub)
        x = x_ref[rows, :]
        xb = x.astype(BF16)
        gate_attn = jax.nn.sigmoid(
            jnp.dot(xb, wga_ref[...], preferred_element_type=F32) + bgate_ref[:, :dm])
        gate_ssm = jax.nn.sigmoid(
            jnp.dot(xb, wgs_ref[...], preferred_element_type=F32) + bgate_ref[:, dm:])
        y_attn = jnp.dot(attn_ref[rows, :], wattn_ref[...], preferred_element_type=F32)
        glu = jnp.dot(g_ref[rows, :], wglu_ref[...], preferred_element_type=F32)
        y_s = glu[:, :SSM_WIDTH] * jax.nn.sigmoid(glu[:, SSM_WIDTH:])
        y_ssm = jnp.dot(y_s.astype(BF16), wssm_ref[...], preferred_element_type=F32)
        mixed = gate_attn * y_attn + gate_ssm * y_ssm
        mix_out = jnp.dot(mixed.astype(BF16), wout_ref[...], preferred_element_type=F32)
        o_ref[rows, :] = _layer_norm(alpha * x + mix_out, lng_ref[...], lnb_ref[...])


def _mix(x2, attn2, g2, w_all, bgate, wattn, wglu, wssm, wout, lng, lnb, alpha, tm=1024, sub=512):
    rows, dm = x2.shape
    row_spec = lambda w: pl.BlockSpec((tm, w), lambda i: (i, 0))
    n_col_blocks = w_all.shape[1] // dm
    gate_specs = [pl.BlockSpec((dm, dm), functools.partial(lambda i, c: (0, c), c=c),
                               pipeline_mode=pl.Buffered(1))
                  for c in (n_col_blocks - 2, n_col_blocks - 1)]
    consts = [bgate, wattn, wglu, wssm, wout, lng, lnb]
    return pl.pallas_call(
        functools.partial(_mix_kernel, alpha=alpha, sub=sub),
        grid=(rows // tm,),
        in_specs=[row_spec(dm), row_spec(ATTN_WIDTH), row_spec(SSM_WIDTH)] + gate_specs
                 + [_const_spec(c.shape) for c in consts],
        out_specs=row_spec(dm),
        out_shape=jax.ShapeDtypeStruct((rows, dm), F32),
        compiler_params=pltpu.CompilerParams(
            dimension_semantics=("parallel",), vmem_limit_bytes=VMEM_LIMIT),
        name="mix",
    )(x2, attn2, g2, w_all, w_all, *consts)


def _load_as_bf16(src_hbm, dst, stage, sem):
    chunk = stage.shape[1]
    n_chunks = src_hbm.shape[0] // chunk

    def copy(k):
        return pltpu.make_async_copy(src_hbm.at[pl.ds(k * chunk, chunk), :],
                                     stage.at[k % 2], sem.at[k % 2])

    copy(0).start()
    for k in range(n_chunks):
        if k + 1 < n_chunks:
            copy(k + 1).start()
        copy(k).wait()
        dst[k * chunk:(k + 1) * chunk, :] = stage[k % 2].astype(dst.dtype)


def _ffn_kernel(h_ref, wg_hbm, wu_hbm, wd_hbm, lng_ref, lnb_ref, o_ref,
                wg_ref, wu_ref, wd_ref, stage_in, stage_out, sem, *, alpha, sub):
    @pl.when(pl.program_id(0) == 0)
    def _():
        _load_as_bf16(wg_hbm, wg_ref, stage_in, sem)
        _load_as_bf16(wu_hbm, wu_ref, stage_in, sem)
        _load_as_bf16(wd_hbm, wd_ref, stage_out, sem)

    for r0 in range(0, h_ref.shape[0], sub):
        rows = slice(r0, r0 + sub)
        h = h_ref[rows, :]
        hb = h.astype(BF16)
        gate = jnp.dot(hb, wg_ref[...], preferred_element_type=F32)
        up = jnp.dot(hb, wu_ref[...], preferred_element_type=F32)
        act = (gate * jax.nn.sigmoid(gate)) * up
        ff = jnp.dot(act.astype(BF16), wd_ref[...], preferred_element_type=F32)
        o_ref[rows, :] = _layer_norm(alpha * h + ff, lng_ref[...], lnb_ref[...])


def _ffn(h2, wg, wu, wd, lng, lnb, alpha, tm=1024, sub=512):
    rows, dm = h2.shape
    d_ff = wg.shape[1]
    n_stage = 8
    row_spec = pl.BlockSpec((tm, dm), lambda i: (i, 0))
    hbm_spec = pl.BlockSpec(memory_space=pl.ANY)
    return pl.pallas_call(
        functools.partial(_ffn_kernel, alpha=alpha, sub=sub),
        grid=(rows // tm,),
        in_specs=[row_spec, hbm_spec, hbm_spec, hbm_spec,
                  _const_spec(lng.shape), _const_spec(lnb.shape)],
        out_specs=row_spec,
        out_shape=jax.ShapeDtypeStruct((rows, dm), F32),
        scratch_shapes=[pltpu.VMEM((dm, d_ff), BF16), pltpu.VMEM((dm, d_ff), BF16),
                        pltpu.VMEM((d_ff, dm), BF16),
                        pltpu.VMEM((2, dm // n_stage, d_ff), F32),
                        pltpu.VMEM((2, d_ff // n_stage, dm), F32),
                        pltpu.SemaphoreType.DMA((2,))],
        compiler_params=pltpu.CompilerParams(
            dimension_semantics=("arbitrary",), vmem_limit_bytes=VMEM_LIMIT),
        name="ffn",
    )(h2, wg, wu, wd, lng, lnb)


def _rope_tables(s):
    half = HEAD_DIM // 2
    pos = jnp.arange(s, dtype=F32)
    inv_freq = ROPE_THETA ** (-jnp.arange(half, dtype=F32) / half)
    ang = pos[:, None] * inv_freq[None, :]
    cos, sin = jnp.cos(ang), jnp.sin(ang)
    reps = LANES // HEAD_DIM
    cos_t = jnp.concatenate([cos, cos] * reps, axis=1)
    sin_t = jnp.concatenate([-sin, sin] * reps, axis=1)
    return cos_t, sin_t


def kernel(x, w_in, b_gate, w_attn_br, w_ssm_br, w_out, ssm_a_re, ssm_a_im, ssm_log_dt, ssm_b_re, ssm_b_im, ssm_c_re, ssm_c_im, ssm_d, w_glu, ln1_g, ln1_b, w_ff_gate, w_ff_up, w_ff_down, ln2_g, ln2_b):
    bsz, s, dm = x.shape
    depth = w_in.shape[0]
    alpha = (2.0 * depth) ** 0.25
    assert all(w // d == ATTN_BLOCK for w, d in zip(WINDOWS, DILATIONS))
    assert s % (ATTN_BLOCK * max(DILATIONS)) == 0
    cos_t, sin_t = _rope_tables(s)
    assert w_in.shape[2] == 3 * QKV_WIDTH + SSM_WIDTH + 2 * dm

    for layer in range(depth):
        w_all = w_in[layer].astype(BF16)

        outs = _project(x, cos_t, sin_t, w_all)
        attn = _attention(outs[:N_GROUPS], bsz, s)

        bm, a_re, a_im, cm = _ssm_matrices(
            ssm_a_re[layer], ssm_a_im[layer], ssm_log_dt[layer], ssm_b_re[layer],
            ssm_b_im[layer], ssm_c_re[layer], ssm_c_im[layer], bsz)
        g = _ssm(outs[N_GROUPS], bm, a_re, a_im, cm,
                 ssm_d[layer].reshape(1, SSM_WIDTH).astype(F32))

        h = _mix(x.reshape(bsz * s, dm), attn.reshape(bsz * s, ATTN_WIDTH),
                 g.reshape(bsz * s, SSM_WIDTH), w_all,
                 b_gate[layer].reshape(1, 2 * dm).astype(F32),
                 w_attn_br[layer].astype(BF16), w_glu[layer].astype(BF16),
                 w_ssm_br[layer].astype(BF16), w_out[layer].astype(BF16),
                 ln1_g[layer].reshape(1, dm).astype(F32), ln1_b[layer].reshape(1, dm).astype(F32),
                 alpha)
        x2 = _ffn(h, w_ff_gate[layer], w_ff_up[layer], w_ff_down[layer],
                  ln2_g[layer].reshape(1, dm).astype(F32), ln2_b[layer].reshape(1, dm).astype(F32),
                  alpha)
        x = x2.reshape(bsz, s, dm)
    return x
```

```python
import functools
import math

import jax
import jax.numpy as jnp
import numpy as np
from jax import lax
from jax.experimental import pallas as pl
from jax.experimental.pallas import tpu as pltpu

F32 = jnp.float32
BF16 = jnp.bfloat16

LANES = 128
HEAD_DIM = 64
ATTN_HEADS = 8
ATTN_WIDTH = ATTN_HEADS * HEAD_DIM
DILATIONS = (1, 4, 16)
WINDOWS = (128, 512, 2048)
N_GROUPS = len(DILATIONS)
QKV_WIDTH = N_GROUPS * ATTN_WIDTH
ATTN_BLOCK = 128
ROPE_THETA = 10000.0
NEG_INF = -1e30
Q_SCALE = math.log2(math.e) / math.sqrt(HEAD_DIM)
SSM_GROUP = 16
SSM_GROUPS = 32
SSM_WIDTH = SSM_GROUP * SSM_GROUPS
SSM_STATE = 64
SSM_PAIRS = SSM_GROUPS // 2
LN_EPS = 1e-5

VMEM_LIMIT = 56 * 1024 * 1024


def _const_spec(shape):
    zeros = (0,) * len(shape)
    return pl.BlockSpec(shape, lambda *_: zeros, pipeline_mode=pl.Buffered(1))


def _cast_slice_specs(side, n_steps, step_of):
    in_specs, out_specs, out_shapes = [], [], []
    for arr, col in side:
        rows = arr.shape[0] // n_steps
        assert rows * n_steps == arr.shape[0] and rows % 16 == 0
        cidx, width = col if col is not None else (0, arr.shape[1])
        in_specs.append(pl.BlockSpec(
            (rows, width), functools.partial(lambda *ids, cidx: (step_of(*ids), cidx), cidx=cidx)))
        out_specs.append(pl.BlockSpec((rows, width), lambda *ids: (step_of(*ids), 0)))
        out_shapes.append(jax.ShapeDtypeStruct((arr.shape[0], width), BF16))
    return in_specs, out_specs, out_shapes


def _cast_slices(src_refs, dst_refs):
    for src, dst in zip(src_refs, dst_refs):
        dst[...] = src[...].astype(dst.dtype)


def _phase_major(ref, tm, d):
    if d == 1:
        return ref[...]
    n = tm // d
    return jnp.concatenate([ref[pl.ds(r, n, stride=d), :] for r in range(d)], axis=0)


def _proj_kernel(*refs, tm, n_slabs, n_side):
    refs = list(refs)
    x_refs = [refs.pop(0) for _ in range(n_slabs)]
    cos_ref, sin_ref, w32_ref = [refs.pop(0) for _ in range(3)]
    side_in = [refs.pop(0) for _ in range(n_side)]
    o_refs = [refs.pop(0) for _ in range(N_GROUPS)]
    u_ref = refs.pop(0)
    side_out = [refs.pop(0) for _ in range(n_side)]
    w_ref, = refs

    @pl.when((pl.program_id(0) == 0) & (pl.program_id(1) == 0))
    def _():
        for c0 in range(0, w_ref.shape[1], ATTN_WIDTH):
            w_ref[:, c0:c0 + ATTN_WIDTH] = w32_ref[:, c0:c0 + ATTN_WIDTH].astype(BF16)

    _cast_slices(side_in, side_out)

    lane = lax.broadcasted_iota(jnp.int32, (1, LANES), 1)
    first_half = (lane % HEAD_DIM) < (HEAD_DIM // 2)

    for gi, d in enumerate(DILATIONS):
        n = tm // d
        xg = jnp.concatenate(
            [_phase_major(xr, tm, d).astype(BF16) for xr in x_refs], axis=1)
        cos = _phase_major(cos_ref, tm, d)
        sin = _phase_major(sin_ref, tm, d)
        if gi == 0:
            u_ref[...] = jnp.dot(xg, w_ref[:, 3 * QKV_WIDTH:], preferred_element_type=F32)
        for part in range(3):
            c0 = part * ATTN_WIDTH
            w0 = part * QKV_WIDTH + gi * ATTN_WIDTH
            acc = jnp.dot(xg, w_ref[:, w0:w0 + ATTN_WIDTH], preferred_element_type=F32)
            for c in range(ATTN_WIDTH // LANES):
                t = acc[:, c * LANES:(c + 1) * LANES]
                if part < 2:
                    rot = jnp.where(first_half,
                                    pltpu.roll(t, LANES - HEAD_DIM // 2, 1),
                                    pltpu.roll(t, HEAD_DIM // 2, 1))
                    t = t * cos + rot * sin
                if part == 0:
                    t = t * Q_SCALE
                t = t.astype(BF16)
                col = c0 + c * LANES
                if d == 1:
                    o_refs[gi][:, col:col + LANES] = t
                else:
                    for r in range(d):
                        o_refs[gi][r, :, col:col + LANES] = t[r * n:(r + 1) * n]


def _project(x, cos_t, sin_t, w_in, side, tm=512):
    b, s, dm = x.shape
    n_slabs = dm // LANES
    n_proj = 3 * QKV_WIDTH + SSM_WIDTH
    grid = (b, s // tm)
    side_in, side_out, side_shapes = _cast_slice_specs(
        side, b * (s // tm), lambda bi, i: bi * (s // tm) + i)
    x_specs = [pl.BlockSpec((None, tm, LANES), functools.partial(lambda bi, i, c: (bi, i, c), c=c))
               for c in range(n_slabs)]
    tab_spec = pl.BlockSpec((tm, LANES), lambda bi, i: (i, 0))
    out_shapes = []
    out_specs = []
    for d in DILATIONS:
        if d == 1:
            out_shapes.append(jax.ShapeDtypeStruct((b, s, QKV_WIDTH), BF16))
            out_specs.append(pl.BlockSpec((None, tm, QKV_WIDTH), lambda bi, i: (bi, i, 0)))
        else:
            out_shapes.append(jax.ShapeDtypeStruct((b, d, s // d, QKV_WIDTH), BF16))
            out_specs.append(pl.BlockSpec((None, d, tm // d, QKV_WIDTH),
                                          lambda bi, i: (bi, 0, i, 0)))
    out_shapes.append(jax.ShapeDtypeStruct((b, s, SSM_WIDTH), F32))
    out_specs.append(pl.BlockSpec((None, tm, SSM_WIDTH), lambda bi, i: (bi, i, 0)))
    outs = pl.pallas_call(
        functools.partial(_proj_kernel, tm=tm, n_slabs=n_slabs, n_side=len(side)),
        grid=grid,
        in_specs=x_specs + [tab_spec, tab_spec,
                            pl.BlockSpec((dm, n_proj), lambda bi, i: (0, 0),
                                         pipeline_mode=pl.Buffered(1))] + side_in,
        out_specs=out_specs + side_out,
        out_shape=out_shapes + side_shapes,
        scratch_shapes=[pltpu.VMEM((dm, n_proj), BF16)],
        compiler_params=pltpu.CompilerParams(
            dimension_semantics=("arbitrary", "arbitrary"), vmem_limit_bytes=VMEM_LIMIT),
        name="in_proj",
    )(*([x] * n_slabs), cos_t, sin_t, w_in, *[arr for arr, _ in side])
    n_main = N_GROUPS + 1
    return outs[:n_main], outs[n_main:]


def _attn_bias():
    blk = ATTN_BLOCK
    a = (np.arange(2 * blk) % blk)[:, None]
    c = np.arange(2 * blk)[None, :]
    in_window = np.where(c < blk, c >= a, c - blk <= a)
    windowed = np.where(in_window, 0.0, NEG_INF)
    causal = np.where(c[:, :blk] <= a, 0.0, NEG_INF)
    return windowed.astype(np.float32), causal.astype(np.float32)


def _attn_kernel(*refs, s, lags):
    qkv = refs[:3 * N_GROUPS]
    mask_w_ref, mask_c_ref = refs[3 * N_GROUPS:3 * N_GROUPS + 2]
    o_ref = refs[3 * N_GROUPS + 2]
    acc_s, m_s, l_s = refs[3 * N_GROUPS + 3:]
    blk = ATTN_BLOCK
    n_blocks = s // blk

    lane = lax.broadcasted_iota(jnp.int32, (1, LANES), 1)
    head_a = lane < HEAD_DIM

    def window(ref, i, nblk):
        start = i if i % nblk == 0 else i - 1
        return ref[start * blk:(i + 1) * blk, :]

    def scores(gi, i, nblk):
        q_ref, k_ref, _ = qkv[3 * gi:3 * gi + 3]
        q = q_ref[i * blk:(i + 1) * blk, :]
        zero = jnp.zeros_like(q)
        q2 = jnp.concatenate([jnp.where(head_a, q, zero), jnp.where(head_a, zero, q)], axis=0)
        mask = mask_c_ref[...] if i % nblk == 0 else mask_w_ref[...]
        return lax.dot_general(q2, window(k_ref, i, nblk), (((1,), (1,)), ((), ())),
                               preferred_element_type=F32) + mask

    def softmax(sc):
        m = jnp.max(sc, axis=1, keepdims=True)
        return m, jnp.exp2((sc - m).astype(BF16))

    def finish(gi, d, i, nblk, m, p):
        vw = window(qkv[3 * gi + 2], i, nblk)
        v_ext = jnp.concatenate([vw, jnp.ones_like(vw)], axis=1)
        res = jnp.dot(p, v_ext, preferred_element_type=F32)
        acc = jnp.where(head_a, res[:blk, :LANES], res[blk:, :LANES])
        l = jnp.where(head_a, res[:blk, LANES:], res[blk:, LANES:])
        m = jnp.where(head_a, m[:blk], m[blk:])
        if d > 1:
            r, m0 = divmod(i * blk, s // d)
            rows = pl.ds(m0 * d + r, blk, stride=d)
            acc_s[gi - 1, rows, :] = acc
            l_s[gi - 1, rows, :] = l
            m_s[gi - 1, rows, :] = m
            return
        rows = slice(i * blk, (i + 1) * blk)
        parts = [(m, l, acc)] + [(m_s[g, rows, :], l_s[g, rows, :], acc_s[g, rows, :])
                                 for g in range(N_GROUPS - 1)]
        m_all = functools.reduce(jnp.maximum, [pm for pm, _, _ in parts])
        num = jnp.zeros((blk, LANES), F32)
        den = jnp.zeros((blk, LANES), F32)
        for pm, pl_, pacc in parts:
            wgt = jnp.exp2(pm - m_all)
            num = num + wgt * pacc
            den = den + wgt * pl_
        o_ref[rows, :] = (num / den).astype(o_ref.dtype)

    assert DILATIONS[0] == 1
    work = [(gi, DILATIONS[gi], i, (s // DILATIONS[gi]) // blk)
            for gi in reversed(range(N_GROUPS)) for i in range(n_blocks)]
    lag_softmax, lag_values = lags
    sc_q, p_q = {}, {}
    for step in range(len(work) + lag_values):
        if step < len(work):
            gi, d, i, nblk = work[step]
            sc_q[step] = scores(gi, i, nblk)
        if 0 <= step - lag_softmax < len(work):
            p_q[step - lag_softmax] = softmax(sc_q.pop(step - lag_softmax))
        if step >= lag_values:
            gi, d, i, nblk = work[step - lag_values]
            finish(gi, d, i, nblk, *p_q.pop(step - lag_values))


def _attention(qkv_groups, b, s, lags=(1, 4)):
    n_hp = ATTN_WIDTH // LANES
    in_specs, args = [], []
    for gi, d in enumerate(DILATIONS):
        arr = qkv_groups[gi].reshape(b, s, QKV_WIDTH)
        for part in range(3):
            in_specs.append(pl.BlockSpec(
                (None, s, LANES),
                functools.partial(lambda bi, hp, part: (bi, 0, part * n_hp + hp), part=part)))
            args.append(arr)
    masks = _attn_bias()
    return pl.pallas_call(
        functools.partial(_attn_kernel, s=s, lags=lags),
        grid=(b, n_hp),
        in_specs=in_specs + [_const_spec(t.shape) for t in masks],
        out_specs=pl.BlockSpec((None, s, LANES), lambda bi, hp: (bi, 0, hp)),
        out_shape=jax.ShapeDtypeStruct((b, s, ATTN_WIDTH), BF16),
        scratch_shapes=[pltpu.VMEM((N_GROUPS - 1, s, LANES), F32)] * 3,
        compiler_params=pltpu.CompilerParams(
            dimension_semantics=("parallel", "parallel"), vmem_limit_bytes=VMEM_LIMIT),
        name="attention",
    )(*args, *masks)


def _gelu_tanh(x):
    return 0.5 * x * (1.0 + jnp.tanh(math.sqrt(2.0 / math.pi) * (x + 0.044715 * (x * x * x))))


def _ssm_kernel(u_ref, bm_ref, are_ref, aim_ref, cm_ref, dskip_ref, o_ref,
                st_a, st_b, ub_s, y_s, h_ref, *, tc, bsz):
    rows = tc * bsz
    half = SSM_PAIRS // 2
    pairs_per_slab = LANES // (2 * SSM_GROUP)
    n_slabs = SSM_WIDTH // LANES
    splits = 2
    n_slices = half * splits
    steps = tc // n_slices
    srows = rows // splits

    @pl.when(pl.program_id(0) == 0)
    def _():
        h_ref[...] = jnp.zeros_like(h_ref)

    for c in range(n_slabs):
        cols = slice(c * LANES, (c + 1) * LANES)
        for b in range(bsz):
            y_s[c, pl.ds(b, tc, stride=bsz), :] = u_ref[b, :, cols]
        u = y_s[c]
        ub_s[c] = u.astype(BF16)
        y_s[c] = dskip_ref[:, cols] * u

    def project_in(buf, j, jl, rs):
        bu = jnp.dot(ub_s[j // pairs_per_slab, rs, :], bm_ref[j], preferred_element_type=F32)
        buf[0, jl, rs, :] = bu[:, :LANES]
        buf[1, jl, rs, :] = bu[:, LANES:]

    def project_out(buf, j, jl, rs):
        st = jnp.concatenate([buf[0, jl, rs, :].astype(BF16), buf[1, jl, rs, :].astype(BF16)],
                             axis=1)
        y_s[j // pairs_per_slab, rs, :] += jnp.dot(st, cm_ref[j], preferred_element_type=F32)

    def scan(buf, j0, other):
        a_re = [are_ref[j0 + i] for i in range(half)]
        a_im = [aim_ref[j0 + i] for i in range(half)]

        def body(it, carry):
            h_re, h_im = carry
            for k in range(steps):
                r0 = pl.multiple_of((it * steps + k) * bsz, bsz)
                new_re, new_im = [], []
                for i in range(half):
                    n_re = a_re[i] * h_re[i] - a_im[i] * h_im[i] + buf[0, i, pl.ds(r0, bsz), :]
                    n_im = a_re[i] * h_im[i] + a_im[i] * h_re[i] + buf[1, i, pl.ds(r0, bsz), :]
                    buf[0, i, pl.ds(r0, bsz), :] = n_re
                    buf[1, i, pl.ds(r0, bsz), :] = n_im
                    new_re.append(n_re)
                    new_im.append(n_im)
                h_re, h_im = tuple(new_re), tuple(new_im)
            other(it // splits, pl.ds((it % splits) * srows, srows))
            return h_re, h_im

        carry = (tuple(h_ref[0, j0 + i] for i in range(half)),
                 tuple(h_ref[1, j0 + i] for i in range(half)))
        for it in range(n_slices):
            carry = body(it, carry)
        h_re, h_im = carry
        for i in range(half):
            h_ref[0, j0 + i] = h_re[i]
            h_ref[1, j0 + i] = h_im[i]

    for jl in range(half):
        project_in(st_a, jl, jl, slice(None))
    scan(st_a, 0, lambda jl, rs: project_in(st_b, half + jl, jl, rs))
    scan(st_b, half, lambda jl, rs: project_out(st_a, jl, jl, rs))
    for jl in range(half):
        project_out(st_b, half + jl, jl, slice(None))

    for c in range(n_slabs):
        for b in range(bsz):
            y = y_s[c, pl.ds(b, tc, stride=bsz), :]
            o_ref[b, :, c * LANES:(c + 1) * LANES] = _gelu_tanh(y).astype(o_ref.dtype)


def _ssm(u, bm, a_re, a_im, cm, d_skip, tc=128):
    bsz, s, _ = u.shape
    rows = tc * bsz
    half = SSM_PAIRS // 2
    chunk_spec = pl.BlockSpec((bsz, tc, SSM_WIDTH), lambda i: (0, i, 0))
    return pl.pallas_call(
        functools.partial(_ssm_kernel, tc=tc, bsz=bsz),
        grid=(s // tc,),
        in_specs=[chunk_spec,
                  _const_spec(bm.shape), _const_spec(a_re.shape), _const_spec(a_im.shape),
                  _const_spec(cm.shape), _const_spec(d_skip.shape)],
        out_specs=chunk_spec,
        out_shape=jax.ShapeDtypeStruct((bsz, s, SSM_WIDTH), BF16),
        scratch_shapes=[pltpu.VMEM((2, half, rows, LANES), F32),
                        pltpu.VMEM((2, half, rows, LANES), F32),
                        pltpu.VMEM((SSM_WIDTH // LANES, rows, LANES), BF16),
                        pltpu.VMEM((SSM_WIDTH // LANES, rows, LANES), F32),
                        pltpu.VMEM((2, SSM_PAIRS, bsz, LANES), F32)],
        compiler_params=pltpu.CompilerParams(
            dimension_semantics=("arbitrary",), vmem_limit_bytes=VMEM_LIMIT),
        name="ssm",
    )(u, bm, a_re, a_im, cm, d_skip)


def _ssm_matrices(a_re, a_im, log_dt, b_re, b_im, c_re, c_im, bsz):
    g, p, h = SSM_GROUPS, SSM_STATE, SSM_GROUP
    lam = lax.complex(a_re.astype(F32), a_im.astype(F32))
    dt = jnp.exp(log_dt.astype(F32))[:, None]
    a_bar = jnp.exp(lam * dt)
    b_bar = ((a_bar - 1.0) / lam)[..., None] * lax.complex(b_re.astype(F32), b_im.astype(F32))
    groups_per_slab = LANES // h
    pairs_per_slab = groups_per_slab // 2
    eye = jnp.eye(groups_per_slab, dtype=F32)

    def in_matrix(bpart):
        bt = jnp.transpose(bpart, (0, 2, 1)).reshape(g // groups_per_slab, groups_per_slab, h, p)
        dense = bt[:, :, :, None, :] * eye[None, :, None, :, None]
        dense = dense.reshape(g // groups_per_slab, LANES, pairs_per_slab, 2 * p)
        return jnp.transpose(dense, (0, 2, 1, 3)).reshape(SSM_PAIRS, LANES, 2 * p)

    bm = jnp.concatenate([in_matrix(b_bar.real), in_matrix(b_bar.imag)], axis=2)

    def out_matrix(cpart):
        ct = jnp.transpose(cpart, (0, 2, 1)).reshape(g // groups_per_slab, groups_per_slab, p, h)
        dense = ct[:, :, :, None, :] * eye[None, :, None, :, None]
        dense = dense.reshape(g // groups_per_slab, pairs_per_slab, 2 * p, LANES)
        return dense.reshape(SSM_PAIRS, 2 * p, LANES)

    cm = jnp.concatenate([out_matrix(c_re.astype(F32)), out_matrix(-c_im.astype(F32))], axis=1)

    def bcast(apart):
        return jnp.broadcast_to(apart.reshape(SSM_PAIRS, 1, 2 * p), (SSM_PAIRS, bsz, 2 * p))

    return bm.astype(BF16), bcast(a_bar.real), bcast(a_bar.imag), cm.astype(BF16)


def _layer_norm(z, g, b):
    mu = jnp.mean(z, axis=-1, keepdims=True)
    zc = z - mu
    var = jnp.mean(zc * zc, axis=-1, keepdims=True)
    return zc * lax.rsqrt(var + LN_EPS) * g + b


def _mix_kernel(*refs, alpha, sub, n_side):
    (x_ref, attn_ref, g_ref, wga_ref, wgs_ref, bgate_ref, wattn_ref, wglu_ref,
     wssm_ref, wout_ref, lng_ref, lnb_ref) = refs[:12]
    o_ref = refs[12 + n_side]
    _cast_slices(refs[12:12 + n_side], refs[13 + n_side:])
    dm = x_ref.shape[1]
    for r0 in range(0, x_ref.shape[0], sub):
        rows = slice(r0, r0 + sub)
        x = x_ref[rows, :]
        xb = x.astype(BF16)
        gate_attn = jax.nn.sigmoid(
            jnp.dot(xb, wga_ref[...], preferred_element_type=F32) + bgate_ref[:, :dm])
        gate_ssm = jax.nn.sigmoid(
            jnp.dot(xb, wgs_ref[...], preferred_element_type=F32) + bgate_ref[:, dm:])
        y_attn = jnp.dot(attn_ref[rows, :], wattn_ref[...], preferred_element_type=F32)
        glu = jnp.dot(g_ref[rows, :], wglu_ref[...], preferred_element_type=F32)
        y_s = glu[:, :SSM_WIDTH] * jax.nn.sigmoid(glu[:, SSM_WIDTH:])
        y_ssm = jnp.dot(y_s.astype(BF16), wssm_ref[...], preferred_element_type=F32)
        mixed = gate_attn * y_attn + gate_ssm * y_ssm
        mix_out = jnp.dot(mixed.astype(BF16), wout_ref[...], preferred_element_type=F32)
        o_ref[rows, :] = _layer_norm(alpha * x + mix_out, lng_ref[...], lnb_ref[...])


def _mix(x2, attn2, g2, consts, side, alpha, tm=1024, sub=512):
    rows, dm = x2.shape
    row_spec = lambda w: pl.BlockSpec((tm, w), lambda i: (i, 0))
    side_in, side_out, side_shapes = _cast_slice_specs(side, rows // tm, lambda i: i)
    outs = pl.pallas_call(
        functools.partial(_mix_kernel, alpha=alpha, sub=sub, n_side=len(side)),
        grid=(rows // tm,),
        in_specs=[row_spec(dm), row_spec(ATTN_WIDTH), row_spec(SSM_WIDTH)]
                 + [_const_spec(c.shape) for c in consts] + side_in,
        out_specs=[row_spec(dm)] + side_out,
        out_shape=[jax.ShapeDtypeStruct((rows, dm), F32)] + side_shapes,
        compiler_params=pltpu.CompilerParams(
            dimension_semantics=("parallel",), vmem_limit_bytes=VMEM_LIMIT),
        name="mix",
    )(x2, attn2, g2, *consts, *[arr for arr, _ in side])
    return outs[0], outs[1:]


def _ffn_kernel(h_ref, wg_ref, wu_ref, wd_ref, lng_ref, lnb_ref, o_ref, *, alpha, sub):
    for r0 in range(0, h_ref.shape[0], sub):
        rows = slice(r0, r0 + sub)
        h = h_ref[rows, :]
        hb = h.astype(BF16)
        gate = jnp.dot(hb, wg_ref[...], preferred_element_type=F32)
        up = jnp.dot(hb, wu_ref[...], preferred_element_type=F32)
        act = (gate * jax.nn.sigmoid(gate)) * up
        ff = jnp.dot(act.astype(BF16), wd_ref[...], preferred_element_type=F32)
        o_ref[rows, :] = _layer_norm(alpha * h + ff, lng_ref[...], lnb_ref[...])


def _ffn(h2, wg, wu, wd, lng, lnb, alpha, tm=1024, sub=512):
    rows, dm = h2.shape
    row_spec = pl.BlockSpec((tm, dm), lambda i: (i, 0))
    consts = [wg, wu, wd, lng, lnb]
    return pl.pallas_call(
        functools.partial(_ffn_kernel, alpha=alpha, sub=sub),
        grid=(rows // tm,),
        in_specs=[row_spec] + [_const_spec(c.shape) for c in consts],
        out_specs=row_spec,
        out_shape=jax.ShapeDtypeStruct((rows, dm), F32),
        compiler_params=pltpu.CompilerParams(
            dimension_semantics=("parallel",), vmem_limit_bytes=VMEM_LIMIT),
        name="ffn",
    )(h2, *consts)


def _rope_tables(s):
    half = HEAD_DIM // 2
    pos = jnp.arange(s, dtype=F32)
    inv_freq = ROPE_THETA ** (-jnp.arange(half, dtype=F32) / half)
    ang = pos[:, None] * inv_freq[None, :]
    cos, sin = jnp.cos(ang), jnp.sin(ang)
    reps = LANES // HEAD_DIM
    cos_t = jnp.concatenate([cos, cos] * reps, axis=1)
    sin_t = jnp.concatenate([-sin, sin] * reps, axis=1)
    return cos_t, sin_t


def kernel(x, w_in, b_gate, w_attn_br, w_ssm_br, w_out, ssm_a_re, ssm_a_im, ssm_log_dt, ssm_b_re, ssm_b_im, ssm_c_re, ssm_c_im, ssm_d, w_glu, ln1_g, ln1_b, w_ff_gate, w_ff_up, w_ff_down, ln2_g, ln2_b):
    bsz, s, dm = x.shape
    depth = w_in.shape[0]
    alpha = (2.0 * depth) ** 0.25
    assert all(w // d == ATTN_BLOCK for w, d in zip(WINDOWS, DILATIONS))
    assert s % (ATTN_BLOCK * max(DILATIONS)) == 0
    cos_t, sin_t = _rope_tables(s)
    assert w_in.shape[2] == 3 * QKV_WIDTH + SSM_WIDTH + 2 * dm

    gate_col0 = (3 * QKV_WIDTH + SSM_WIDTH) // dm

    for layer in range(depth):
        outs, (wga, wgs, wattn, wglu, wssm, wout) = _project(
            x, cos_t, sin_t, w_in[layer],
            [(w_in[layer], (gate_col0, dm)), (w_in[layer], (gate_col0 + 1, dm)),
             (w_attn_br[layer], None), (w_glu[layer], None), (w_ssm_br[layer], None),
             (w_out[layer], None)])
        attn = _attention(outs[:N_GROUPS], bsz, s)

        bm, a_re, a_im, cm = _ssm_matrices(
            ssm_a_re[layer], ssm_a_im[layer], ssm_log_dt[layer], ssm_b_re[layer],
            ssm_b_im[layer], ssm_c_re[layer], ssm_c_im[layer], bsz)
        g = _ssm(outs[N_GROUPS], bm, a_re, a_im, cm,
                 ssm_d[layer].reshape(1, SSM_WIDTH).astype(F32))

        h, (wff_gate, wff_up, wff_down) = _mix(
            x.reshape(bsz * s, dm), attn.reshape(bsz * s, ATTN_WIDTH),
            g.reshape(bsz * s, SSM_WIDTH),
            [wga, wgs, b_gate[layer].reshape(1, 2 * dm).astype(F32), wattn, wglu, wssm, wout,
             ln1_g[layer].reshape(1, dm).astype(F32), ln1_b[layer].reshape(1, dm).astype(F32)],
            [(w_ff_gate[layer], None), (w_ff_up[layer], None), (w_ff_down[layer], None)],
            alpha)
        x2 = _ffn(h, wff_gate, wff_up, wff_down,
                  ln2_g[layer].reshape(1, dm).astype(F32), ln2_b[layer].reshape(1, dm).astype(F32),
                  alpha)
        x = x2.reshape(bsz, s, dm)
    return x
```

```python
import functools
import math

import jax
import jax.numpy as jnp
import numpy as np
from jax import lax
from jax.experimental import pallas as pl
from jax.experimental.pallas import tpu as pltpu

F32 = jnp.float32
BF16 = jnp.bfloat16

LANES = 128
HEAD_DIM = 64
ATTN_HEADS = 8
ATTN_WIDTH = ATTN_HEADS * HEAD_DIM
DILATIONS = (1, 4, 16)
WINDOWS = (128, 512, 2048)
N_GROUPS = len(DILATIONS)
QKV_WIDTH = N_GROUPS * ATTN_WIDTH
ATTN_BLOCK = 128
ROPE_THETA = 10000.0
NEG_INF = -1e30
Q_SCALE = math.log2(math.e) / math.sqrt(HEAD_DIM)
SSM_GROUP = 16
SSM_GROUPS = 32
SSM_WIDTH = SSM_GROUP * SSM_GROUPS
SSM_STATE = 64
SSM_PAIRS = SSM_GROUPS // 2
LN_EPS = 1e-5

VMEM_LIMIT = 56 * 1024 * 1024


def _const_spec(shape):
    zeros = (0,) * len(shape)
    return pl.BlockSpec(shape, lambda *_: zeros, pipeline_mode=pl.Buffered(1))


def _cast_slice_specs(side, n_steps, step_of):
    in_specs, out_specs, out_shapes = [], [], []
    for arr, col in side:
        rows = arr.shape[0] // n_steps
        assert rows * n_steps == arr.shape[0] and rows % 16 == 0
        cidx, width = col if col is not None else (0, arr.shape[1])
        in_specs.append(pl.BlockSpec(
            (rows, width), functools.partial(lambda *ids, cidx: (step_of(*ids), cidx), cidx=cidx)))
        out_specs.append(pl.BlockSpec((rows, width), lambda *ids: (step_of(*ids), 0)))
        out_shapes.append(jax.ShapeDtypeStruct((arr.shape[0], width), BF16))
    return in_specs, out_specs, out_shapes


def _cast_slices(src_refs, dst_refs):
    for src, dst in zip(src_refs, dst_refs):
        dst[...] = src[...].astype(dst.dtype)


def _phase_major(ref, tm, d):
    if d == 1:
        return ref[...]
    n = tm // d
    return jnp.concatenate([ref[pl.ds(r, n, stride=d), :] for r in range(d)], axis=0)


def _proj_kernel(*refs, tm, n_slabs, n_side):
    refs = list(refs)
    x_refs = [refs.pop(0) for _ in range(n_slabs)]
    cos_ref, sin_ref, w32_ref = [refs.pop(0) for _ in range(3)]
    side_in = [refs.pop(0) for _ in range(n_side)]
    o_refs = [refs.pop(0) for _ in range(N_GROUPS)]
    u_ref = refs.pop(0)
    side_out = [refs.pop(0) for _ in range(n_side)]
    w_ref, = refs

    @pl.when((pl.program_id(0) == 0) & (pl.program_id(1) == 0))
    def _():
        for c0 in range(0, w_ref.shape[1], ATTN_WIDTH):
            w_ref[:, c0:c0 + ATTN_WIDTH] = w32_ref[:, c0:c0 + ATTN_WIDTH].astype(BF16)

    _cast_slices(side_in, side_out)

    lane = lax.broadcasted_iota(jnp.int32, (1, LANES), 1)
    first_half = (lane % HEAD_DIM) < (HEAD_DIM // 2)

    for gi, d in enumerate(DILATIONS):
        n = tm // d
        xg = jnp.concatenate(
            [_phase_major(xr, tm, d).astype(BF16) for xr in x_refs], axis=1)
        cos = _phase_major(cos_ref, tm, d)
        sin = _phase_major(sin_ref, tm, d)
        if gi == 0:
            u_ref[...] = jnp.dot(xg, w_ref[:, 3 * QKV_WIDTH:], preferred_element_type=F32)
        for part in range(3):
            c0 = part * ATTN_WIDTH
            w0 = part * QKV_WIDTH + gi * ATTN_WIDTH
            acc = jnp.dot(xg, w_ref[:, w0:w0 + ATTN_WIDTH], preferred_element_type=F32)
            for c in range(ATTN_WIDTH // LANES):
                t = acc[:, c * LANES:(c + 1) * LANES]
                if part < 2:
                    rot = jnp.where(first_half,
                                    pltpu.roll(t, LANES - HEAD_DIM // 2, 1),
                                    pltpu.roll(t, HEAD_DIM // 2, 1))
                    t = t * cos + rot * sin
                if part == 0:
                    t = t * Q_SCALE
                t = t.astype(BF16)
                col = c0 + c * LANES
                if d == 1:
                    o_refs[gi][:, col:col + LANES] = t
                else:
                    for r in range(d):
                        o_refs[gi][r, :, col:col + LANES] = t[r * n:(r + 1) * n]


def _project(x, cos_t, sin_t, w_in, side, tm=512):
    b, s, dm = x.shape
    n_slabs = dm // LANES
    n_proj = 3 * QKV_WIDTH + SSM_WIDTH
    grid = (b, s // tm)
    side_in, side_out, side_shapes = _cast_slice_specs(
        side, b * (s // tm), lambda bi, i: bi * (s // tm) + i)
    x_specs = [pl.BlockSpec((None, tm, LANES), functools.partial(lambda bi, i, c: (bi, i, c), c=c))
               for c in range(n_slabs)]
    tab_spec = pl.BlockSpec((tm, LANES), lambda bi, i: (i, 0))
    out_shapes = []
    out_specs = []
    for d in DILATIONS:
        if d == 1:
            out_shapes.append(jax.ShapeDtypeStruct((b, s, QKV_WIDTH), BF16))
            out_specs.append(pl.BlockSpec((None, tm, QKV_WIDTH), lambda bi, i: (bi, i, 0)))
        else:
            out_shapes.append(jax.ShapeDtypeStruct((b, d, s // d, QKV_WIDTH), BF16))
            out_specs.append(pl.BlockSpec((None, d, tm // d, QKV_WIDTH),
                                          lambda bi, i: (bi, 0, i, 0)))
    out_shapes.append(jax.ShapeDtypeStruct((b, s, SSM_WIDTH), F32))
    out_specs.append(pl.BlockSpec((None, tm, SSM_WIDTH), lambda bi, i: (bi, i, 0)))
    outs = pl.pallas_call(
        functools.partial(_proj_kernel, tm=tm, n_slabs=n_slabs, n_side=len(side)),
        grid=grid,
        in_specs=x_specs + [tab_spec, tab_spec,
                            pl.BlockSpec((dm, n_proj), lambda bi, i: (0, 0),
                                         pipeline_mode=pl.Buffered(1))] + side_in,
        out_specs=out_specs + side_out,
        out_shape=out_shapes + side_shapes,
        scratch_shapes=[pltpu.VMEM((dm, n_proj), BF16)],
        compiler_params=pltpu.CompilerParams(
            dimension_semantics=("arbitrary", "arbitrary"), vmem_limit_bytes=VMEM_LIMIT),
        name="in_proj",
    )(*([x] * n_slabs), cos_t, sin_t, w_in, *[arr for arr, _ in side])
    n_main = N_GROUPS + 1
    return outs[:n_main], outs[n_main:]


def _attn_bias():
    blk = ATTN_BLOCK
    a = (np.arange(2 * blk) % blk)[:, None]
    c = np.arange(2 * blk)[None, :]
    in_window = np.where(c < blk, c >= a, c - blk <= a)
    windowed = np.where(in_window, 0.0, NEG_INF)
    causal = np.where(c[:, :blk] <= a, 0.0, NEG_INF)
    return windowed.astype(np.float32), causal.astype(np.float32)


def _attn_kernel(*refs, s, lags):
    qkv = refs[:3 * N_GROUPS]
    mask_w_ref, mask_c_ref = refs[3 * N_GROUPS:3 * N_GROUPS + 2]
    o_ref = refs[3 * N_GROUPS + 2]
    acc_s, m_s, l_s = refs[3 * N_GROUPS + 3:]
    blk = ATTN_BLOCK
    n_blocks = s // blk

    lane = lax.broadcasted_iota(jnp.int32, (1, LANES), 1)
    head_a = lane < HEAD_DIM

    def window(ref, i, nblk):
        start = i if i % nblk == 0 else i - 1
        return ref[start * blk:(i + 1) * blk, :]

    def scores(gi, i, nblk):
        q_ref, k_ref, _ = qkv[3 * gi:3 * gi + 3]
        q = q_ref[i * blk:(i + 1) * blk, :]
        zero = jnp.zeros_like(q)
        q2 = jnp.concatenate([jnp.where(head_a, q, zero), jnp.where(head_a, zero, q)], axis=0)
        mask = mask_c_ref[...] if i % nblk == 0 else mask_w_ref[...]
        return lax.dot_general(q2, window(k_ref, i, nblk), (((1,), (1,)), ((), ())),
                               preferred_element_type=F32) + mask

    def softmax(sc):
        m = jnp.max(sc, axis=1, keepdims=True)
        return m, jnp.exp2((sc - m).astype(BF16))

    def finish(gi, d, i, nblk, m, p):
        vw = window(qkv[3 * gi + 2], i, nblk)
        v_ext = jnp.concatenate([vw, jnp.ones_like(vw)], axis=1)
        res = jnp.dot(p, v_ext, preferred_element_type=F32)
        acc = jnp.where(head_a, res[:blk, :LANES], res[blk:, :LANES])
        l = jnp.where(head_a, res[:blk, LANES:], res[blk:, LANES:])
        m = jnp.where(head_a, m[:blk], m[blk:])
        if d > 1:
            r, m0 = divmod(i * blk, s // d)
            rows = pl.ds(m0 * d + r, blk, stride=d)
            acc_s[gi - 1, rows, :] = acc
            l_s[gi - 1, rows, :] = l
            m_s[gi - 1, rows, :] = m
            return
        rows = slice(i * blk, (i + 1) * blk)
        parts = [(m, l, acc)] + [(m_s[g, rows, :], l_s[g, rows, :], acc_s[g, rows, :])
                                 for g in range(N_GROUPS - 1)]
        m_all = functools.reduce(jnp.maximum, [pm for pm, _, _ in parts])
        num = jnp.zeros((blk, LANES), F32)
        den = jnp.zeros((blk, LANES), F32)
        for pm, pl_, pacc in parts:
            wgt = jnp.exp2(pm - m_all)
            num = num + wgt * pacc
            den = den + wgt * pl_
        o_ref[rows, :] = (num / den).astype(o_ref.dtype)

    assert DILATIONS[0] == 1
    work = [(gi, DILATIONS[gi], i, (s // DILATIONS[gi]) // blk)
            for gi in reversed(range(N_GROUPS)) for i in range(n_blocks)]
    lag_softmax, lag_values = lags
    sc_q, p_q = {}, {}
    for step in range(len(work) + lag_values):
        if step < len(work):
            gi, d, i, nblk = work[step]
            sc_q[step] = scores(gi, i, nblk)
        if 0 <= step - lag_softmax < len(work):
            p_q[step - lag_softmax] = softmax(sc_q.pop(step - lag_softmax))
        if step >= lag_values:
            gi, d, i, nblk = work[step - lag_values]
            finish(gi, d, i, nblk, *p_q.pop(step - lag_values))


def _attention(qkv_groups, b, s, lags=(1, 4)):
    n_hp = ATTN_WIDTH // LANES
    in_specs, args = [], []
    for gi, d in enumerate(DILATIONS):
        arr = qkv_groups[gi].reshape(b, s, QKV_WIDTH)
        for part in range(3):
            in_specs.append(pl.BlockSpec(
                (None, s, LANES),
                functools.partial(lambda bi, hp, part: (bi, 0, part * n_hp + hp), part=part)))
            args.append(arr)
    masks = _attn_bias()
    return pl.pallas_call(
        functools.partial(_attn_kernel, s=s, lags=lags),
        grid=(b, n_hp),
        in_specs=in_specs + [_const_spec(t.shape) for t in masks],
        out_specs=pl.BlockSpec((None, s, LANES), lambda bi, hp: (bi, 0, hp)),
        out_shape=jax.ShapeDtypeStruct((b, s, ATTN_WIDTH), BF16),
        scratch_shapes=[pltpu.VMEM((N_GROUPS - 1, s, LANES), F32)] * 3,
        compiler_params=pltpu.CompilerParams(
            dimension_semantics=("parallel", "parallel"), vmem_limit_bytes=VMEM_LIMIT),
        name="attention",
    )(*args, *masks)


def _gelu_tanh(x):
    return 0.5 * x * (1.0 + jnp.tanh(math.sqrt(2.0 / math.pi) * (x + 0.044715 * (x * x * x))))


def _ssm_kernel(u_ref, bm_ref, are_ref, aim_ref, cm_ref, dskip_ref, o_ref,
                st_a, st_b, ub_s, y_s, h_ref, *, tc, bsz):
    rows = tc * bsz
    half = SSM_PAIRS // 2
    pairs_per_slab = LANES // (2 * SSM_GROUP)
    n_slabs = SSM_WIDTH // LANES
    splits = 2
    n_slices = half * splits
    steps = tc // n_slices
    srows = rows // splits

    @pl.when(pl.program_id(0) == 0)
    def _():
        h_ref[...] = jnp.zeros_like(h_ref)

    for c in range(n_slabs):
        cols = slice(c * LANES, (c + 1) * LANES)
        for b in range(bsz):
            y_s[c, pl.ds(b, tc, stride=bsz), :] = u_ref[b, :, cols]
        u = y_s[c]
        ub_s[c] = u.astype(BF16)
        y_s[c] = dskip_ref[:, cols] * u

    def project_in(buf, j, jl, rs):
        bu = jnp.dot(ub_s[j // pairs_per_slab, rs, :], bm_ref[j], preferred_element_type=F32)
        buf[0, jl, rs, :] = bu[:, :LANES]
        buf[1, jl, rs, :] = bu[:, LANES:]

    def project_out(buf, j, jl, rs):
        st = jnp.concatenate([buf[0, jl, rs, :].astype(BF16), buf[1, jl, rs, :].astype(BF16)],
                             axis=1)
        y_s[j // pairs_per_slab, rs, :] += jnp.dot(st, cm_ref[j], preferred_element_type=F32)

    def scan(buf, j0, other):
        a_re = [are_ref[j0 + i] for i in range(half)]
        a_im = [aim_ref[j0 + i] for i in range(half)]

        def body(it, carry):
            h_re, h_im = carry
            for k in range(steps):
                r0 = pl.multiple_of((it * steps + k) * bsz, bsz)
                new_re, new_im = [], []
                for i in range(half):
                    n_re = a_re[i] * h_re[i] - a_im[i] * h_im[i] + buf[0, i, pl.ds(r0, bsz), :]
                    n_im = a_re[i] * h_im[i] + a_im[i] * h_re[i] + buf[1, i, pl.ds(r0, bsz), :]
                    buf[0, i, pl.ds(r0, bsz), :] = n_re
                    buf[1, i, pl.ds(r0, bsz), :] = n_im
                    new_re.append(n_re)
                    new_im.append(n_im)
                h_re, h_im = tuple(new_re), tuple(new_im)
            other(it // splits, pl.ds((it % splits) * srows, srows))
            return h_re, h_im

        carry = (tuple(h_ref[0, j0 + i] for i in range(half)),
                 tuple(h_ref[1, j0 + i] for i in range(half)))
        for it in range(n_slices):
            carry = body(it, carry)
        h_re, h_im = carry
        for i in range(half):
            h_ref[0, j0 + i] = h_re[i]
            h_ref[1, j0 + i] = h_im[i]

    for jl in range(half):
        project_in(st_a, jl, jl, slice(None))
    scan(st_a, 0, lambda jl, rs: project_in(st_b, half + jl, jl, rs))
    scan(st_b, half, lambda jl, rs: project_out(st_a, jl, jl, rs))
    for jl in range(half):
        project_out(st_b, half + jl, jl, slice(None))

    for c in range(n_slabs):
        for b in range(bsz):
            y = y_s[c, pl.ds(b, tc, stride=bsz), :]
            o_ref[b, :, c * LANES:(c + 1) * LANES] = _gelu_tanh(y).astype(o_ref.dtype)


def _ssm(u, bm, a_re, a_im, cm, d_skip, tc=128):
    bsz, s, _ = u.shape
    rows = tc * bsz
    half = SSM_PAIRS // 2
    chunk_spec = pl.BlockSpec((bsz, tc, SSM_WIDTH), lambda i: (0, i, 0))
    return pl.pallas_call(
        functools.partial(_ssm_kernel, tc=tc, bsz=bsz),
        grid=(s // tc,),
        in_specs=[chunk_spec,
                  _const_spec(bm.shape), _const_spec(a_re.shape), _const_spec(a_im.shape),
                  _const_spec(cm.shape), _const_spec(d_skip.shape)],
        out_specs=chunk_spec,
        out_shape=jax.ShapeDtypeStruct((bsz, s, SSM_WIDTH), BF16),
        scratch_shapes=[pltpu.VMEM((2, half, rows, LANES), F32),
                        pltpu.VMEM((2, half, rows, LANES), F32),
                        pltpu.VMEM((SSM_WIDTH // LANES, rows, LANES), BF16),
                        pltpu.VMEM((SSM_WIDTH // LANES, rows, LANES), F32),
                        pltpu.VMEM((2, SSM_PAIRS, bsz, LANES), F32)],
        compiler_params=pltpu.CompilerParams(
            dimension_semantics=("arbitrary",), vmem_limit_bytes=VMEM_LIMIT),
        name="ssm",
    )(u, bm, a_re, a_im, cm, d_skip)


def _ssm_matrices(a_re, a_im, log_dt, b_re, b_im, c_re, c_im, bsz):
    g, p, h = SSM_GROUPS, SSM_STATE, SSM_GROUP
    lam_re, lam_im = a_re.astype(F32), a_im.astype(F32)
    dt = jnp.exp(log_dt.astype(F32))[:, None]
    mag = jnp.exp(lam_re * dt)
    abar_re, abar_im = mag * jnp.cos(lam_im * dt), mag * jnp.sin(lam_im * dt)
    num_re, num_im = abar_re - 1.0, abar_im
    den = lam_re * lam_re + lam_im * lam_im
    coef_re = ((num_re * lam_re + num_im * lam_im) / den)[..., None]
    coef_im = ((num_im * lam_re - num_re * lam_im) / den)[..., None]
    b_re, b_im = b_re.astype(F32), b_im.astype(F32)
    bbar_re = coef_re * b_re - coef_im * b_im
    bbar_im = coef_re * b_im + coef_im * b_re

    groups_per_slab = LANES // h
    pairs_per_slab = groups_per_slab // 2
    n_slabs = g // groups_per_slab
    same_pair = jnp.eye(pairs_per_slab, dtype=F32)
    same_group = jnp.eye(2, dtype=F32)

    def in_matrix(bpart):
        b5 = jnp.transpose(bpart, (0, 2, 1)).reshape(n_slabs, pairs_per_slab, 2, h, p)
        dense = (b5[:, :, None, :, :, None, :]
                 * same_pair[None, :, :, None, None, None, None]
                 * same_group[None, None, None, :, None, :, None])
        return dense.reshape(SSM_PAIRS, LANES, 2 * p)

    def out_matrix(cpart):
        c5 = jnp.transpose(cpart, (0, 2, 1)).reshape(n_slabs, pairs_per_slab, 2, p, h)
        dense = (c5[:, :, :, :, None, None, :]
                 * same_pair[None, :, None, None, :, None, None]
                 * same_group[None, None, :, None, None, :, None])
        return dense.reshape(SSM_PAIRS, 2 * p, LANES)

    bm = jnp.concatenate([in_matrix(bbar_re), in_matrix(bbar_im)], axis=2)
    cm = jnp.concatenate([out_matrix(c_re.astype(F32)), out_matrix(-c_im.astype(F32))], axis=1)

    def bcast(apart):
        return jnp.broadcast_to(apart.reshape(SSM_PAIRS, 1, 2 * p), (SSM_PAIRS, bsz, 2 * p))

    return bm.astype(BF16), bcast(abar_re), bcast(abar_im), cm.astype(BF16)


def _layer_norm(z, g, b):
    mu = jnp.mean(z, axis=-1, keepdims=True)
    zc = z - mu
    var = jnp.mean(zc * zc, axis=-1, keepdims=True)
    return zc * lax.rsqrt(var + LN_EPS) * g + b


def _mix_kernel(*refs, alpha, sub, n_side):
    (x_ref, attn_ref, g_ref, wga_ref, wgs_ref, bgate_ref, wattn_ref, wglu_ref,
     wssm_ref, wout_ref, lng_ref, lnb_ref) = refs[:12]
    o_ref = refs[12 + n_side]
    _cast_slices(refs[12:12 + n_side], refs[13 + n_side:])
    dm = x_ref.shape[1]
    for r0 in range(0, x_ref.shape[0], sub):
        rows = slice(r0, r0 + sub)
        x = x_ref[rows, :]
        xb = x.astype(BF16)
        gate_attn = jax.nn.sigmoid(
            jnp.dot(xb, wga_ref[...], preferred_element_type=F32) + bgate_ref[:, :dm])
        gate_ssm = jax.nn.sigmoid(
            jnp.dot(xb, wgs_ref[...], preferred_element_type=F32) + bgate_ref[:, dm:])
        y_attn = jnp.dot(attn_ref[rows, :], wattn_ref[...], preferred_element_type=F32)
        glu = jnp.dot(g_ref[rows, :], wglu_ref[...], preferred_element_type=F32)
        y_s = glu[:, :SSM_WIDTH] * jax.nn.sigmoid(glu[:, SSM_WIDTH:])
        y_ssm = jnp.dot(y_s.astype(BF16), wssm_ref[...], preferred_element_type=F32)
        mixed = gate_attn * y_attn + gate_ssm * y_ssm
        mix_out = jnp.dot(mixed.astype(BF16), wout_ref[...], preferred_element_type=F32)
        o_ref[rows, :] = _layer_norm(alpha * x + mix_out, lng_ref[...], lnb_ref[...])


def _mix(x2, attn2, g2, consts, side, alpha, tm=1024, sub=512):
    rows, dm = x2.shape
    row_spec = lambda w: pl.BlockSpec((tm, w), lambda i: (i, 0))
    side_in, side_out, side_shapes = _cast_slice_specs(side, rows // tm, lambda i: i)
    outs = pl.pallas_call(
        functools.partial(_mix_kernel, alpha=alpha, sub=sub, n_side=len(side)),
        grid=(rows // tm,),
        in_specs=[row_spec(dm), row_spec(ATTN_WIDTH), row_spec(SSM_WIDTH)]
                 + [_const_spec(c.shape) for c in consts] + side_in,
        out_specs=[row_spec(dm)] + side_out,
        out_shape=[jax.ShapeDtypeStruct((rows, dm), F32)] + side_shapes,
        compiler_params=pltpu.CompilerParams(
            dimension_semantics=("parallel",), vmem_limit_bytes=VMEM_LIMIT),
        name="mix",
    )(x2, attn2, g2, *consts, *[arr for arr, _ in side])
    return outs[0], outs[1:]


def _ffn_kernel(h_ref, wg_ref, wu_ref, wd_ref, lng_ref, lnb_ref, o_ref, *, alpha, sub):
    for r0 in range(0, h_ref.shape[0], sub):
        rows = slice(r0, r0 + sub)
        h = h_ref[rows, :]
        hb = h.astype(BF16)
        gate = jnp.dot(hb, wg_ref[...], preferred_element_type=F32)
        up = jnp.dot(hb, wu_ref[...], preferred_element_type=F32)
        act = (gate * jax.nn.sigmoid(gate)) * up
        ff = jnp.dot(act.astype(BF16), wd_ref[...], preferred_element_type=F32)
        o_ref[rows, :] = _layer_norm(alpha * h + ff, lng_ref[...], lnb_ref[...])


def _ffn(h2, wg, wu, wd, lng, lnb, alpha, tm=1024, sub=512):
    rows, dm = h2.shape
    row_spec = pl.BlockSpec((tm, dm), lambda i: (i, 0))
    consts = [wg, wu, wd, lng, lnb]
    return pl.pallas_call(
        functools.partial(_ffn_kernel, alpha=alpha, sub=sub),
        grid=(rows // tm,),
        in_specs=[row_spec] + [_const_spec(c.shape) for c in consts],
        out_specs=row_spec,
        out_shape=jax.ShapeDtypeStruct((rows, dm), F32),
        compiler_params=pltpu.CompilerParams(
            dimension_semantics=("parallel",), vmem_limit_bytes=VMEM_LIMIT),
        name="ffn",
    )(h2, *consts)


def _rope_tables(s):
    half = HEAD_DIM // 2
    f32 = np.float32
    pos = np.arange(s, dtype=f32)
    inv_freq = f32(ROPE_THETA) ** (-np.arange(half, dtype=f32) / f32(half))
    ang = pos[:, None] * inv_freq[None, :]
    cos, sin = np.cos(ang), np.sin(ang)
    reps = LANES // HEAD_DIM
    cos_t = np.concatenate([cos, cos] * reps, axis=1)
    sin_t = np.concatenate([-sin, sin] * reps, axis=1)
    return cos_t.astype(f32), sin_t.astype(f32)


def kernel(x, w_in, b_gate, w_attn_br, w_ssm_br, w_out, ssm_a_re, ssm_a_im, ssm_log_dt, ssm_b_re, ssm_b_im, ssm_c_re, ssm_c_im, ssm_d, w_glu, ln1_g, ln1_b, w_ff_gate, w_ff_up, w_ff_down, ln2_g, ln2_b):
    bsz, s, dm = x.shape
    depth = w_in.shape[0]
    alpha = (2.0 * depth) ** 0.25
    assert all(w // d == ATTN_BLOCK for w, d in zip(WINDOWS, DILATIONS))
    assert s % (ATTN_BLOCK * max(DILATIONS)) == 0
    cos_t, sin_t = _rope_tables(s)
    assert w_in.shape[2] == 3 * QKV_WIDTH + SSM_WIDTH + 2 * dm

    gate_col0 = (3 * QKV_WIDTH + SSM_WIDTH) // dm

    for layer in range(depth):
        outs, (wga, wgs, wattn, wglu, wssm, wout) = _project(
            x, cos_t, sin_t, w_in[layer],
            [(w_in[layer], (gate_col0, dm)), (w_in[layer], (gate_col0 + 1, dm)),
             (w_attn_br[layer], None), (w_glu[layer], None), (w_ssm_br[layer], None),
             (w_out[layer], None)])
        attn = _attention(outs[:N_GROUPS], bsz, s)

        bm, a_re, a_im, cm = _ssm_matrices(
            ssm_a_re[layer], ssm_a_im[layer], ssm_log_dt[layer], ssm_b_re[layer],
            ssm_b_im[layer], ssm_c_re[layer], ssm_c_im[layer], bsz)
        g = _ssm(outs[N_GROUPS], bm, a_re, a_im, cm,
                 ssm_d[layer].reshape(1, SSM_WIDTH).astype(F32))

        h, (wff_gate, wff_up, wff_down) = _mix(
            x.reshape(bsz * s, dm), attn.reshape(bsz * s, ATTN_WIDTH),
            g.reshape(bsz * s, SSM_WIDTH),
            [wga, wgs, b_gate[layer].reshape(1, 2 * dm).astype(F32), wattn, wglu, wssm, wout,
             ln1_g[layer].reshape(1, dm).astype(F32), ln1_b[layer].reshape(1, dm).astype(F32)],
            [(w_ff_gate[layer], None), (w_ff_up[layer], None), (w_ff_down[layer], None)],
            alpha)
        x2 = _ffn(h, wff_gate, wff_up, wff_down,
                  ln2_g[layer].reshape(1, dm).astype(F32), ln2_b[layer].reshape(1, dm).astype(F32),
                  alpha)
        x = x2.reshape(bsz, s, dm)
    return x
```

```python
import functools
import math

import jax
import jax.numpy as jnp
import numpy as np
from jax import lax
from jax.experimental import pallas as pl
from jax.experimental.pallas import tpu as pltpu

F32 = jnp.float32
BF16 = jnp.bfloat16

LANES = 128
HEAD_DIM = 64
ATTN_HEADS = 8
ATTN_WIDTH = ATTN_HEADS * HEAD_DIM
DILATIONS = (1, 4, 16)
WINDOWS = (128, 512, 2048)
N_GROUPS = len(DILATIONS)
QKV_WIDTH = N_GROUPS * ATTN_WIDTH
ATTN_BLOCK = 128
ROPE_THETA = 10000.0
NEG_INF = -1e30
Q_SCALE = math.log2(math.e) / math.sqrt(HEAD_DIM)
SSM_GROUP = 16
SSM_GROUPS = 32
SSM_WIDTH = SSM_GROUP * SSM_GROUPS
SSM_STATE = 64
SSM_PAIRS = SSM_GROUPS // 2
LN_EPS = 1e-5

VMEM_LIMIT = 56 * 1024 * 1024


def _const_spec(shape):
    zeros = (0,) * len(shape)
    return pl.BlockSpec(shape, lambda *_: zeros, pipeline_mode=pl.Buffered(1))


def _cast_slice_specs(side, n_steps, step_of):
    in_specs, out_specs, out_shapes = [], [], []
    for arr, col in side:
        rows = arr.shape[0] // n_steps
        assert rows * n_steps == arr.shape[0] and rows % 16 == 0
        cidx, width = col if col is not None else (0, arr.shape[1])
        in_specs.append(pl.BlockSpec(
            (rows, width), functools.partial(lambda *ids, cidx: (step_of(*ids), cidx), cidx=cidx)))
        out_specs.append(pl.BlockSpec((rows, width), lambda *ids: (step_of(*ids), 0)))
        out_shapes.append(jax.ShapeDtypeStruct((arr.shape[0], width), BF16))
    return in_specs, out_specs, out_shapes


def _cast_slices(src_refs, dst_refs):
    for src, dst in zip(src_refs, dst_refs):
        dst[...] = src[...].astype(dst.dtype)


def _phase_major(ref, tm, d):
    if d == 1:
        return ref[...]
    n = tm // d
    return jnp.concatenate([ref[pl.ds(r, n, stride=d), :] for r in range(d)], axis=0)


def _proj_kernel(*refs, tm, n_slabs, n_side):
    refs = list(refs)
    x_refs = [refs.pop(0) for _ in range(n_slabs)]
    cos_ref, sin_ref, w32_ref = [refs.pop(0) for _ in range(3)]
    side_in = [refs.pop(0) for _ in range(n_side)]
    o_refs = [refs.pop(0) for _ in range(N_GROUPS)]
    u_ref = refs.pop(0)
    side_out = [refs.pop(0) for _ in range(n_side)]
    w_ref, = refs

    @pl.when((pl.program_id(0) == 0) & (pl.program_id(1) == 0))
    def _():
        for c0 in range(0, w_ref.shape[1], ATTN_WIDTH):
            w_ref[:, c0:c0 + ATTN_WIDTH] = w32_ref[:, c0:c0 + ATTN_WIDTH].astype(BF16)

    _cast_slices(side_in, side_out)

    lane = lax.broadcasted_iota(jnp.int32, (1, LANES), 1)
    first_half = (lane % HEAD_DIM) < (HEAD_DIM // 2)

    for gi, d in enumerate(DILATIONS):
        n = tm // d
        xg = jnp.concatenate(
            [_phase_major(xr, tm, d).astype(BF16) for xr in x_refs], axis=1)
        cos = _phase_major(cos_ref, tm, d)
        sin = _phase_major(sin_ref, tm, d)
        if gi == 0:
            u_ref[...] = jnp.dot(xg, w_ref[:, 3 * QKV_WIDTH:], preferred_element_type=F32)
        for part in range(3):
            c0 = part * ATTN_WIDTH
            w0 = part * QKV_WIDTH + gi * ATTN_WIDTH
            acc = jnp.dot(xg, w_ref[:, w0:w0 + ATTN_WIDTH], preferred_element_type=F32)
            for c in range(ATTN_WIDTH // LANES):
                t = acc[:, c * LANES:(c + 1) * LANES]
                if part < 2:
                    rot = jnp.where(first_half,
                                    pltpu.roll(t, LANES - HEAD_DIM // 2, 1),
                                    pltpu.roll(t, HEAD_DIM // 2, 1))
                    t = t * cos + rot * sin
                if part == 0:
                    t = t * Q_SCALE
                t = t.astype(BF16)
                col = c0 + c * LANES
                if d == 1:
                    o_refs[gi][:, col:col + LANES] = t
                else:
                    for r in range(d):
                        o_refs[gi][r, :, col:col + LANES] = t[r * n:(r + 1) * n]


def _project(x, cos_t, sin_t, w_in, side, tm=512):
    b, s, dm = x.shape
    n_slabs = dm // LANES
    n_proj = 3 * QKV_WIDTH + SSM_WIDTH
    grid = (b, s // tm)
    side_in, side_out, side_shapes = _cast_slice_specs(
        side, b * (s // tm), lambda bi, i: bi * (s // tm) + i)
    x_specs = [pl.BlockSpec((None, tm, LANES), functools.partial(lambda bi, i, c: (bi, i, c), c=c))
               for c in range(n_slabs)]
    tab_spec = pl.BlockSpec((tm, LANES), lambda bi, i: (i, 0))
    out_shapes = []
    out_specs = []
    for d in DILATIONS:
        if d == 1:
            out_shapes.append(jax.ShapeDtypeStruct((b, s, QKV_WIDTH), BF16))
            out_specs.append(pl.BlockSpec((None, tm, QKV_WIDTH), lambda bi, i: (bi, i, 0)))
        else:
            out_shapes.append(jax.ShapeDtypeStruct((b, d, s // d, QKV_WIDTH), BF16))
            out_specs.append(pl.BlockSpec((None, d, tm // d, QKV_WIDTH),
                                          lambda bi, i: (bi, 0, i, 0)))
    out_shapes.append(jax.ShapeDtypeStruct((b, s, SSM_WIDTH), F32))
    out_specs.append(pl.BlockSpec((None, tm, SSM_WIDTH), lambda bi, i: (bi, i, 0)))
    outs = pl.pallas_call(
        functools.partial(_proj_kernel, tm=tm, n_slabs=n_slabs, n_side=len(side)),
        grid=grid,
        in_specs=x_specs + [tab_spec, tab_spec,
                            pl.BlockSpec((dm, n_proj), lambda bi, i: (0, 0),
                                         pipeline_mode=pl.Buffered(1))] + side_in,
        out_specs=out_specs + side_out,
        out_shape=out_shapes + side_shapes,
        scratch_shapes=[pltpu.VMEM((dm, n_proj), BF16)],
        compiler_params=pltpu.CompilerParams(
            dimension_semantics=("arbitrary", "arbitrary"), vmem_limit_bytes=VMEM_LIMIT),
        name="in_proj",
    )(*([x] * n_slabs), cos_t, sin_t, w_in, *[arr for arr, _ in side])
    n_main = N_GROUPS + 1
    return outs[:n_main], outs[n_main:]


def _attn_bias():
    blk = ATTN_BLOCK
    a = (np.arange(2 * blk) % blk)[:, None]
    c = np.arange(2 * blk)[None, :]
    in_window = np.where(c < blk, c >= a, c - blk <= a)
    windowed = np.where(in_window, 0.0, NEG_INF)
    causal = np.where(c[:, :blk] <= a, 0.0, NEG_INF)
    return windowed.astype(np.float32), causal.astype(np.float32)


def _attn_kernel(*refs, s, lags):
    qkv = refs[:3 * N_GROUPS]
    mask_w_ref, mask_c_ref = refs[3 * N_GROUPS:3 * N_GROUPS + 2]
    o_ref = refs[3 * N_GROUPS + 2]
    acc_s, m_s, l_s = refs[3 * N_GROUPS + 3:]
    blk = ATTN_BLOCK
    n_blocks = s // blk

    lane = lax.broadcasted_iota(jnp.int32, (1, LANES), 1)
    head_a = lane < HEAD_DIM

    def window(ref, i, nblk):
        start = i if i % nblk == 0 else i - 1
        return ref[start * blk:(i + 1) * blk, :]

    def scores(gi, i, nblk):
        q_ref, k_ref, _ = qkv[3 * gi:3 * gi + 3]
        q = q_ref[i * blk:(i + 1) * blk, :]
        zero = jnp.zeros_like(q)
        q2 = jnp.concatenate([jnp.where(head_a, q, zero), jnp.where(head_a, zero, q)], axis=0)
        mask = mask_c_ref[...] if i % nblk == 0 else mask_w_ref[...]
        return lax.dot_general(q2, window(k_ref, i, nblk), (((1,), (1,)), ((), ())),
                               preferred_element_type=F32) + mask

    def softmax(sc):
        m = jnp.max(sc, axis=1, keepdims=True)
        return m, jnp.exp2((sc - m).astype(BF16))

    def finish(gi, d, i, nblk, m, p):
        vw = window(qkv[3 * gi + 2], i, nblk)
        v_ext = jnp.concatenate([vw, jnp.ones_like(vw)], axis=1)
        res = jnp.dot(p, v_ext, preferred_element_type=F32)
        acc = jnp.where(head_a, res[:blk, :LANES], res[blk:, :LANES])
        l = jnp.where(head_a, res[:blk, LANES:], res[blk:, LANES:])
        m = jnp.where(head_a, m[:blk], m[blk:])
        if d > 1:
            r, m0 = divmod(i * blk, s // d)
            rows = pl.ds(m0 * d + r, blk, stride=d)
            acc_s[gi - 1, rows, :] = acc
            l_s[gi - 1, rows, :] = l
            m_s[gi - 1, rows, :] = m
            return
        rows = slice(i * blk, (i + 1) * blk)
        parts = [(m, l, acc)] + [(m_s[g, rows, :], l_s[g, rows, :], acc_s[g, rows, :])
                                 for g in range(N_GROUPS - 1)]
        m_all = functools.reduce(jnp.maximum, [pm for pm, _, _ in parts])
        num = jnp.zeros((blk, LANES), F32)
        den = jnp.zeros((blk, LANES), F32)
        for pm, pl_, pacc in parts:
            wgt = jnp.exp2(pm - m_all)
            num = num + wgt * pacc
            den = den + wgt * pl_
        o_ref[rows, :] = (num / den).astype(o_ref.dtype)

    assert DILATIONS[0] == 1
    work = [(gi, DILATIONS[gi], i, (s // DILATIONS[gi]) // blk)
            for gi in reversed(range(N_GROUPS)) for i in range(n_blocks)]
    lag_softmax, lag_values = lags
    sc_q, p_q = {}, {}
    for step in range(len(work) + lag_values):
        if step < len(work):
            gi, d, i, nblk = work[step]
            sc_q[step] = scores(gi, i, nblk)
        if 0 <= step - lag_softmax < len(work):
            p_q[step - lag_softmax] = softmax(sc_q.pop(step - lag_softmax))
        if step >= lag_values:
            gi, d, i, nblk = work[step - lag_values]
            finish(gi, d, i, nblk, *p_q.pop(step - lag_values))


def _attention(qkv_groups, b, s, lags=(1, 4)):
    n_hp = ATTN_WIDTH // LANES
    in_specs, args = [], []
    for gi, d in enumerate(DILATIONS):
        arr = qkv_groups[gi].reshape(b, s, QKV_WIDTH)
        for part in range(3):
            in_specs.append(pl.BlockSpec(
                (None, s, LANES),
                functools.partial(lambda bi, hp, part: (bi, 0, part * n_hp + hp), part=part)))
            args.append(arr)
    masks = _attn_bias()
    return pl.pallas_call(
        functools.partial(_attn_kernel, s=s, lags=lags),
        grid=(b, n_hp),
        in_specs=in_specs + [_const_spec(t.shape) for t in masks],
        out_specs=pl.BlockSpec((None, s, LANES), lambda bi, hp: (bi, 0, hp)),
        out_shape=jax.ShapeDtypeStruct((b, s, ATTN_WIDTH), BF16),
        scratch_shapes=[pltpu.VMEM((N_GROUPS - 1, s, LANES), F32)] * 3,
        compiler_params=pltpu.CompilerParams(
            dimension_semantics=("parallel", "parallel"), vmem_limit_bytes=VMEM_LIMIT),
        name="attention",
    )(*args, *masks)


def _gelu_tanh(x):
    return 0.5 * x * (1.0 + jnp.tanh(math.sqrt(2.0 / math.pi) * (x + 0.044715 * (x * x * x))))


def _ssm_kernel(u_ref, bm_ref, are_ref, aim_ref, cm_ref, dskip_ref, o_ref,
                st_a, st_b, ub_s, y_s, h_ref, *, tc, bsz):
    rows = tc * bsz
    half = SSM_PAIRS // 2
    pairs_per_slab = LANES // (2 * SSM_GROUP)
    n_slabs = SSM_WIDTH // LANES
    splits = 2
    n_slices = half * splits
    steps = tc // n_slices
    srows = rows // splits

    @pl.when(pl.program_id(0) == 0)
    def _():
        h_ref[...] = jnp.zeros_like(h_ref)

    for c in range(n_slabs):
        cols = slice(c * LANES, (c + 1) * LANES)
        for b in range(bsz):
            y_s[c, pl.ds(b, tc, stride=bsz), :] = u_ref[b, :, cols]
        u = y_s[c]
        ub_s[c] = u.astype(BF16)
        y_s[c] = dskip_ref[:, cols] * u

    def project_in(buf, j, jl, rs):
        bu = jnp.dot(ub_s[j // pairs_per_slab, rs, :], bm_ref[j], preferred_element_type=F32)
        buf[0, jl, rs, :] = bu[:, :LANES]
        buf[1, jl, rs, :] = bu[:, LANES:]

    def project_out(buf, j, jl, rs):
        st = jnp.concatenate([buf[0, jl, rs, :].astype(BF16), buf[1, jl, rs, :].astype(BF16)],
                             axis=1)
        y_s[j // pairs_per_slab, rs, :] += jnp.dot(st, cm_ref[j], preferred_element_type=F32)

    def scan(buf, j0, other):
        a_re = [are_ref[j0 + i] for i in range(half)]
        a_im = [aim_ref[j0 + i] for i in range(half)]

        def body(it, carry):
            h_re, h_im = carry
            for k in range(steps):
                r0 = pl.multiple_of((it * steps + k) * bsz, bsz)
                new_re, new_im = [], []
                for i in range(half):
                    n_re = a_re[i] * h_re[i] - a_im[i] * h_im[i] + buf[0, i, pl.ds(r0, bsz), :]
                    n_im = a_re[i] * h_im[i] + a_im[i] * h_re[i] + buf[1, i, pl.ds(r0, bsz), :]
                    buf[0, i, pl.ds(r0, bsz), :] = n_re
                    buf[1, i, pl.ds(r0, bsz), :] = n_im
                    new_re.append(n_re)
                    new_im.append(n_im)
                h_re, h_im = tuple(new_re), tuple(new_im)
            other(it // splits, pl.ds((it % splits) * srows, srows))
            return h_re, h_im

        carry = (tuple(h_ref[0, j0 + i] for i in range(half)),
                 tuple(h_ref[1, j0 + i] for i in range(half)))
        for it in range(n_slices):
            carry = body(it, carry)
        h_re, h_im = carry
        for i in range(half):
            h_ref[0, j0 + i] = h_re[i]
            h_ref[1, j0 + i] = h_im[i]

    for jl in range(half):
        project_in(st_a, jl, jl, slice(None))
    scan(st_a, 0, lambda jl, rs: project_in(st_b, half + jl, jl, rs))
    scan(st_b, half, lambda jl, rs: project_out(st_a, jl, jl, rs))
    for jl in range(half):
        project_out(st_b, half + jl, jl, slice(None))

    for c in range(n_slabs):
        for b in range(bsz):
            y = y_s[c, pl.ds(b, tc, stride=bsz), :]
            o_ref[b, :, c * LANES:(c + 1) * LANES] = _gelu_tanh(y).astype(o_ref.dtype)


def _ssm(u, bm, a_re, a_im, cm, d_skip, tc=128):
    bsz, s, _ = u.shape
    rows = tc * bsz
    half = SSM_PAIRS // 2
    chunk_spec = pl.BlockSpec((bsz, tc, SSM_WIDTH), lambda i: (0, i, 0))
    return pl.pallas_call(
        functools.partial(_ssm_kernel, tc=tc, bsz=bsz),
        grid=(s // tc,),
        in_specs=[chunk_spec,
                  _const_spec(bm.shape), _const_spec(a_re.shape), _const_spec(a_im.shape),
                  _const_spec(cm.shape), _const_spec(d_skip.shape)],
        out_specs=chunk_spec,
        out_shape=jax.ShapeDtypeStruct((bsz, s, SSM_WIDTH), BF16),
        scratch_shapes=[pltpu.VMEM((2, half, rows, LANES), F32),
                        pltpu.VMEM((2, half, rows, LANES), F32),
                        pltpu.VMEM((SSM_WIDTH // LANES, rows, LANES), BF16),
                        pltpu.VMEM((SSM_WIDTH // LANES, rows, LANES), F32),
                        pltpu.VMEM((2, SSM_PAIRS, bsz, LANES), F32)],
        compiler_params=pltpu.CompilerParams(
            dimension_semantics=("arbitrary",), vmem_limit_bytes=VMEM_LIMIT),
        name="ssm",
    )(u, bm, a_re, a_im, cm, d_skip)


def _ssm_matrices(a_re, a_im, log_dt, b_re, b_im, c_re, c_im, bsz):
    g, p, h = SSM_GROUPS, SSM_STATE, SSM_GROUP
    lam_re, lam_im = a_re.astype(F32), a_im.astype(F32)
    dt = jnp.exp(log_dt.astype(F32))[:, None]
    mag = jnp.exp(lam_re * dt)
    abar_re, abar_im = mag * jnp.cos(lam_im * dt), mag * jnp.sin(lam_im * dt)
    num_re, num_im = abar_re - 1.0, abar_im
    den = lam_re * lam_re + lam_im * lam_im
    coef_re = ((num_re * lam_re + num_im * lam_im) / den)[..., None]
    coef_im = ((num_im * lam_re - num_re * lam_im) / den)[..., None]
    b_re, b_im = b_re.astype(F32), b_im.astype(F32)
    bbar_re = coef_re * b_re - coef_im * b_im
    bbar_im = coef_re * b_im + coef_im * b_re

    groups_per_slab = LANES // h
    pairs_per_slab = groups_per_slab // 2
    n_slabs = g // groups_per_slab
    eye = jnp.eye(groups_per_slab, dtype=F32)

    def in_matrix(bpart):
        bt = jnp.transpose(bpart, (0, 2, 1)).reshape(n_slabs, groups_per_slab, h, p)
        dense = bt[:, :, :, None, :] * eye[None, :, None, :, None]
        dense = dense.reshape(n_slabs, LANES, pairs_per_slab, 2 * p)
        return jnp.transpose(dense, (0, 2, 1, 3)).reshape(SSM_PAIRS, LANES, 2 * p)

    def out_matrix(cpart):
        ct = jnp.transpose(cpart, (0, 2, 1)).reshape(n_slabs, groups_per_slab, p, h)
        dense = ct[:, :, :, None, :] * eye[None, :, None, :, None]
        return dense.reshape(SSM_PAIRS, 2 * p, LANES)

    bm = jnp.concatenate([in_matrix(bbar_re), in_matrix(bbar_im)], axis=2)
    cm = jnp.concatenate([out_matrix(c_re.astype(F32)), out_matrix(-c_im.astype(F32))], axis=1)

    def bcast(apart):
        return jnp.broadcast_to(apart.reshape(SSM_PAIRS, 1, 2 * p), (SSM_PAIRS, bsz, 2 * p))

    return bm.astype(BF16), bcast(abar_re), bcast(abar_im), cm.astype(BF16)


def _layer_norm(z, g, b):
    mu = jnp.mean(z, axis=-1, keepdims=True)
    zc = z - mu
    var = jnp.mean(zc * zc, axis=-1, keepdims=True)
    return zc * lax.rsqrt(var + LN_EPS) * g + b


def _mix_kernel(*refs, alpha, sub, n_side):
    (x_ref, attn_ref, g_ref, wga_ref, wgs_ref, bgate_ref, wattn_ref, wglu_ref,
     wssm_ref, wout_ref, lng_ref, lnb_ref) = refs[:12]
    o_ref = refs[12 + n_side]
    _cast_slices(refs[12:12 + n_side], refs[13 + n_side:])
    dm = x_ref.shape[1]
    for r0 in range(0, x_ref.shape[0], sub):
        rows = slice(r0, r0 + sub)
        x = x_ref[rows, :]
        xb = x.astype(BF16)
        gate_attn = jax.nn.sigmoid(
            jnp.dot(xb, wga_ref[...], preferred_element_type=F32) + bgate_ref[:, :dm])
        gate_ssm = jax.nn.sigmoid(
            jnp.dot(xb, wgs_ref[...], preferred_element_type=F32) + bgate_ref[:, dm:])
        y_attn = jnp.dot(attn_ref[rows, :], wattn_ref[...], preferred_element_type=F32)
        glu = jnp.dot(g_ref[rows, :], wglu_ref[...], preferred_element_type=F32)
        y_s = glu[:, :SSM_WIDTH] * jax.nn.sigmoid(glu[:, SSM_WIDTH:])
        y_ssm = jnp.dot(y_s.astype(BF16), wssm_ref[...], preferred_element_type=F32)
        mixed = gate_attn * y_attn + gate_ssm * y_ssm
        mix_out = jnp.dot(mixed.astype(BF16), wout_ref[...], preferred_element_type=F32)
        o_ref[rows, :] = _layer_norm(alpha * x + mix_out, lng_ref[...], lnb_ref[...])


def _mix(x2, attn2, g2, consts, side, alpha, tm=1024, sub=512):
    rows, dm = x2.shape
    row_spec = lambda w: pl.BlockSpec((tm, w), lambda i: (i, 0))
    side_in, side_out, side_shapes = _cast_slice_specs(side, rows // tm, lambda i: i)
    outs = pl.pallas_call(
        functools.partial(_mix_kernel, alpha=alpha, sub=sub, n_side=len(side)),
        grid=(rows // tm,),
        in_specs=[row_spec(dm), row_spec(ATTN_WIDTH), row_spec(SSM_WIDTH)]
                 + [_const_spec(c.shape) for c in consts] + side_in,
        out_specs=[row_spec(dm)] + side_out,
        out_shape=[jax.ShapeDtypeStruct((rows, dm), F32)] + side_shapes,
        compiler_params=pltpu.CompilerParams(
            dimension_semantics=("parallel",), vmem_limit_bytes=VMEM_LIMIT),
        name="mix",
    )(x2, attn2, g2, *consts, *[arr for arr, _ in side])
    return outs[0], outs[1:]


def _ffn_kernel(h_ref, wg_ref, wu_ref, wd_ref, lng_ref, lnb_ref, o_ref, *, alpha, sub):
    for r0 in range(0, h_ref.shape[0], sub):
        rows = slice(r0, r0 + sub)
        h = h_ref[rows, :]
        hb = h.astype(BF16)
        gate = jnp.dot(hb, wg_ref[...], preferred_element_type=F32)
        up = jnp.dot(hb, wu_ref[...], preferred_element_type=F32)
        act = (gate * jax.nn.sigmoid(gate)) * up
        ff = jnp.dot(act.astype(BF16), wd_ref[...], preferred_element_type=F32)
        o_ref[rows, :] = _layer_norm(alpha * h + ff, lng_ref[...], lnb_ref[...])


def _ffn(h2, wg, wu, wd, lng, lnb, alpha, tm=1024, sub=512):
    rows, dm = h2.shape
    row_spec = pl.BlockSpec((tm, dm), lambda i: (i, 0))
    consts = [wg, wu, wd, lng, lnb]
    return pl.pallas_call(
        functools.partial(_ffn_kernel, alpha=alpha, sub=sub),
        grid=(rows // tm,),
        in_specs=[row_spec] + [_const_spec(c.shape) for c in consts],
        out_specs=row_spec,
        out_shape=jax.ShapeDtypeStruct((rows, dm), F32),
        compiler_params=pltpu.CompilerParams(
            dimension_semantics=("parallel",), vmem_limit_bytes=VMEM_LIMIT),
        name="ffn",
    )(h2, *consts)


def _rope_tables(s):
    half = HEAD_DIM // 2
    f32 = np.float32
    pos = np.arange(s, dtype=f32)
    inv_freq = f32(ROPE_THETA) ** (-np.arange(half, dtype=f32) / f32(half))
    ang = pos[:, None] * inv_freq[None, :]
    cos, sin = np.cos(ang), np.sin(ang)
    reps = LANES // HEAD_DIM
    cos_t = np.concatenate([cos, cos] * reps, axis=1)
    sin_t = np.concatenate([-sin, sin] * reps, axis=1)
    return cos_t.astype(f32), sin_t.astype(f32)


def kernel(x, w_in, b_gate, w_attn_br, w_ssm_br, w_out, ssm_a_re, ssm_a_im, ssm_log_dt, ssm_b_re, ssm_b_im, ssm_c_re, ssm_c_im, ssm_d, w_glu, ln1_g, ln1_b, w_ff_gate, w_ff_up, w_ff_down, ln2_g, ln2_b):
    bsz, s, dm = x.shape
    depth = w_in.shape[0]
    alpha = (2.0 * depth) ** 0.25
    assert all(w // d == ATTN_BLOCK for w, d in zip(WINDOWS, DILATIONS))
    assert s % (ATTN_BLOCK * max(DILATIONS)) == 0
    cos_t, sin_t = _rope_tables(s)
    assert w_in.shape[2] == 3 * QKV_WIDTH + SSM_WIDTH + 2 * dm

    gate_col0 = (3 * QKV_WIDTH + SSM_WIDTH) // dm

    for layer in range(depth):
        outs, (wga, wgs, wattn, wglu, wssm, wout) = _project(
            x, cos_t, sin_t, w_in[layer],
            [(w_in[layer], (gate_col0, dm)), (w_in[layer], (gate_col0 + 1, dm)),
             (w_attn_br[layer], None), (w_glu[layer], None), (w_ssm_br[layer], None),
             (w_out[layer], None)])
        attn = _attention(outs[:N_GROUPS], bsz, s)

        bm, a_re, a_im, cm = _ssm_matrices(
            ssm_a_re[layer], ssm_a_im[layer], ssm_log_dt[layer], ssm_b_re[layer],
            ssm_b_im[layer], ssm_c_re[layer], ssm_c_im[layer], bsz)
        g = _ssm(outs[N_GROUPS], bm, a_re, a_im, cm,
                 ssm_d[layer].reshape(1, SSM_WIDTH).astype(F32))

        h, (wff_gate, wff_up, wff_down) = _mix(
            x.reshape(bsz * s, dm), attn.reshape(bsz * s, ATTN_WIDTH),
            g.reshape(bsz * s, SSM_WIDTH),
            [wga, wgs, b_gate[layer].reshape(1, 2 * dm).astype(F32), wattn, wglu, wssm, wout,
             ln1_g[layer].reshape(1, dm).astype(F32), ln1_b[layer].reshape(1, dm).astype(F32)],
            [(w_ff_gate[layer], None), (w_ff_up[layer], None), (w_ff_down[layer], None)],
            alpha)
        x2 = _ffn(h, wff_gate, wff_up, wff_down,
                  ln2_g[layer].reshape(1, dm).astype(F32), ln2_b[layer].reshape(1, dm).astype(F32),
                  alpha)
        x = x2.reshape(bsz, s, dm)
    return x
```

```python
import functools
import math

import jax
import jax.numpy as jnp
import numpy as np
from jax import lax
from jax.experimental import pallas as pl
from jax.experimental.pallas import tpu as pltpu

F32 = jnp.float32
BF16 = jnp.bfloat16

LANES = 128
HEAD_DIM = 64
ATTN_HEADS = 8
ATTN_WIDTH = ATTN_HEADS * HEAD_DIM
DILATIONS = (1, 4, 16)
WINDOWS = (128, 512, 2048)
N_GROUPS = len(DILATIONS)
QKV_WIDTH = N_GROUPS * ATTN_WIDTH
ATTN_BLOCK = 128
ROPE_THETA = 10000.0
NEG_INF = -1e30
Q_SCALE = math.log2(math.e) / math.sqrt(HEAD_DIM)
SSM_GROUP = 16
SSM_GROUPS = 32
SSM_WIDTH = SSM_GROUP * SSM_GROUPS
SSM_STATE = 64
SSM_PAIRS = SSM_GROUPS // 2
LN_EPS = 1e-5

VMEM_LIMIT = 56 * 1024 * 1024


def _const_spec(shape):
    zeros = (0,) * len(shape)
    return pl.BlockSpec(shape, lambda *_: zeros, pipeline_mode=pl.Buffered(1))


def _cast_slice_specs(side, n_steps, step_of):
    in_specs, out_specs, out_shapes = [], [], []
    for arr, col in side:
        rows = arr.shape[0] // n_steps
        assert rows * n_steps == arr.shape[0] and rows % 16 == 0
        cidx, width = col if col is not None else (0, arr.shape[1])
        in_specs.append(pl.BlockSpec(
            (rows, width), functools.partial(lambda *ids, cidx: (step_of(*ids), cidx), cidx=cidx)))
        out_specs.append(pl.BlockSpec((rows, width), lambda *ids: (step_of(*ids), 0)))
        out_shapes.append(jax.ShapeDtypeStruct((arr.shape[0], width), BF16))
    return in_specs, out_specs, out_shapes


def _cast_slices(src_refs, dst_refs):
    for src, dst in zip(src_refs, dst_refs):
        dst[...] = src[...].astype(dst.dtype)


def _phase_major(ref, tm, d):
    if d == 1:
        return ref[...]
    n = tm // d
    return jnp.concatenate([ref[pl.ds(r, n, stride=d), :] for r in range(d)], axis=0)


def _proj_kernel(*refs, tm, n_slabs, n_side):
    refs = list(refs)
    x_refs = [refs.pop(0) for _ in range(n_slabs)]
    cos_ref, sin_ref, w32_ref = [refs.pop(0) for _ in range(3)]
    side_in = [refs.pop(0) for _ in range(n_side)]
    o_refs = [refs.pop(0) for _ in range(N_GROUPS)]
    u_ref = refs.pop(0)
    side_out = [refs.pop(0) for _ in range(n_side)]
    w_ref, = refs

    @pl.when((pl.program_id(0) == 0) & (pl.program_id(1) == 0))
    def _():
        for c0 in range(0, w_ref.shape[1], ATTN_WIDTH):
            w_ref[:, c0:c0 + ATTN_WIDTH] = w32_ref[:, c0:c0 + ATTN_WIDTH].astype(BF16)

    _cast_slices(side_in, side_out)

    lane = lax.broadcasted_iota(jnp.int32, (1, LANES), 1)
    first_half = (lane % HEAD_DIM) < (HEAD_DIM // 2)

    for gi, d in enumerate(DILATIONS):
        n = tm // d
        xg = jnp.concatenate(
            [_phase_major(xr, tm, d).astype(BF16) for xr in x_refs], axis=1)
        cos = _phase_major(cos_ref, tm, d)
        sin = _phase_major(sin_ref, tm, d)
        if gi == 0:
            u_ref[...] = jnp.dot(xg, w_ref[:, 3 * QKV_WIDTH:], preferred_element_type=F32)
        for part in range(3):
            c0 = part * ATTN_WIDTH
            w0 = part * QKV_WIDTH + gi * ATTN_WIDTH
            acc = jnp.dot(xg, w_ref[:, w0:w0 + ATTN_WIDTH], preferred_element_type=F32)
            for c in range(ATTN_WIDTH // LANES):
                t = acc[:, c * LANES:(c + 1) * LANES]
                if part < 2:
                    rot = jnp.where(first_half,
                                    pltpu.roll(t, LANES - HEAD_DIM // 2, 1),
                                    pltpu.roll(t, HEAD_DIM // 2, 1))
                    t = t * cos + rot * sin
                if part == 0:
                    t = t * Q_SCALE
                t = t.astype(BF16)
                col = c0 + c * LANES
                if d == 1:
                    o_refs[gi][:, col:col + LANES] = t
                else:
                    for r in range(d):
                        o_refs[gi][r, :, col:col + LANES] = t[r * n:(r + 1) * n]


def _project(x, cos_t, sin_t, w_in, side, tm=512):
    b, s, dm = x.shape
    n_slabs = dm // LANES
    n_proj = 3 * QKV_WIDTH + SSM_WIDTH
    grid = (b, s // tm)
    side_in, side_out, side_shapes = _cast_slice_specs(
        side, b * (s // tm), lambda bi, i: bi * (s // tm) + i)
    x_specs = [pl.BlockSpec((None, tm, LANES), functools.partial(lambda bi, i, c: (bi, i, c), c=c))
               for c in range(n_slabs)]
    tab_spec = pl.BlockSpec((tm, LANES), lambda bi, i: (i, 0))
    out_shapes = []
    out_specs = []
    for d in DILATIONS:
        if d == 1:
            out_shapes.append(jax.ShapeDtypeStruct((b, s, QKV_WIDTH), BF16))
            out_specs.append(pl.BlockSpec((None, tm, QKV_WIDTH), lambda bi, i: (bi, i, 0)))
        else:
            out_shapes.append(jax.ShapeDtypeStruct((b, d, s // d, QKV_WIDTH), BF16))
            out_specs.append(pl.BlockSpec((None, d, tm // d, QKV_WIDTH),
                                          lambda bi, i: (bi, 0, i, 0)))
    out_shapes.append(jax.ShapeDtypeStruct((b, s, SSM_WIDTH), F32))
    out_specs.append(pl.BlockSpec((None, tm, SSM_WIDTH), lambda bi, i: (bi, i, 0)))
    outs = pl.pallas_call(
        functools.partial(_proj_kernel, tm=tm, n_slabs=n_slabs, n_side=len(side)),
        grid=grid,
        in_specs=x_specs + [tab_spec, tab_spec,
                            pl.BlockSpec((dm, n_proj), lambda bi, i: (0, 0),
                                         pipeline_mode=pl.Buffered(1))] + side_in,
        out_specs=out_specs + side_out,
        out_shape=out_shapes + side_shapes,
        scratch_shapes=[pltpu.VMEM((dm, n_proj), BF16)],
        compiler_params=pltpu.CompilerParams(
            dimension_semantics=("arbitrary", "arbitrary"), vmem_limit_bytes=VMEM_LIMIT),
        name="in_proj",
    )(*([x] * n_slabs), cos_t, sin_t, w_in, *[arr for arr, _ in side])
    n_main = N_GROUPS + 1
    return outs[:n_main], outs[n_main:]


def _attn_bias():
    blk = ATTN_BLOCK
    a = (np.arange(2 * blk) % blk)[:, None]
    c = np.arange(2 * blk)[None, :]
    in_window = np.where(c < blk, c >= a, c - blk <= a)
    windowed = np.where(in_window, 0.0, NEG_INF)
    causal = np.where(c[:, :blk] <= a, 0.0, NEG_INF)
    return windowed.astype(np.float32), causal.astype(np.float32)


def _attn_kernel(*refs, s, lags):
    qkv = refs[:3 * N_GROUPS]
    mask_w_ref, mask_c_ref = refs[3 * N_GROUPS:3 * N_GROUPS + 2]
    o_ref = refs[3 * N_GROUPS + 2]
    acc_s, m_s, l_s = refs[3 * N_GROUPS + 3:]
    blk = ATTN_BLOCK
    n_blocks = s // blk

    lane = lax.broadcasted_iota(jnp.int32, (1, LANES), 1)
    head_a = lane < HEAD_DIM

    def window(ref, i, nblk):
        start = i if i % nblk == 0 else i - 1
        return ref[start * blk:(i + 1) * blk, :]

    def scores(gi, i, nblk):
        q_ref, k_ref, _ = qkv[3 * gi:3 * gi + 3]
        q = q_ref[i * blk:(i + 1) * blk, :]
        zero = jnp.zeros_like(q)
        q2 = jnp.concatenate([jnp.where(head_a, q, zero), jnp.where(head_a, zero, q)], axis=0)
        mask = mask_c_ref[...] if i % nblk == 0 else mask_w_ref[...]
        return lax.dot_general(q2, window(k_ref, i, nblk), (((1,), (1,)), ((), ())),
                               preferred_element_type=F32) + mask

    def softmax(sc):
        m = jnp.max(sc, axis=1, keepdims=True)
        return m, jnp.exp2((sc - m).astype(BF16))

    def finish(gi, d, i, nblk, m, p):
        vw = window(qkv[3 * gi + 2], i, nblk)
        v_ext = jnp.concatenate([vw, jnp.ones_like(vw)], axis=1)
        res = jnp.dot(p, v_ext, preferred_element_type=F32)
        acc = jnp.where(head_a, res[:blk, :LANES], res[blk:, :LANES])
        l = jnp.where(head_a, res[:blk, LANES:], res[blk:, LANES:])
        m = jnp.where(head_a, m[:blk], m[blk:])
        if d > 1:
            r, m0 = divmod(i * blk, s // d)
            rows = pl.ds(m0 * d + r, blk, stride=d)
            acc_s[gi - 1, rows, :] = acc
            l_s[gi - 1, rows, :] = l
            m_s[gi - 1, rows, :] = m
            return
        rows = slice(i * blk, (i + 1) * blk)
        parts = [(m, l, acc)] + [(m_s[g, rows, :], l_s[g, rows, :], acc_s[g, rows, :])
                                 for g in range(N_GROUPS - 1)]
        m_all = functools.reduce(jnp.maximum, [pm for pm, _, _ in parts])
        num = jnp.zeros((blk, LANES), F32)
        den = jnp.zeros((blk, LANES), F32)
        for pm, pl_, pacc in parts:
            wgt = jnp.exp2(pm - m_all)
            num = num + wgt * pacc
            den = den + wgt * pl_
        o_ref[rows, :] = (num / den).astype(o_ref.dtype)

    assert DILATIONS[0] == 1
    work = [(gi, DILATIONS[gi], i, (s // DILATIONS[gi]) // blk)
            for gi in reversed(range(N_GROUPS)) for i in range(n_blocks)]
    lag_softmax, lag_values = lags
    sc_q, p_q = {}, {}
    for step in range(len(work) + lag_values):
        if step < len(work):
            gi, d, i, nblk = work[step]
            sc_q[step] = scores(gi, i, nblk)
        if 0 <= step - lag_softmax < len(work):
            p_q[step - lag_softmax] = softmax(sc_q.pop(step - lag_softmax))
        if step >= lag_values:
            gi, d, i, nblk = work[step - lag_values]
            finish(gi, d, i, nblk, *p_q.pop(step - lag_values))


def _attention(qkv_groups, b, s, lags=(1, 4)):
    n_hp = ATTN_WIDTH // LANES
    in_specs, args = [], []
    for gi, d in enumerate(DILATIONS):
        arr = qkv_groups[gi].reshape(b, s, QKV_WIDTH)
        for part in range(3):
            in_specs.append(pl.BlockSpec(
                (None, s, LANES),
                functools.partial(lambda bi, hp, part: (bi, 0, part * n_hp + hp), part=part)))
            args.append(arr)
    masks = _attn_bias()
    return pl.pallas_call(
        functools.partial(_attn_kernel, s=s, lags=lags),
        grid=(b, n_hp),
        in_specs=in_specs + [_const_spec(t.shape) for t in masks],
        out_specs=pl.BlockSpec((None, s, LANES), lambda bi, hp: (bi, 0, hp)),
        out_shape=jax.ShapeDtypeStruct((b, s, ATTN_WIDTH), BF16),
        scratch_shapes=[pltpu.VMEM((N_GROUPS - 1, s, LANES), F32)] * 3,
        compiler_params=pltpu.CompilerParams(
            dimension_semantics=("parallel", "parallel"), vmem_limit_bytes=VMEM_LIMIT),
        name="attention",
    )(*args, *masks)


def _gelu_tanh(x):
    return 0.5 * x * (1.0 + jnp.tanh(math.sqrt(2.0 / math.pi) * (x + 0.044715 * (x * x * x))))


def _ssm_kernel(u_ref, bm_ref, are_ref, aim_ref, cm_ref, dskip_ref, o_ref,
                st_a, st_b, ub_s, y_s, h_ref, *, tc, bsz):
    rows = tc * bsz
    half = SSM_PAIRS // 2
    pairs_per_slab = LANES // (2 * SSM_GROUP)
    n_slabs = SSM_WIDTH // LANES
    splits = 2
    n_slices = half * splits
    steps = tc // n_slices
    srows = rows // splits

    @pl.when(pl.program_id(0) == 0)
    def _():
        h_ref[...] = jnp.zeros_like(h_ref)

    for c in range(n_slabs):
        cols = slice(c * LANES, (c + 1) * LANES)
        for b in range(bsz):
            y_s[c, pl.ds(b, tc, stride=bsz), :] = u_ref[b, :, cols]
        u = y_s[c]
        ub_s[c] = u.astype(BF16)
        y_s[c] = dskip_ref[:, cols] * u

    def project_in(buf, j, jl, rs):
        bu = jnp.dot(ub_s[j // pairs_per_slab, rs, :], bm_ref[j], preferred_element_type=F32)
        buf[0, jl, rs, :] = bu[:, :LANES]
        buf[1, jl, rs, :] = bu[:, LANES:]

    def project_out(buf, j, jl, rs):
        st = jnp.concatenate([buf[0, jl, rs, :].astype(BF16), buf[1, jl, rs, :].astype(BF16)],
                             axis=1)
        y_s[j // pairs_per_slab, rs, :] += jnp.dot(st, cm_ref[j], preferred_element_type=F32)

    def scan(buf, j0, other):
        a_re = [are_ref[j0 + i] for i in range(half)]
        a_im = [aim_ref[j0 + i] for i in range(half)]

        def body(it, carry):
            h_re, h_im = carry
            for k in range(steps):
                r0 = pl.multiple_of((it * steps + k) * bsz, bsz)
                new_re, new_im = [], []
                for i in range(half):
                    n_re = a_re[i] * h_re[i] - a_im[i] * h_im[i] + buf[0, i, pl.ds(r0, bsz), :]
                    n_im = a_re[i] * h_im[i] + a_im[i] * h_re[i] + buf[1, i, pl.ds(r0, bsz), :]
                    buf[0, i, pl.ds(r0, bsz), :] = n_re
                    buf[1, i, pl.ds(r0, bsz), :] = n_im
                    new_re.append(n_re)
                    new_im.append(n_im)
                h_re, h_im = tuple(new_re), tuple(new_im)
            other(it // splits, pl.ds((it % splits) * srows, srows))
            return h_re, h_im

        carry = (tuple(h_ref[0, j0 + i] for i in range(half)),
                 tuple(h_ref[1, j0 + i] for i in range(half)))
        for it in range(n_slices):
            carry = body(it, carry)
        h_re, h_im = carry
        for i in range(half):
            h_ref[0, j0 + i] = h_re[i]
            h_ref[1, j0 + i] = h_im[i]

    for jl in range(half):
        project_in(st_a, jl, jl, slice(None))
    scan(st_a, 0, lambda jl, rs: project_in(st_b, half + jl, jl, rs))
    scan(st_b, half, lambda jl, rs: project_out(st_a, jl, jl, rs))
    for jl in range(half):
        project_out(st_b, half + jl, jl, slice(None))

    for c in range(n_slabs):
        for b in range(bsz):
            y = y_s[c, pl.ds(b, tc, stride=bsz), :]
            o_ref[b, :, c * LANES:(c + 1) * LANES] = _gelu_tanh(y).astype(o_ref.dtype)


def _ssm(u, bm, a_re, a_im, cm, d_skip, tc=128):
    bsz, s, _ = u.shape
    rows = tc * bsz
    half = SSM_PAIRS // 2
    chunk_spec = pl.BlockSpec((bsz, tc, SSM_WIDTH), lambda i: (0, i, 0))
    return pl.pallas_call(
        functools.partial(_ssm_kernel, tc=tc, bsz=bsz),
        grid=(s // tc,),
        in_specs=[chunk_spec,
                  _const_spec(bm.shape), _const_spec(a_re.shape), _const_spec(a_im.shape),
                  _const_spec(cm.shape), _const_spec(d_skip.shape)],
        out_specs=chunk_spec,
        out_shape=jax.ShapeDtypeStruct((bsz, s, SSM_WIDTH), BF16),
        scratch_shapes=[pltpu.VMEM((2, half, rows, LANES), F32),
                        pltpu.VMEM((2, half, rows, LANES), F32),
                        pltpu.VMEM((SSM_WIDTH // LANES, rows, LANES), BF16),
                        pltpu.VMEM((SSM_WIDTH // LANES, rows, LANES), F32),
                        pltpu.VMEM((2, SSM_PAIRS, bsz, LANES), F32)],
        compiler_params=pltpu.CompilerParams(
            dimension_semantics=("arbitrary",), vmem_limit_bytes=VMEM_LIMIT),
        name="ssm",
    )(u, bm, a_re, a_im, cm, d_skip)


def _ssm_matrices(a_re, a_im, log_dt, b_re, b_im, c_re, c_im, bsz):
    g, p, h = SSM_GROUPS, SSM_STATE, SSM_GROUP
    lam_re, lam_im = a_re.astype(F32), a_im.astype(F32)
    dt = jnp.exp(log_dt.astype(F32))[:, None]
    mag = jnp.exp(lam_re * dt)
    abar_re, abar_im = mag * jnp.cos(lam_im * dt), mag * jnp.sin(lam_im * dt)
    num_re, num_im = abar_re - 1.0, abar_im
    den = lam_re * lam_re + lam_im * lam_im
    coef_re = ((num_re * lam_re + num_im * lam_im) / den)[..., None]
    coef_im = ((num_im * lam_re - num_re * lam_im) / den)[..., None]
    b_re, b_im = b_re.astype(F32), b_im.astype(F32)
    bbar_re = coef_re * b_re - coef_im * b_im
    bbar_im = coef_re * b_im + coef_im * b_re

    groups_per_slab = LANES // h
    pairs_per_slab = groups_per_slab // 2
    n_slabs = g // groups_per_slab
    eye = jnp.eye(groups_per_slab, dtype=F32)

    def in_matrix(bpart):
        bt = jnp.transpose(bpart, (0, 2, 1)).reshape(n_slabs, groups_per_slab, h, p)
        dense = bt[:, :, :, None, :] * eye[None, :, None, :, None]
        dense = dense.reshape(n_slabs, LANES, pairs_per_slab, 2 * p)
        return jnp.transpose(dense, (0, 2, 1, 3)).reshape(SSM_PAIRS, LANES, 2 * p)

    def out_matrix(cpart):
        ct = jnp.transpose(cpart, (0, 2, 1)).reshape(n_slabs, groups_per_slab, p, h)
        dense = ct[:, :, :, None, :] * eye[None, :, None, :, None]
        return dense.reshape(SSM_PAIRS, 2 * p, LANES)

    bm = jnp.concatenate([in_matrix(bbar_re), in_matrix(bbar_im)], axis=2)
    cm = jnp.concatenate([out_matrix(c_re.astype(F32)), out_matrix(-c_im.astype(F32))], axis=1)

    def bcast(apart):
        return jnp.broadcast_to(apart.reshape(SSM_PAIRS, 1, 2 * p), (SSM_PAIRS, bsz, 2 * p))

    return bm.astype(BF16), bcast(abar_re), bcast(abar_im), cm.astype(BF16)


def _layer_norm(z, g, b):
    mu = jnp.mean(z, axis=-1, keepdims=True)
    zc = z - mu
    var = jnp.mean(zc * zc, axis=-1, keepdims=True)
    return zc * lax.rsqrt(var + LN_EPS) * g + b


def _mix_kernel(*refs, alpha, sub, fine, n_side):
    (x_ref, attn_ref, g_ref, wga_ref, wgs_ref, bgate_ref, wattn_ref, wglu_ref,
     wssm_ref, wout_ref, lng_ref, lnb_ref) = refs[:12]
    o_ref = refs[12 + n_side]
    _cast_slices(refs[12:12 + n_side], refs[13 + n_side:])
    dm = x_ref.shape[1]
    tiles = [slice(r0, r0 + sub) for r0 in range(0, x_ref.shape[0], sub)]
    dot = functools.partial(jnp.dot, preferred_element_type=F32)
    y_s = [None] * len(tiles)
    for t, rows in enumerate(tiles):
        glu = dot(g_ref[rows, :], wglu_ref[...])
        y_s[t] = (glu[:, :SSM_WIDTH] * jax.nn.sigmoid(glu[:, SSM_WIDTH:])).astype(BF16)
    gated_ssm = [None] * len(tiles)
    gate_attn = [None] * len(tiles)
    for t, rows in enumerate(tiles):
        xb = x_ref[rows, :].astype(BF16)
        gate_attn[t] = jax.nn.sigmoid(dot(xb, wga_ref[...]) + bgate_ref[:, :dm])
        gate_ssm = jax.nn.sigmoid(dot(xb, wgs_ref[...]) + bgate_ref[:, dm:])
        gated_ssm[t] = gate_ssm * dot(y_s[t], wssm_ref[...])
    mixed = [None] * len(tiles)
    for t, rows in enumerate(tiles):
        y_attn = dot(attn_ref[rows, :], wattn_ref[...])
        mixed[t] = (gate_attn[t] * y_attn + gated_ssm[t]).astype(BF16)
    for t, rows in enumerate(tiles):
        for f0 in range(0, sub, fine):
            r = slice(rows.start + f0, rows.start + f0 + fine)
            mix_out = dot(mixed[t][f0:f0 + fine], wout_ref[...])
            o_ref[r, :] = _layer_norm(alpha * x_ref[r, :] + mix_out, lng_ref[...], lnb_ref[...])


def _mix(x2, attn2, g2, consts, side, alpha, tm=1024, sub=512, fine=256):
    rows, dm = x2.shape
    row_spec = lambda w: pl.BlockSpec((tm, w), lambda i: (i, 0))
    side_in, side_out, side_shapes = _cast_slice_specs(side, rows // tm, lambda i: i)
    outs = pl.pallas_call(
        functools.partial(_mix_kernel, alpha=alpha, sub=sub, fine=fine, n_side=len(side)),
        grid=(rows // tm,),
        in_specs=[row_spec(dm), row_spec(ATTN_WIDTH), row_spec(SSM_WIDTH)]
                 + [_const_spec(c.shape) for c in consts] + side_in,
        out_specs=[row_spec(dm)] + side_out,
        out_shape=[jax.ShapeDtypeStruct((rows, dm), F32)] + side_shapes,
        compiler_params=pltpu.CompilerParams(
            dimension_semantics=("parallel",), vmem_limit_bytes=VMEM_LIMIT),
        name="mix",
    )(x2, attn2, g2, *consts, *[arr for arr, _ in side])
    return outs[0], outs[1:]


def _ffn_kernel(h_ref, wg_ref, wu_ref, wd_ref, lng_ref, lnb_ref, o_ref, *, alpha, sub, fine):
    tiles = [slice(r0, r0 + sub) for r0 in range(0, h_ref.shape[0], sub)]
    dot = functools.partial(jnp.dot, preferred_element_type=F32)
    act = [None] * len(tiles)
    for t, rows in enumerate(tiles):
        hb = h_ref[rows, :].astype(BF16)
        gate = dot(hb, wg_ref[...])
        up = dot(hb, wu_ref[...])
        act[t] = ((gate * jax.nn.sigmoid(gate)) * up).astype(BF16)
    for t, rows in enumerate(tiles):
        for f0 in range(0, sub, fine):
            r = slice(rows.start + f0, rows.start + f0 + fine)
            ff = dot(act[t][f0:f0 + fine], wd_ref[...])
            o_ref[r, :] = _layer_norm(alpha * h_ref[r, :] + ff, lng_ref[...], lnb_ref[...])


def _ffn(h2, wg, wu, wd, lng, lnb, alpha, tm=1024, sub=512, fine=256):
    rows, dm = h2.shape
    row_spec = pl.BlockSpec((tm, dm), lambda i: (i, 0))
    consts = [wg, wu, wd, lng, lnb]
    return pl.pallas_call(
        functools.partial(_ffn_kernel, alpha=alpha, sub=sub, fine=fine),
        grid=(rows // tm,),
        in_specs=[row_spec] + [_const_spec(c.shape) for c in consts],
        out_specs=row_spec,
        out_shape=jax.ShapeDtypeStruct((rows, dm), F32),
        compiler_params=pltpu.CompilerParams(
            dimension_semantics=("parallel",), vmem_limit_bytes=VMEM_LIMIT),
        name="ffn",
    )(h2, *consts)


def _rope_tables(s):
    half = HEAD_DIM // 2
    f32 = np.float32
    pos = np.arange(s, dtype=f32)
    inv_freq = f32(ROPE_THETA) ** (-np.arange(half, dtype=f32) / f32(half))
    ang = pos[:, None] * inv_freq[None, :]
    cos, sin = np.cos(ang), np.sin(ang)
    reps = LANES // HEAD_DIM
    cos_t = np.concatenate([cos, cos] * reps, axis=1)
    sin_t = np.concatenate([-sin, sin] * reps, axis=1)
    return cos_t.astype(f32), sin_t.astype(f32)


def kernel(x, w_in, b_gate, w_attn_br, w_ssm_br, w_out, ssm_a_re, ssm_a_im, ssm_log_dt, ssm_b_re, ssm_b_im, ssm_c_re, ssm_c_im, ssm_d, w_glu, ln1_g, ln1_b, w_ff_gate, w_ff_up, w_ff_down, ln2_g, ln2_b):
    bsz, s, dm = x.shape
    depth = w_in.shape[0]
    alpha = (2.0 * depth) ** 0.25
    assert all(w // d == ATTN_BLOCK for w, d in zip(WINDOWS, DILATIONS))
    assert s % (ATTN_BLOCK * max(DILATIONS)) == 0
    cos_t, sin_t = _rope_tables(s)
    assert w_in.shape[2] == 3 * QKV_WIDTH + SSM_WIDTH + 2 * dm

    gate_col0 = (3 * QKV_WIDTH + SSM_WIDTH) // dm

    for layer in range(depth):
        outs, (wga, wgs, wattn, wglu, wssm, wout) = _project(
            x, cos_t, sin_t, w_in[layer],
            [(w_in[layer], (gate_col0, dm)), (w_in[layer], (gate_col0 + 1, dm)),
             (w_attn_br[layer], None), (w_glu[layer], None), (w_ssm_br[layer], None),
             (w_out[layer], None)])
        attn = _attention(outs[:N_GROUPS], bsz, s)

        bm, a_re, a_im, cm = _ssm_matrices(
            ssm_a_re[layer], ssm_a_im[layer], ssm_log_dt[layer], ssm_b_re[layer],
            ssm_b_im[layer], ssm_c_re[layer], ssm_c_im[layer], bsz)
        g = _ssm(outs[N_GROUPS], bm, a_re, a_im, cm,
                 ssm_d[layer].reshape(1, SSM_WIDTH).astype(F32))

        h, (wff_gate, wff_up, wff_down) = _mix(
            x.reshape(bsz * s, dm), attn.reshape(bsz * s, ATTN_WIDTH),
            g.reshape(bsz * s, SSM_WIDTH),
            [wga, wgs, b_gate[layer].reshape(1, 2 * dm).astype(F32), wattn, wglu, wssm, wout,
             ln1_g[layer].reshape(1, dm).astype(F32), ln1_b[layer].reshape(1, dm).astype(F32)],
            [(w_ff_gate[layer], None), (w_ff_up[layer], None), (w_ff_down[layer], None)],
            alpha)
        x2 = _ffn(h, wff_gate, wff_up, wff_down,
                  ln2_g[layer].reshape(1, dm).astype(F32), ln2_b[layer].reshape(1, dm).astype(F32),
                  alpha)
        x = x2.reshape(bsz, s, dm)
    return x
```

```python
import functools
import math

import jax
import jax.numpy as jnp
import numpy as np
from jax import lax
from jax.experimental import pallas as pl
from jax.experimental.pallas import tpu as pltpu

F32 = jnp.float32
BF16 = jnp.bfloat16

LANES = 128
HEAD_DIM = 64
ATTN_HEADS = 8
ATTN_WIDTH = ATTN_HEADS * HEAD_DIM
DILATIONS = (1, 4, 16)
WINDOWS = (128, 512, 2048)
N_GROUPS = len(DILATIONS)
QKV_WIDTH = N_GROUPS * ATTN_WIDTH
ATTN_BLOCK = 128
ROPE_THETA = 10000.0
NEG_INF = -1e30
Q_SCALE = math.log2(math.e) / math.sqrt(HEAD_DIM)
SSM_GROUP = 16
SSM_GROUPS = 32
SSM_WIDTH = SSM_GROUP * SSM_GROUPS
SSM_STATE = 64
SSM_PAIRS = SSM_GROUPS // 2
LN_EPS = 1e-5

VMEM_LIMIT = 56 * 1024 * 1024


def _const_spec(shape):
    zeros = (0,) * len(shape)
    return pl.BlockSpec(shape, lambda *_: zeros, pipeline_mode=pl.Buffered(1))


def _cast_slice_specs(side, n_steps, step_of):
    in_specs, out_specs, out_shapes = [], [], []
    for arr, col in side:
        rows = arr.shape[0] // n_steps
        assert rows * n_steps == arr.shape[0] and rows % 16 == 0
        cidx, width = col if col is not None else (0, arr.shape[1])
        in_specs.append(pl.BlockSpec(
            (rows, width), functools.partial(lambda *ids, cidx: (step_of(*ids), cidx), cidx=cidx)))
        out_specs.append(pl.BlockSpec((rows, width), lambda *ids: (step_of(*ids), 0)))
        out_shapes.append(jax.ShapeDtypeStruct((arr.shape[0], width), BF16))
    return in_specs, out_specs, out_shapes


def _cast_slices(src_refs, dst_refs):
    for src, dst in zip(src_refs, dst_refs):
        dst[...] = src[...].astype(dst.dtype)


def _phase_major(ref, tm, d):
    if d == 1:
        return ref[...]
    n = tm // d
    return jnp.concatenate([ref[pl.ds(r, n, stride=d), :] for r in range(d)], axis=0)


def _block_phase_order(ref, tm):
    dmax = max(DILATIONS)
    tile = ATTN_BLOCK // dmax
    return jnp.concatenate([ref[pl.ds(b0 + r, tile, stride=dmax), :]
                            for b0 in range(0, tm, ATTN_BLOCK) for r in range(dmax)], axis=0)


def _proj_kernel(*refs, tm, n_slabs, n_side):
    refs = list(refs)
    x_refs = [refs.pop(0) for _ in range(n_slabs)]
    cos_ref, sin_ref, w32_ref = [refs.pop(0) for _ in range(3)]
    side_in = [refs.pop(0) for _ in range(n_side)]
    o_refs = [refs.pop(0) for _ in range(N_GROUPS)]
    u_ref = refs.pop(0)
    side_out = [refs.pop(0) for _ in range(n_side)]
    w_ref, = refs

    @pl.when((pl.program_id(0) == 0) & (pl.program_id(1) == 0))
    def _():
        for c0 in range(0, w_ref.shape[1], ATTN_WIDTH):
            w_ref[:, c0:c0 + ATTN_WIDTH] = w32_ref[:, c0:c0 + ATTN_WIDTH].astype(BF16)

    _cast_slices(side_in, side_out)

    lane = lax.broadcasted_iota(jnp.int32, (1, LANES), 1)
    first_half = (lane % HEAD_DIM) < (HEAD_DIM // 2)

    for gi, d in enumerate(DILATIONS):
        n = tm // d
        xg = jnp.concatenate(
            [_phase_major(xr, tm, d).astype(BF16) for xr in x_refs], axis=1)
        cos = _phase_major(cos_ref, tm, d)
        sin = _phase_major(sin_ref, tm, d)
        if gi == 0:
            u_ref[...] = jnp.dot(xg, w_ref[:, 3 * QKV_WIDTH:], preferred_element_type=F32)
        for part in range(3):
            c0 = part * ATTN_WIDTH
            w0 = part * QKV_WIDTH + gi * ATTN_WIDTH
            lhs, rope_cos, rope_sin = xg, cos, sin
            if gi == 0 and part == 0:
                lhs = jnp.concatenate(
                    [_block_phase_order(xr, tm).astype(BF16) for xr in x_refs], axis=1)
                rope_cos = _block_phase_order(cos_ref, tm)
                rope_sin = _block_phase_order(sin_ref, tm)
            acc = jnp.dot(lhs, w_ref[:, w0:w0 + ATTN_WIDTH], preferred_element_type=F32)
            for c in range(ATTN_WIDTH // LANES):
                t = acc[:, c * LANES:(c + 1) * LANES]
                if part < 2:
                    rot = jnp.where(first_half,
                                    pltpu.roll(t, LANES - HEAD_DIM // 2, 1),
                                    pltpu.roll(t, HEAD_DIM // 2, 1))
                    t = t * rope_cos + rot * rope_sin
                if part == 0:
                    t = t * Q_SCALE
                t = t.astype(BF16)
                col = c0 + c * LANES
                if d == 1:
                    o_refs[gi][:, col:col + LANES] = t
                else:
                    for r in range(d):
                        o_refs[gi][r, :, col:col + LANES] = t[r * n:(r + 1) * n]


def _project(x, cos_t, sin_t, w_in, side, tm=512):
    b, s, dm = x.shape
    n_slabs = dm // LANES
    n_proj = 3 * QKV_WIDTH + SSM_WIDTH
    grid = (b, s // tm)
    side_in, side_out, side_shapes = _cast_slice_specs(
        side, b * (s // tm), lambda bi, i: bi * (s // tm) + i)
    x_specs = [pl.BlockSpec((None, tm, LANES), functools.partial(lambda bi, i, c: (bi, i, c), c=c))
               for c in range(n_slabs)]
    tab_spec = pl.BlockSpec((tm, LANES), lambda bi, i: (i, 0))
    out_shapes = []
    out_specs = []
    for d in DILATIONS:
        if d == 1:
            out_shapes.append(jax.ShapeDtypeStruct((b, s, QKV_WIDTH), BF16))
            out_specs.append(pl.BlockSpec((None, tm, QKV_WIDTH), lambda bi, i: (bi, i, 0)))
        else:
            out_shapes.append(jax.ShapeDtypeStruct((b, d, s // d, QKV_WIDTH), BF16))
            out_specs.append(pl.BlockSpec((None, d, tm // d, QKV_WIDTH),
                                          lambda bi, i: (bi, 0, i, 0)))
    out_shapes.append(jax.ShapeDtypeStruct((b, s, SSM_WIDTH), F32))
    out_specs.append(pl.BlockSpec((None, tm, SSM_WIDTH), lambda bi, i: (bi, i, 0)))
    outs = pl.pallas_call(
        functools.partial(_proj_kernel, tm=tm, n_slabs=n_slabs, n_side=len(side)),
        grid=grid,
        in_specs=x_specs + [tab_spec, tab_spec,
                            pl.BlockSpec((dm, n_proj), lambda bi, i: (0, 0),
                                         pipeline_mode=pl.Buffered(1))] + side_in,
        out_specs=out_specs + side_out,
        out_shape=out_shapes + side_shapes,
        scratch_shapes=[pltpu.VMEM((dm, n_proj), BF16)],
        compiler_params=pltpu.CompilerParams(
            dimension_semantics=("arbitrary", "arbitrary"), vmem_limit_bytes=VMEM_LIMIT),
        name="in_proj",
    )(*([x] * n_slabs), cos_t, sin_t, w_in, *[arr for arr, _ in side])
    n_main = N_GROUPS + 1
    return outs[:n_main], outs[n_main:]


def _attn_bias():
    blk = ATTN_BLOCK
    c = np.arange(2 * blk)[None, :]

    def tables(position):
        a = np.tile(position, 2)[:, None]
        in_window = np.where(c < blk, c >= a, c - blk <= a)
        windowed = np.where(in_window, 0.0, NEG_INF).astype(np.float32)
        causal = np.where(c[:, :blk] <= a, 0.0, NEG_INF).astype(np.float32)
        return windowed, causal

    return tables(np.arange(blk)), tables(_PHASE_ORDER)


_PHASE_ORDER = np.arange(ATTN_BLOCK).reshape(-1, max(DILATIONS)).T.reshape(-1)


def _attn_kernel(*refs, s, lags):
    qkv = refs[:3 * N_GROUPS]
    mask_w_ref, mask_c_ref, pmask_w_ref, pmask_c_ref = refs[3 * N_GROUPS:3 * N_GROUPS + 4]
    o_ref = refs[3 * N_GROUPS + 4]
    acc_s, m_s, l_s, merged_s = refs[3 * N_GROUPS + 5:]
    blk = ATTN_BLOCK
    n_blocks = s // blk
    dmax = max(DILATIONS)
    tile = blk // dmax

    lane = lax.broadcasted_iota(jnp.int32, (1, LANES), 1)
    head_a = lane < HEAD_DIM

    def window(ref, i, nblk):
        start = i if i % nblk == 0 else i - 1
        return ref[start * blk:(i + 1) * blk, :]

    def scores(gi, i, nblk):
        q_ref, k_ref, _ = qkv[3 * gi:3 * gi + 3]
        q = q_ref[i * blk:(i + 1) * blk, :]
        if gi == 0:
            mask = pmask_c_ref[...] if i % nblk == 0 else pmask_w_ref[...]
        else:
            mask = mask_c_ref[...] if i % nblk == 0 else mask_w_ref[...]
        zero = jnp.zeros_like(q)
        q2 = jnp.concatenate([jnp.where(head_a, q, zero), jnp.where(head_a, zero, q)], axis=0)
        return lax.dot_general(q2, window(k_ref, i, nblk), (((1,), (1,)), ((), ())),
                               preferred_element_type=F32) + mask

    def softmax(sc):
        m = jnp.max(sc, axis=1, keepdims=True)
        return m, jnp.exp2((sc - m).astype(BF16))

    def finish(gi, d, i, nblk, m, p):
        vw = window(qkv[3 * gi + 2], i, nblk)
        v_ext = jnp.concatenate([vw, jnp.ones_like(vw)], axis=1)
        res = jnp.dot(p, v_ext, preferred_element_type=F32)
        acc = jnp.where(head_a, res[:blk, :LANES], res[blk:, :LANES])
        l = jnp.where(head_a, res[:blk, LANES:], res[blk:, LANES:])
        m = jnp.where(head_a, m[:blk], m[blk:])
        rows = slice(i * blk, (i + 1) * blk)
        if d > 1:
            acc_s[gi - 1, rows, :] = acc
            l_s[gi - 1, rows, :] = l
            m_s[gi - 1, rows, :] = m
            return

        def parked(ref, g):
            dg = DILATIONS[g + 1]
            tiles = []
            for r in range(dmax):
                start = (r % dg) * (s // dg) + (blk // dg) * i + r // dg
                idx = pl.ds(start, tile, stride=dmax // dg) if dg < dmax else pl.ds(start, tile)
                tiles.append(ref[g, idx, :])
            return jnp.concatenate(tiles, axis=0)

        parts = [(m, l, acc)] + [(parked(m_s, g), parked(l_s, g), parked(acc_s, g))
                                 for g in range(N_GROUPS - 1)]
        m_all = functools.reduce(jnp.maximum, [pm for pm, _, _ in parts])
        num = jnp.zeros((blk, LANES), F32)
        den = jnp.zeros((blk, LANES), F32)
        for pm, pl_, pacc in parts:
            wgt = jnp.exp2(pm - m_all)
            num = num + wgt * pacc
            den = den + wgt * pl_
        merged_s[i] = num / den
        natural = [merged_s[i, pl.ds((a0 % dmax) * tile + a0 // dmax, tile, stride=tile), :]
                   for a0 in range(0, blk, tile)]
        o_ref[rows, :] = jnp.concatenate(natural, axis=0).astype(o_ref.dtype)

    assert DILATIONS[0] == 1
    block = lambda gi, i: (gi, DILATIONS[gi], i, (s // DILATIONS[gi]) // blk)
    ready = [0] * n_blocks
    dilated = []
    for gi in reversed(range(1, N_GROUPS)):
        nblk = (s // DILATIONS[gi]) // blk
        order = sorted(range(n_blocks), key=lambda b: (b % nblk, b // nblk))
        for pos, b in enumerate(order):
            dilated.append(block(gi, b))
            for i in range(n_blocks):
                if i * nblk // n_blocks == b % nblk:
                    ready[i] = max(ready[i], len(dilated))
    work, done = [], 0
    for i in range(n_blocks):
        take = max(ready[i] - done, 1 if done < len(dilated) else 0)
        work += dilated[done:done + take]
        done += take
        work.append(block(0, i))
    work += dilated[done:]
    lag_softmax, lag_values = lags
    sc_q, p_q = {}, {}
    for step in range(len(work) + lag_values):
        if step < len(work):
            gi, d, i, nblk = work[step]
            sc_q[step] = scores(gi, i, nblk)
        if 0 <= step - lag_softmax < len(work):
            p_q[step - lag_softmax] = softmax(sc_q.pop(step - lag_softmax))
        if step >= lag_values:
            gi, d, i, nblk = work[step - lag_values]
            finish(gi, d, i, nblk, *p_q.pop(step - lag_values))


def _attention(qkv_groups, b, s, lags=(1, 4)):
    n_hp = ATTN_WIDTH // LANES
    in_specs, args = [], []
    for gi, d in enumerate(DILATIONS):
        arr = qkv_groups[gi].reshape(b, s, QKV_WIDTH)
        for part in range(3):
            in_specs.append(pl.BlockSpec(
                (None, s, LANES),
                functools.partial(lambda bi, hp, part: (bi, 0, part * n_hp + hp), part=part)))
            args.append(arr)
    (mask_w, mask_c), (pmask_w, pmask_c) = _attn_bias()
    masks = [mask_w, mask_c, pmask_w, pmask_c]
    return pl.pallas_call(
        functools.partial(_attn_kernel, s=s, lags=lags),
        grid=(b, n_hp),
        in_specs=in_specs + [_const_spec(t.shape) for t in masks],
        out_specs=pl.BlockSpec((None, s, LANES), lambda bi, hp: (bi, 0, hp)),
        out_shape=jax.ShapeDtypeStruct((b, s, ATTN_WIDTH), BF16),
        scratch_shapes=[pltpu.VMEM((N_GROUPS - 1, s, LANES), F32)] * 3
                       + [pltpu.VMEM((s // ATTN_BLOCK, ATTN_BLOCK, LANES), F32)],
        compiler_params=pltpu.CompilerParams(
            dimension_semantics=("parallel", "parallel"), vmem_limit_bytes=VMEM_LIMIT),
        name="attention",
    )(*args, *masks)


def _gelu_tanh(x):
    return 0.5 * x * (1.0 + jnp.tanh(math.sqrt(2.0 / math.pi) * (x + 0.044715 * (x * x * x))))


def _ssm_kernel(u_ref, bm_ref, are_ref, aim_ref, cm_ref, dskip_ref, o_ref,
                st_a, st_b, ub_s, y_s, h_ref, *, tc, bsz):
    rows = tc * bsz
    half = SSM_PAIRS // 2
    pairs_per_slab = LANES // (2 * SSM_GROUP)
    n_slabs = SSM_WIDTH // LANES
    splits = 2
    n_slices = half * splits
    steps = tc // n_slices
    srows = rows // splits

    @pl.when(pl.program_id(0) == 0)
    def _():
        h_ref[...] = jnp.zeros_like(h_ref)

    for c in range(n_slabs):
        cols = slice(c * LANES, (c + 1) * LANES)
        for b in range(bsz):
            y_s[c, pl.ds(b, tc, stride=bsz), :] = u_ref[b, :, cols]
        u = y_s[c]
        ub_s[c] = u.astype(BF16)
        y_s[c] = dskip_ref[:, cols] * u

    def project_in(buf, j, jl, rs):
        bu = jnp.dot(ub_s[j // pairs_per_slab, rs, :], bm_ref[j], preferred_element_type=F32)
        buf[0, jl, rs, :] = bu[:, :LANES]
        buf[1, jl, rs, :] = bu[:, LANES:]

    def project_out(buf, j, jl, rs):
        st = jnp.concatenate([buf[0, jl, rs, :].astype(BF16), buf[1, jl, rs, :].astype(BF16)],
                             axis=1)
        y_s[j // pairs_per_slab, rs, :] += jnp.dot(st, cm_ref[j], preferred_element_type=F32)

    def scan(buf, j0, other):
        a_re = [are_ref[j0 + i] for i in range(half)]
        a_im = [aim_ref[j0 + i] for i in range(half)]

        def body(it, carry):
            h_re, h_im = carry
            for k in range(steps):
                r0 = pl.multiple_of((it * steps + k) * bsz, bsz)
                new_re, new_im = [], []
                for i in range(half):
                    n_re = a_re[i] * h_re[i] - a_im[i] * h_im[i] + buf[0, i, pl.ds(r0, bsz), :]
                    n_im = a_re[i] * h_im[i] + a_im[i] * h_re[i] + buf[1, i, pl.ds(r0, bsz), :]
                    buf[0, i, pl.ds(r0, bsz), :] = n_re
                    buf[1, i, pl.ds(r0, bsz), :] = n_im
                    new_re.append(n_re)
                    new_im.append(n_im)
                h_re, h_im = tuple(new_re), tuple(new_im)
            other(it // splits, pl.ds((it % splits) * srows, srows))
            return h_re, h_im

        carry = (tuple(h_ref[0, j0 + i] for i in range(half)),
                 tuple(h_ref[1, j0 + i] for i in range(half)))
        for it in range(n_slices):
            carry = body(it, carry)
        h_re, h_im = carry
        for i in range(half):
            h_ref[0, j0 + i] = h_re[i]
            h_ref[1, j0 + i] = h_im[i]

    for jl in range(half):
        project_in(st_a, jl, jl, slice(None))
    scan(st_a, 0, lambda jl, rs: project_in(st_b, half + jl, jl, rs))
    scan(st_b, half, lambda jl, rs: project_out(st_a, jl, jl, rs))
    for jl in range(half):
        project_out(st_b, half + jl, jl, slice(None))

    for c in range(n_slabs):
        for b in range(bsz):
            y = y_s[c, pl.ds(b, tc, stride=bsz), :]
            o_ref[b, :, c * LANES:(c + 1) * LANES] = _gelu_tanh(y).astype(o_ref.dtype)


def _ssm(u, bm, a_re, a_im, cm, d_skip, tc=128):
    bsz, s, _ = u.shape
    rows = tc * bsz
    half = SSM_PAIRS // 2
    chunk_spec = pl.BlockSpec((bsz, tc, SSM_WIDTH), lambda i: (0, i, 0))
    return pl.pallas_call(
        functools.partial(_ssm_kernel, tc=tc, bsz=bsz),
        grid=(s // tc,),
        in_specs=[chunk_spec,
                  _const_spec(bm.shape), _const_spec(a_re.shape), _const_spec(a_im.shape),
                  _const_spec(cm.shape), _const_spec(d_skip.shape)],
        out_specs=chunk_spec,
        out_shape=jax.ShapeDtypeStruct((bsz, s, SSM_WIDTH), BF16),
        scratch_shapes=[pltpu.VMEM((2, half, rows, LANES), F32),
                        pltpu.VMEM((2, half, rows, LANES), F32),
                        pltpu.VMEM((SSM_WIDTH // LANES, rows, LANES), BF16),
                        pltpu.VMEM((SSM_WIDTH // LANES, rows, LANES), F32),
                        pltpu.VMEM((2, SSM_PAIRS, bsz, LANES), F32)],
        compiler_params=pltpu.CompilerParams(
            dimension_semantics=("arbitrary",), vmem_limit_bytes=VMEM_LIMIT),
        name="ssm",
    )(u, bm, a_re, a_im, cm, d_skip)


def _ssm_matrices(a_re, a_im, log_dt, b_re, b_im, c_re, c_im, bsz):
    g, p, h = SSM_GROUPS, SSM_STATE, SSM_GROUP
    lam_re, lam_im = a_re.astype(F32), a_im.astype(F32)
    dt = jnp.exp(log_dt.astype(F32))[:, None]
    mag = jnp.exp(lam_re * dt)
    abar_re, abar_im = mag * jnp.cos(lam_im * dt), mag * jnp.sin(lam_im * dt)
    num_re, num_im = abar_re - 1.0, abar_im
    den = lam_re * lam_re + lam_im * lam_im
    coef_re = ((num_re * lam_re + num_im * lam_im) / den)[..., None]
    coef_im = ((num_im * lam_re - num_re * lam_im) / den)[..., None]
    b_re, b_im = b_re.astype(F32), b_im.astype(F32)
    bbar_re = coef_re * b_re - coef_im * b_im
    bbar_im = coef_re * b_im + coef_im * b_re

    groups_per_slab = LANES // h
    pairs_per_slab = groups_per_slab // 2
    n_slabs = g // groups_per_slab
    eye = jnp.eye(groups_per_slab, dtype=F32)

    def in_matrix(bpart):
        bt = jnp.transpose(bpart, (0, 2, 1)).reshape(n_slabs, groups_per_slab, h, p)
        dense = bt[:, :, :, None, :] * eye[None, :, None, :, None]
        dense = dense.reshape(n_slabs, LANES, pairs_per_slab, 2 * p)
        return jnp.transpose(dense, (0, 2, 1, 3)).reshape(SSM_PAIRS, LANES, 2 * p)

    def out_matrix(cpart):
        ct = jnp.transpose(cpart, (0, 2, 1)).reshape(n_slabs, groups_per_slab, p, h)
        dense = ct[:, :, :, None, :] * eye[None, :, None, :, None]
        return dense.reshape(SSM_PAIRS, 2 * p, LANES)

    bm = jnp.concatenate([in_matrix(bbar_re), in_matrix(bbar_im)], axis=2)
    cm = jnp.concatenate([out_matrix(c_re.astype(F32)), out_matrix(-c_im.astype(F32))], axis=1)

    def bcast(apart):
        return jnp.broadcast_to(apart.reshape(SSM_PAIRS, 1, 2 * p), (SSM_PAIRS, bsz, 2 * p))

    return bm.astype(BF16), bcast(abar_re), bcast(abar_im), cm.astype(BF16)


def _layer_norm(z, g, b):
    mu = jnp.mean(z, axis=-1, keepdims=True)
    zc = z - mu
    var = jnp.mean(zc * zc, axis=-1, keepdims=True)
    return zc * lax.rsqrt(var + LN_EPS) * g + b


def _mix_kernel(*refs, alpha, sub, fine, n_side):
    (x_ref, attn_ref, g_ref, wga_ref, wgs_ref, bgate_ref, wattn_ref, wglu_ref,
     wssm_ref, wout_ref, lng_ref, lnb_ref) = refs[:12]
    o_ref = refs[12 + n_side]
    _cast_slices(refs[12:12 + n_side], refs[13 + n_side:])
    dm = x_ref.shape[1]
    tiles = [slice(r0, r0 + sub) for r0 in range(0, x_ref.shape[0], sub)]
    dot = functools.partial(jnp.dot, preferred_element_type=F32)
    y_s = [None] * len(tiles)
    for t, rows in enumerate(tiles):
        glu = dot(g_ref[rows, :], wglu_ref[...])
        y_s[t] = (glu[:, :SSM_WIDTH] * jax.nn.sigmoid(glu[:, SSM_WIDTH:])).astype(BF16)
    gated_ssm = [None] * len(tiles)
    gate_attn = [None] * len(tiles)
    for t, rows in enumerate(tiles):
        xb = x_ref[rows, :].astype(BF16)
        gate_attn[t] = jax.nn.sigmoid(dot(xb, wga_ref[...]) + bgate_ref[:, :dm])
        gate_ssm = jax.nn.sigmoid(dot(xb, wgs_ref[...]) + bgate_ref[:, dm:])
        gated_ssm[t] = gate_ssm * dot(y_s[t], wssm_ref[...])
    mixed = [None] * len(tiles)
    for t, rows in enumerate(tiles):
        y_attn = dot(attn_ref[rows, :], wattn_ref[...])
        mixed[t] = (gate_attn[t] * y_attn + gated_ssm[t]).astype(BF16)
    for t, rows in enumerate(tiles):
        for f0 in range(0, sub, fine):
            r = slice(rows.start + f0, rows.start + f0 + fine)
            mix_out = dot(mixed[t][f0:f0 + fine], wout_ref[...])
            o_ref[r, :] = _layer_norm(alpha * x_ref[r, :] + mix_out, lng_ref[...], lnb_ref[...])


def _mix(x2, attn2, g2, consts, side, alpha, tm=1024, sub=512, fine=256):
    rows, dm = x2.shape
    row_spec = lambda w: pl.BlockSpec((tm, w), lambda i: (i, 0))
    side_in, side_out, side_shapes = _cast_slice_specs(side, rows // tm, lambda i: i)
    outs = pl.pallas_call(
        functools.partial(_mix_kernel, alpha=alpha, sub=sub, fine=fine, n_side=len(side)),
        grid=(rows // tm,),
        in_specs=[row_spec(dm), row_spec(ATTN_WIDTH), row_spec(SSM_WIDTH)]
                 + [_const_spec(c.shape) for c in consts] + side_in,
        out_specs=[row_spec(dm)] + side_out,
        out_shape=[jax.ShapeDtypeStruct((rows, dm), F32)] + side_shapes,
        compiler_params=pltpu.CompilerParams(
            dimension_semantics=("parallel",), vmem_limit_bytes=VMEM_LIMIT),
        name="mix",
    )(x2, attn2, g2, *consts, *[arr for arr, _ in side])
    return outs[0], outs[1:]


def _ffn_kernel(h_ref, wg_ref, wu_ref, wd_ref, lng_ref, lnb_ref, o_ref, *, alpha, sub, fine):
    tiles = [slice(r0, r0 + sub) for r0 in range(0, h_ref.shape[0], sub)]
    dot = functools.partial(jnp.dot, preferred_element_type=F32)
    act = [None] * len(tiles)
    for t, rows in enumerate(tiles):
        hb = h_ref[rows, :].astype(BF16)
        gate = dot(hb, wg_ref[...])
        up = dot(hb, wu_ref[...])
        act[t] = ((gate * jax.nn.sigmoid(gate)) * up).astype(BF16)
    for t, rows in enumerate(tiles):
        for f0 in range(0, sub, fine):
            r = slice(rows.start + f0, rows.start + f0 + fine)
            ff = dot(act[t][f0:f0 + fine], wd_ref[...])
            o_ref[r, :] = _layer_norm(alpha * h_ref[r, :] + ff, lng_ref[...], lnb_ref[...])


def _ffn(h2, wg, wu, wd, lng, lnb, alpha, tm=1024, sub=512, fine=256):
    rows, dm = h2.shape
    row_spec = pl.BlockSpec((tm, dm), lambda i: (i, 0))
    consts = [wg, wu, wd, lng, lnb]
    return pl.pallas_call(
        functools.partial(_ffn_kernel, alpha=alpha, sub=sub, fine=fine),
        grid=(rows // tm,),
        in_specs=[row_spec] + [_const_spec(c.shape) for c in consts],
        out_specs=row_spec,
        out_shape=jax.ShapeDtypeStruct((rows, dm), F32),
        compiler_params=pltpu.CompilerParams(
            dimension_semantics=("parallel",), vmem_limit_bytes=VMEM_LIMIT),
        name="ffn",
    )(h2, *consts)


def _rope_tables(s):
    half = HEAD_DIM // 2
    f32 = np.float32
    pos = np.arange(s, dtype=f32)
    inv_freq = f32(ROPE_THETA) ** (-np.arange(half, dtype=f32) / f32(half))
    ang = pos[:, None] * inv_freq[None, :]
    cos, sin = np.cos(ang), np.sin(ang)
    reps = LANES // HEAD_DIM
    cos_t = np.concatenate([cos, cos] * reps, axis=1)
    sin_t = np.concatenate([-sin, sin] * reps, axis=1)
    return cos_t.astype(f32), sin_t.astype(f32)


def kernel(x, w_in, b_gate, w_attn_br, w_ssm_br, w_out, ssm_a_re, ssm_a_im, ssm_log_dt, ssm_b_re, ssm_b_im, ssm_c_re, ssm_c_im, ssm_d, w_glu, ln1_g, ln1_b, w_ff_gate, w_ff_up, w_ff_down, ln2_g, ln2_b):
    bsz, s, dm = x.shape
    depth = w_in.shape[0]
    alpha = (2.0 * depth) ** 0.25
    assert all(w // d == ATTN_BLOCK for w, d in zip(WINDOWS, DILATIONS))
    assert s % (ATTN_BLOCK * max(DILATIONS)) == 0
    cos_t, sin_t = _rope_tables(s)
    assert w_in.shape[2] == 3 * QKV_WIDTH + SSM_WIDTH + 2 * dm

    gate_col0 = (3 * QKV_WIDTH + SSM_WIDTH) // dm

    for layer in range(depth):
        outs, (wga, wgs, wattn, wglu, wssm, wout) = _project(
            x, cos_t, sin_t, w_in[layer],
            [(w_in[layer], (gate_col0, dm)), (w_in[layer], (gate_col0 + 1, dm)),
             (w_attn_br[layer], None), (w_glu[layer], None), (w_ssm_br[layer], None),
             (w_out[layer], None)])
        attn = _attention(outs[:N_GROUPS], bsz, s)

        bm, a_re, a_im, cm = _ssm_matrices(
            ssm_a_re[layer], ssm_a_im[layer], ssm_log_dt[layer], ssm_b_re[layer],
            ssm_b_im[layer], ssm_c_re[layer], ssm_c_im[layer], bsz)
        g = _ssm(outs[N_GROUPS], bm, a_re, a_im, cm,
                 ssm_d[layer].reshape(1, SSM_WIDTH).astype(F32))

        h, (wff_gate, wff_up, wff_down) = _mix(
            x.reshape(bsz * s, dm), attn.reshape(bsz * s, ATTN_WIDTH),
            g.reshape(bsz * s, SSM_WIDTH),
            [wga, wgs, b_gate[layer].reshape(1, 2 * dm).astype(F32), wattn, wglu, wssm, wout,
             ln1_g[layer].reshape(1, dm).astype(F32), ln1_b[layer].reshape(1, dm).astype(F32)],
            [(w_ff_gate[layer], None), (w_ff_up[layer], None), (w_ff_down[layer], None)],
            alpha)
        x2 = _ffn(h, wff_gate, wff_up, wff_down,
                  ln2_g[layer].reshape(1, dm).astype(F32), ln2_b[layer].reshape(1, dm).astype(F32),
                  alpha)
        x = x2.reshape(bsz, s, dm)
    return x
```

```python
import functools
import math

import jax
import jax.numpy as jnp
import numpy as np
from jax import lax
from jax.experimental import pallas as pl
from jax.experimental.pallas import tpu as pltpu

F32 = jnp.float32
BF16 = jnp.bfloat16

LANES = 128
HEAD_DIM = 64
ATTN_HEADS = 8
ATTN_WIDTH = ATTN_HEADS * HEAD_DIM
DILATIONS = (1, 4, 16)
WINDOWS = (128, 512, 2048)
N_GROUPS = len(DILATIONS)
QKV_WIDTH = N_GROUPS * ATTN_WIDTH
ATTN_BLOCK = 128
ROPE_THETA = 10000.0
NEG_INF = -1e30
Q_SCALE = math.log2(math.e) / math.sqrt(HEAD_DIM)
SSM_GROUP = 16
SSM_GROUPS = 32
SSM_WIDTH = SSM_GROUP * SSM_GROUPS
SSM_STATE = 64
SSM_PAIRS = SSM_GROUPS // 2
LN_EPS = 1e-5

VMEM_LIMIT = 56 * 1024 * 1024


def _const_spec(shape):
    zeros = (0,) * len(shape)
    return pl.BlockSpec(shape, lambda *_: zeros, pipeline_mode=pl.Buffered(1))


def _cast_slice_specs(side, n_steps, step_of):
    in_specs, out_specs, out_shapes = [], [], []
    for arr, col in side:
        rows = arr.shape[0] // n_steps
        assert rows * n_steps == arr.shape[0] and rows % 16 == 0
        cidx, width = col if col is not None else (0, arr.shape[1])
        in_specs.append(pl.BlockSpec(
            (rows, width), functools.partial(lambda *ids, cidx: (step_of(*ids), cidx), cidx=cidx)))
        out_specs.append(pl.BlockSpec((rows, width), lambda *ids: (step_of(*ids), 0)))
        out_shapes.append(jax.ShapeDtypeStruct((arr.shape[0], width), BF16))
    return in_specs, out_specs, out_shapes


def _cast_slices(src_refs, dst_refs):
    for src, dst in zip(src_refs, dst_refs):
        dst[...] = src[...].astype(dst.dtype)


def _phase_major(ref, tm, d):
    if d == 1:
        return ref[...]
    n = tm // d
    return jnp.concatenate([ref[pl.ds(r, n, stride=d), :] for r in range(d)], axis=0)


def _block_phase_order(ref, tm):
    dmax = max(DILATIONS)
    tile = ATTN_BLOCK // dmax
    return jnp.concatenate([ref[pl.ds(b0 + r, tile, stride=dmax), :]
                            for b0 in range(0, tm, ATTN_BLOCK) for r in range(dmax)], axis=0)


def _proj_kernel(*refs, tm, n_slabs, n_side):
    refs = list(refs)
    x_refs = [refs.pop(0) for _ in range(n_slabs)]
    cos_ref, sin_ref, w32_ref = [refs.pop(0) for _ in range(3)]
    side_in = [refs.pop(0) for _ in range(n_side)]
    o_refs = [refs.pop(0) for _ in range(N_GROUPS)]
    u_ref = refs.pop(0)
    side_out = [refs.pop(0) for _ in range(n_side)]
    w_ref, = refs

    @pl.when((pl.program_id(0) == 0) & (pl.program_id(1) == 0))
    def _():
        for c0 in range(0, w_ref.shape[1], ATTN_WIDTH):
            w_ref[:, c0:c0 + ATTN_WIDTH] = w32_ref[:, c0:c0 + ATTN_WIDTH].astype(BF16)

    _cast_slices(side_in, side_out)

    lane = lax.broadcasted_iota(jnp.int32, (1, LANES), 1)
    first_half = (lane % HEAD_DIM) < (HEAD_DIM // 2)

    for gi, d in enumerate(DILATIONS):
        n = tm // d
        xg = jnp.concatenate(
            [_phase_major(xr, tm, d).astype(BF16) for xr in x_refs], axis=1)
        cos = _phase_major(cos_ref, tm, d)
        sin = _phase_major(sin_ref, tm, d)
        if gi == 0:
            u_ref[...] = jnp.dot(xg, w_ref[:, 3 * QKV_WIDTH:], preferred_element_type=F32)
        for part in range(3):
            c0 = part * ATTN_WIDTH
            w0 = part * QKV_WIDTH + gi * ATTN_WIDTH
            lhs, rope_cos, rope_sin = xg, cos, sin
            if gi == 0 and part == 0:
                lhs = jnp.concatenate(
                    [_block_phase_order(xr, tm).astype(BF16) for xr in x_refs], axis=1)
                rope_cos = _block_phase_order(cos_ref, tm)
                rope_sin = _block_phase_order(sin_ref, tm)
            acc = jnp.dot(lhs, w_ref[:, w0:w0 + ATTN_WIDTH], preferred_element_type=F32)
            for c in range(ATTN_WIDTH // LANES):
                t = acc[:, c * LANES:(c + 1) * LANES]
                if part < 2:
                    rot = jnp.where(first_half,
                                    pltpu.roll(t, LANES - HEAD_DIM // 2, 1),
                                    pltpu.roll(t, HEAD_DIM // 2, 1))
                    t = t * rope_cos + rot * rope_sin
                if part == 0:
                    t = t * Q_SCALE
                t = t.astype(BF16)
                col = c0 + c * LANES
                if d == 1:
                    o_refs[gi][:, col:col + LANES] = t
                else:
                    for r in range(d):
                        o_refs[gi][r, :, col:col + LANES] = t[r * n:(r + 1) * n]


def _project(x, cos_t, sin_t, w_in, side, tm=512):
    b, s, dm = x.shape
    n_slabs = dm // LANES
    n_proj = 3 * QKV_WIDTH + SSM_WIDTH
    grid = (b, s // tm)
    side_in, side_out, side_shapes = _cast_slice_specs(
        side, b * (s // tm), lambda bi, i: bi * (s // tm) + i)
    x_specs = [pl.BlockSpec((None, tm, LANES), functools.partial(lambda bi, i, c: (bi, i, c), c=c))
               for c in range(n_slabs)]
    tab_spec = pl.BlockSpec((tm, LANES), lambda bi, i: (i, 0))
    out_shapes = []
    out_specs = []
    for d in DILATIONS:
        if d == 1:
            out_shapes.append(jax.ShapeDtypeStruct((b, s, QKV_WIDTH), BF16))
            out_specs.append(pl.BlockSpec((None, tm, QKV_WIDTH), lambda bi, i: (bi, i, 0)))
        else:
            out_shapes.append(jax.ShapeDtypeStruct((b, d, s // d, QKV_WIDTH), BF16))
            out_specs.append(pl.BlockSpec((None, d, tm // d, QKV_WIDTH),
                                          lambda bi, i: (bi, 0, i, 0)))
    out_shapes.append(jax.ShapeDtypeStruct((b, s, SSM_WIDTH), F32))
    out_specs.append(pl.BlockSpec((None, tm, SSM_WIDTH), lambda bi, i: (bi, i, 0)))
    outs = pl.pallas_call(
        functools.partial(_proj_kernel, tm=tm, n_slabs=n_slabs, n_side=len(side)),
        grid=grid,
        in_specs=x_specs + [tab_spec, tab_spec,
                            pl.BlockSpec((dm, n_proj), lambda bi, i: (0, 0),
                                         pipeline_mode=pl.Buffered(1))] + side_in,
        out_specs=out_specs + side_out,
        out_shape=out_shapes + side_shapes,
        scratch_shapes=[pltpu.VMEM((dm, n_proj), BF16)],
        compiler_params=pltpu.CompilerParams(
            dimension_semantics=("arbitrary", "arbitrary"), vmem_limit_bytes=VMEM_LIMIT),
        name="in_proj",
    )(*([x] * n_slabs), cos_t, sin_t, w_in, *[arr for arr, _ in side])
    n_main = N_GROUPS + 1
    return outs[:n_main], outs[n_main:]


def _attn_bias():
    blk = ATTN_BLOCK
    c = np.arange(2 * blk)[None, :]

    def tables(position):
        a = np.tile(position, 2)[:, None]
        in_window = np.where(c < blk, c >= a, c - blk <= a)
        windowed = np.where(in_window, 0.0, NEG_INF).astype(np.float32)
        causal = np.where(c[:, :blk] <= a, 0.0, NEG_INF).astype(np.float32)
        return windowed, causal

    return tables(np.arange(blk)), tables(_PHASE_ORDER)


_PHASE_ORDER = np.arange(ATTN_BLOCK).reshape(-1, max(DILATIONS)).T.reshape(-1)


def _attn_kernel(*refs, s, lags):
    qkv = refs[:3 * N_GROUPS]
    mask_w_ref, mask_c_ref, pmask_w_ref, pmask_c_ref = refs[3 * N_GROUPS:3 * N_GROUPS + 4]
    o_ref = refs[3 * N_GROUPS + 4]
    acc_s, m_s, l_s, merged_s = refs[3 * N_GROUPS + 5:]
    blk = ATTN_BLOCK
    n_blocks = s // blk
    dmax = max(DILATIONS)
    tile = blk // dmax

    lane = lax.broadcasted_iota(jnp.int32, (1, LANES), 1)
    head_a = lane < HEAD_DIM

    def window(ref, i, nblk):
        start = i if i % nblk == 0 else i - 1
        return ref[start * blk:(i + 1) * blk, :]

    def scores(gi, i, nblk):
        q_ref, k_ref, _ = qkv[3 * gi:3 * gi + 3]
        q = q_ref[i * blk:(i + 1) * blk, :]
        if gi == 0:
            mask = pmask_c_ref[...] if i % nblk == 0 else pmask_w_ref[...]
        else:
            mask = mask_c_ref[...] if i % nblk == 0 else mask_w_ref[...]
        zero = jnp.zeros_like(q)
        q2 = jnp.concatenate([jnp.where(head_a, q, zero), jnp.where(head_a, zero, q)], axis=0)
        return lax.dot_general(q2, window(k_ref, i, nblk), (((1,), (1,)), ((), ())),
                               preferred_element_type=F32) + mask

    def softmax(sc):
        m = jnp.max(sc, axis=1, keepdims=True)
        return m, jnp.exp2((sc - m).astype(BF16))

    def finish(gi, d, i, nblk, m, p):
        vw = window(qkv[3 * gi + 2], i, nblk)
        v_ext = jnp.concatenate([vw, jnp.ones_like(vw)], axis=1)
        res = jnp.dot(p, v_ext, preferred_element_type=F32)
        acc = jnp.where(head_a, res[:blk, :LANES], res[blk:, :LANES])
        l = jnp.where(head_a, res[:blk, LANES:], res[blk:, LANES:])
        m = jnp.where(head_a, m[:blk], m[blk:])
        rows = slice(i * blk, (i + 1) * blk)
        if d > 1:
            acc_s[gi - 1, rows, :] = acc
            l_s[gi - 1, rows, :] = l
            m_s[gi - 1, rows, :] = m
            return

        def parked(ref, g):
            dg = DILATIONS[g + 1]
            tiles = []
            for r in range(dmax):
                start = (r % dg) * (s // dg) + (blk // dg) * i + r // dg
                idx = pl.ds(start, tile, stride=dmax // dg) if dg < dmax else pl.ds(start, tile)
                tiles.append(ref[g, idx, :])
            return jnp.concatenate(tiles, axis=0)

        parts = [(m, l, acc)] + [(parked(m_s, g), parked(l_s, g), parked(acc_s, g))
                                 for g in range(N_GROUPS - 1)]
        m_all = functools.reduce(jnp.maximum, [pm for pm, _, _ in parts])
        num = jnp.zeros((blk, LANES), F32)
        den = jnp.zeros((blk, LANES), F32)
        for pm, pl_, pacc in parts:
            wgt = jnp.exp2(pm - m_all)
            num = num + wgt * pacc
            den = den + wgt * pl_
        merged_s[i] = num / den
        natural = [merged_s[i, pl.ds((a0 % dmax) * tile + a0 // dmax, tile, stride=tile), :]
                   for a0 in range(0, blk, tile)]
        o_ref[rows, :] = jnp.concatenate(natural, axis=0).astype(o_ref.dtype)

    assert DILATIONS[0] == 1
    block = lambda gi, i: (gi, DILATIONS[gi], i, (s // DILATIONS[gi]) // blk)
    ready = [0] * n_blocks
    dilated = []
    for gi in reversed(range(1, N_GROUPS)):
        nblk = (s // DILATIONS[gi]) // blk
        order = sorted(range(n_blocks), key=lambda b: (b % nblk, b // nblk))
        for pos, b in enumerate(order):
            dilated.append(block(gi, b))
            for i in range(n_blocks):
                if i * nblk // n_blocks == b % nblk:
                    ready[i] = max(ready[i], len(dilated))
    work, done = [], 0
    for i in range(n_blocks):
        take = max(ready[i] - done, 1 if done < len(dilated) else 0)
        work += dilated[done:done + take]
        done += take
        work.append(block(0, i))
    work += dilated[done:]
    lag_softmax, lag_values = lags
    sc_q, p_q = {}, {}
    for step in range(len(work) + lag_values):
        if step < len(work):
            gi, d, i, nblk = work[step]
            sc_q[step] = scores(gi, i, nblk)
        if 0 <= step - lag_softmax < len(work):
            p_q[step - lag_softmax] = softmax(sc_q.pop(step - lag_softmax))
        if step >= lag_values:
            gi, d, i, nblk = work[step - lag_values]
            finish(gi, d, i, nblk, *p_q.pop(step - lag_values))


def _attention(qkv_groups, b, s, lags=(1, 4)):
    n_hp = ATTN_WIDTH // LANES
    in_specs, args = [], []
    for gi, d in enumerate(DILATIONS):
        arr = qkv_groups[gi].reshape(b, s, QKV_WIDTH)
        for part in range(3):
            in_specs.append(pl.BlockSpec(
                (None, s, LANES),
                functools.partial(lambda bi, hp, part: (bi, 0, part * n_hp + hp), part=part)))
            args.append(arr)
    (mask_w, mask_c), (pmask_w, pmask_c) = _attn_bias()
    masks = [mask_w, mask_c, pmask_w, pmask_c]
    return pl.pallas_call(
        functools.partial(_attn_kernel, s=s, lags=lags),
        grid=(b, n_hp),
        in_specs=in_specs + [_const_spec(t.shape) for t in masks],
        out_specs=pl.BlockSpec((None, s, LANES), lambda bi, hp: (bi, 0, hp)),
        out_shape=jax.ShapeDtypeStruct((b, s, ATTN_WIDTH), BF16),
        scratch_shapes=[pltpu.VMEM((N_GROUPS - 1, s, LANES), F32)] * 3
                       + [pltpu.VMEM((s // ATTN_BLOCK, ATTN_BLOCK, LANES), F32)],
        compiler_params=pltpu.CompilerParams(
            dimension_semantics=("parallel", "parallel"), vmem_limit_bytes=VMEM_LIMIT),
        name="attention",
    )(*args, *masks)


def _gelu_tanh(x):
    return 0.5 * x * (1.0 + jnp.tanh(math.sqrt(2.0 / math.pi) * (x + 0.044715 * (x * x * x))))


def _ssm_kernel(*refs, tc, bsz, n_side):
    u_ref, bm_ref, are_ref, aim_ref, cm_ref, dskip_ref = refs[:6]
    o_ref = refs[6 + n_side]
    st_a, st_b, ub_s, y_s, h_ref = refs[7 + 2 * n_side:]
    _cast_slices(refs[6:6 + n_side], refs[7 + n_side:7 + 2 * n_side])
    rows = tc * bsz
    half = SSM_PAIRS // 2
    pairs_per_slab = LANES // (2 * SSM_GROUP)
    n_slabs = SSM_WIDTH // LANES
    splits = 2
    n_slices = half * splits
    steps = tc // n_slices
    srows = rows // splits

    @pl.when(pl.program_id(0) == 0)
    def _():
        h_ref[...] = jnp.zeros_like(h_ref)

    for c in range(n_slabs):
        cols = slice(c * LANES, (c + 1) * LANES)
        for b in range(bsz):
            y_s[c, pl.ds(b, tc, stride=bsz), :] = u_ref[b, :, cols]
        u = y_s[c]
        ub_s[c] = u.astype(BF16)
        y_s[c] = dskip_ref[:, cols] * u

    def project_in(buf, j, jl, rs):
        bu = jnp.dot(ub_s[j // pairs_per_slab, rs, :], bm_ref[j], preferred_element_type=F32)
        buf[0, jl, rs, :] = bu[:, :LANES]
        buf[1, jl, rs, :] = bu[:, LANES:]

    def project_out(buf, j, jl, rs):
        st = jnp.concatenate([buf[0, jl, rs, :].astype(BF16), buf[1, jl, rs, :].astype(BF16)],
                             axis=1)
        y_s[j // pairs_per_slab, rs, :] += jnp.dot(st, cm_ref[j], preferred_element_type=F32)

    def scan(buf, j0, other):
        a_re = [are_ref[j0 + i] for i in range(half)]
        a_im = [aim_ref[j0 + i] for i in range(half)]

        def body(it, carry):
            h_re, h_im = carry
            for k in range(steps):
                r0 = pl.multiple_of((it * steps + k) * bsz, bsz)
                new_re, new_im = [], []
                for i in range(half):
                    n_re = a_re[i] * h_re[i] - a_im[i] * h_im[i] + buf[0, i, pl.ds(r0, bsz), :]
                    n_im = a_re[i] * h_im[i] + a_im[i] * h_re[i] + buf[1, i, pl.ds(r0, bsz), :]
                    buf[0, i, pl.ds(r0, bsz), :] = n_re
                    buf[1, i, pl.ds(r0, bsz), :] = n_im
                    new_re.append(n_re)
                    new_im.append(n_im)
                h_re, h_im = tuple(new_re), tuple(new_im)
            other(it // splits, pl.ds((it % splits) * srows, srows))
            return h_re, h_im

        carry = (tuple(h_ref[0, j0 + i] for i in range(half)),
                 tuple(h_ref[1, j0 + i] for i in range(half)))
        for it in range(n_slices):
            carry = body(it, carry)
        h_re, h_im = carry
        for i in range(half):
            h_ref[0, j0 + i] = h_re[i]
            h_ref[1, j0 + i] = h_im[i]

    for jl in range(half):
        project_in(st_a, jl, jl, slice(None))
    scan(st_a, 0, lambda jl, rs: project_in(st_b, half + jl, jl, rs))
    scan(st_b, half, lambda jl, rs: project_out(st_a, jl, jl, rs))
    for jl in range(half):
        project_out(st_b, half + jl, jl, slice(None))

    for c in range(n_slabs):
        for b in range(bsz):
            y = y_s[c, pl.ds(b, tc, stride=bsz), :]
            o_ref[b, :, c * LANES:(c + 1) * LANES] = _gelu_tanh(y).astype(o_ref.dtype)


def _ssm(u, bm, a_re, a_im, cm, d_skip, side, tc=128):
    bsz, s, _ = u.shape
    rows = tc * bsz
    half = SSM_PAIRS // 2
    chunk_spec = pl.BlockSpec((bsz, tc, SSM_WIDTH), lambda i: (0, i, 0))
    side_in, side_out, side_shapes = _cast_slice_specs(side, s // tc, lambda i: i)
    outs = pl.pallas_call(
        functools.partial(_ssm_kernel, tc=tc, bsz=bsz, n_side=len(side)),
        grid=(s // tc,),
        in_specs=[chunk_spec,
                  _const_spec(bm.shape), _const_spec(a_re.shape), _const_spec(a_im.shape),
                  _const_spec(cm.shape), _const_spec(d_skip.shape)] + side_in,
        out_specs=[chunk_spec] + side_out,
        out_shape=[jax.ShapeDtypeStruct((bsz, s, SSM_WIDTH), BF16)] + side_shapes,
        scratch_shapes=[pltpu.VMEM((2, half, rows, LANES), F32),
                        pltpu.VMEM((2, half, rows, LANES), F32),
                        pltpu.VMEM((SSM_WIDTH // LANES, rows, LANES), BF16),
                        pltpu.VMEM((SSM_WIDTH // LANES, rows, LANES), F32),
                        pltpu.VMEM((2, SSM_PAIRS, bsz, LANES), F32)],
        compiler_params=pltpu.CompilerParams(
            dimension_semantics=("arbitrary",), vmem_limit_bytes=VMEM_LIMIT),
        name="ssm",
    )(u, bm, a_re, a_im, cm, d_skip, *[arr for arr, _ in side])
    return outs[0], outs[1:]


def _ssm_matrices(a_re, a_im, log_dt, b_re, b_im, c_re, c_im, bsz):
    g, p, h = SSM_GROUPS, SSM_STATE, SSM_GROUP
    lam_re, lam_im = a_re.astype(F32), a_im.astype(F32)
    dt = jnp.exp(log_dt.astype(F32))[:, None]
    mag = jnp.exp(lam_re * dt)
    abar_re, abar_im = mag * jnp.cos(lam_im * dt), mag * jnp.sin(lam_im * dt)
    num_re, num_im = abar_re - 1.0, abar_im
    den = lam_re * lam_re + lam_im * lam_im
    coef_re = ((num_re * lam_re + num_im * lam_im) / den)[..., None]
    coef_im = ((num_im * lam_re - num_re * lam_im) / den)[..., None]
    b_re, b_im = b_re.astype(F32), b_im.astype(F32)
    bbar_re = coef_re * b_re - coef_im * b_im
    bbar_im = coef_re * b_im + coef_im * b_re

    groups_per_slab = LANES // h
    pairs_per_slab = groups_per_slab // 2
    n_slabs = g // groups_per_slab
    eye = jnp.eye(groups_per_slab, dtype=F32)

    def in_matrix(bpart):
        bt = jnp.transpose(bpart, (0, 2, 1)).reshape(n_slabs, groups_per_slab, h, p)
        dense = bt[:, :, :, None, :] * eye[None, :, None, :, None]
        dense = dense.reshape(n_slabs, LANES, pairs_per_slab, 2 * p)
        return jnp.transpose(dense, (0, 2, 1, 3)).reshape(SSM_PAIRS, LANES, 2 * p)

    def out_matrix(cpart):
        ct = jnp.transpose(cpart, (0, 2, 1)).reshape(n_slabs, groups_per_slab, p, h)
        dense = ct[:, :, :, None, :] * eye[None, :, None, :, None]
        return dense.reshape(SSM_PAIRS, 2 * p, LANES)

    bm = jnp.concatenate([in_matrix(bbar_re), in_matrix(bbar_im)], axis=2)
    cm = jnp.concatenate([out_matrix(c_re.astype(F32)), out_matrix(-c_im.astype(F32))], axis=1)

    def bcast(apart):
        return jnp.broadcast_to(apart.reshape(SSM_PAIRS, 1, 2 * p), (SSM_PAIRS, bsz, 2 * p))

    return bm.astype(BF16), bcast(abar_re), bcast(abar_im), cm.astype(BF16)


def _layer_norm(z, g, b):
    mu = jnp.mean(z, axis=-1, keepdims=True)
    zc = z - mu
    var = jnp.mean(zc * zc, axis=-1, keepdims=True)
    return zc * lax.rsqrt(var + LN_EPS) * g + b


def _mix_ffn_kernel(x_ref, attn_ref, g_ref, wga_ref, wgs_ref, bgate_ref, wattn_ref, wglu_ref,
                    wssm_ref, wout_ref, ln1g_ref, ln1b_ref, wfg_ref, wfu_ref, wfd_ref,
                    ln2g_ref, ln2b_ref, o_ref, *, alpha, sub):
    dm = x_ref.shape[1]
    tiles = [slice(r0, r0 + sub) for r0 in range(0, x_ref.shape[0], sub)]
    dot = functools.partial(jnp.dot, preferred_element_type=F32)
    y_s = [None] * len(tiles)
    for t, rows in enumerate(tiles):
        glu = dot(g_ref[rows, :], wglu_ref[...])
        y_s[t] = (glu[:, :SSM_WIDTH] * jax.nn.sigmoid(glu[:, SSM_WIDTH:])).astype(BF16)
    gated_ssm = [None] * len(tiles)
    gate_attn = [None] * len(tiles)
    for t, rows in enumerate(tiles):
        xb = x_ref[rows, :].astype(BF16)
        gate_attn[t] = jax.nn.sigmoid(dot(xb, wga_ref[...]) + bgate_ref[:, :dm])
        gate_ssm = jax.nn.sigmoid(dot(xb, wgs_ref[...]) + bgate_ref[:, dm:])
        gated_ssm[t] = gate_ssm * dot(y_s[t], wssm_ref[...])
    mixed = [None] * len(tiles)
    for t, rows in enumerate(tiles):
        y_attn = dot(attn_ref[rows, :], wattn_ref[...])
        mixed[t] = (gate_attn[t] * y_attn + gated_ssm[t]).astype(BF16)
    h = [None] * len(tiles)
    for t, rows in enumerate(tiles):
        mix_out = dot(mixed[t], wout_ref[...])
        h[t] = _layer_norm(alpha * x_ref[rows, :] + mix_out, ln1g_ref[...], ln1b_ref[...])
    act = [None] * len(tiles)
    for t, rows in enumerate(tiles):
        hb = h[t].astype(BF16)
        gate = dot(hb, wfg_ref[...])
        up = dot(hb, wfu_ref[...])
        act[t] = ((gate * jax.nn.sigmoid(gate)) * up).astype(BF16)
    for t, rows in enumerate(tiles):
        ff = dot(act[t], wfd_ref[...])
        o_ref[rows, :] = _layer_norm(alpha * h[t] + ff, ln2g_ref[...], ln2b_ref[...])


def _mix_ffn(x2, attn2, g2, consts, alpha, tm=512, sub=256):
    rows, dm = x2.shape
    row_spec = lambda w: pl.BlockSpec((tm, w), lambda i: (i, 0))
    return pl.pallas_call(
        functools.partial(_mix_ffn_kernel, alpha=alpha, sub=sub),
        grid=(rows // tm,),
        in_specs=[row_spec(dm), row_spec(ATTN_WIDTH), row_spec(SSM_WIDTH)]
                 + [_const_spec(c.shape) for c in consts],
        out_specs=row_spec(dm),
        out_shape=jax.ShapeDtypeStruct((rows, dm), F32),
        compiler_params=pltpu.CompilerParams(
            dimension_semantics=("parallel",), vmem_limit_bytes=VMEM_LIMIT),
        name="mix_ffn",
    )(x2, attn2, g2, *consts)


def _rope_tables(s):
    half = HEAD_DIM // 2
    f32 = np.float32
    pos = np.arange(s, dtype=f32)
    inv_freq = f32(ROPE_THETA) ** (-np.arange(half, dtype=f32) / f32(half))
    ang = pos[:, None] * inv_freq[None, :]
    cos, sin = np.cos(ang), np.sin(ang)
    reps = LANES // HEAD_DIM
    cos_t = np.concatenate([cos, cos] * reps, axis=1)
    sin_t = np.concatenate([-sin, sin] * reps, axis=1)
    return cos_t.astype(f32), sin_t.astype(f32)


def kernel(x, w_in, b_gate, w_attn_br, w_ssm_br, w_out, ssm_a_re, ssm_a_im, ssm_log_dt, ssm_b_re, ssm_b_im, ssm_c_re, ssm_c_im, ssm_d, w_glu, ln1_g, ln1_b, w_ff_gate, w_ff_up, w_ff_down, ln2_g, ln2_b):
    bsz, s, dm = x.shape
    depth = w_in.shape[0]
    alpha = (2.0 * depth) ** 0.25
    assert all(w // d == ATTN_BLOCK for w, d in zip(WINDOWS, DILATIONS))
    assert s % (ATTN_BLOCK * max(DILATIONS)) == 0
    cos_t, sin_t = _rope_tables(s)
    assert w_in.shape[2] == 3 * QKV_WIDTH + SSM_WIDTH + 2 * dm

    gate_col0 = (3 * QKV_WIDTH + SSM_WIDTH) // dm

    for layer in range(depth):
        outs, (wga, wgs, wattn, wglu, wssm, wout) = _project(
            x, cos_t, sin_t, w_in[layer],
            [(w_in[layer], (gate_col0, dm)), (w_in[layer], (gate_col0 + 1, dm)),
             (w_attn_br[layer], None), (w_glu[layer], None), (w_ssm_br[layer], None),
             (w_out[layer], None)])
        attn = _attention(outs[:N_GROUPS], bsz, s)

        bm, a_re, a_im, cm = _ssm_matrices(
            ssm_a_re[layer], ssm_a_im[layer], ssm_log_dt[layer], ssm_b_re[layer],
            ssm_b_im[layer], ssm_c_re[layer], ssm_c_im[layer], bsz)
        g, (wff_gate, wff_up, wff_down) = _ssm(
            outs[N_GROUPS], bm, a_re, a_im, cm, ssm_d[layer].reshape(1, SSM_WIDTH).astype(F32),
            [(w_ff_gate[layer], None), (w_ff_up[layer], None), (w_ff_down[layer], None)])

        row = lambda v: v[layer].reshape(1, -1).astype(F32)
        x2 = _mix_ffn(
            x.reshape(bsz * s, dm), attn.reshape(bsz * s, ATTN_WIDTH),
            g.reshape(bsz * s, SSM_WIDTH),
            [wga, wgs, row(b_gate), wattn, wglu, wssm, wout, row(ln1_g), row(ln1_b),
             wff_gate, wff_up, wff_down, row(ln2_g), row(ln2_b)],
            alpha)
        x = x2.reshape(bsz, s, dm)
    return x
```

```python
import functools
import math

import jax
import jax.numpy as jnp
import numpy as np
from jax import lax
from jax.experimental import pallas as pl
from jax.experimental.pallas import tpu as pltpu

F32 = jnp.float32
BF16 = jnp.bfloat16

LANES = 128
HEAD_DIM = 64
ATTN_HEADS = 8
ATTN_WIDTH = ATTN_HEADS * HEAD_DIM
DILATIONS = (1, 4, 16)
WINDOWS = (128, 512, 2048)
N_GROUPS = len(DILATIONS)
QKV_WIDTH = N_GROUPS * ATTN_WIDTH
QA_COL, QB_COL, K_COL, VA_COL, VB_COL = (i * ATTN_WIDTH for i in range(5))
ATTN_IN_WIDTH = 5 * ATTN_WIDTH
ATTN_BLOCK = 128
ROPE_THETA = 10000.0
NEG_INF = -1e30
Q_SCALE = math.log2(math.e) / math.sqrt(HEAD_DIM)
SSM_GROUP = 16
SSM_GROUPS = 32
SSM_WIDTH = SSM_GROUP * SSM_GROUPS
SSM_STATE = 64
SSM_PAIRS = SSM_GROUPS // 2
LN_EPS = 1e-5

VMEM_LIMIT = 56 * 1024 * 1024


def _const_spec(shape):
    zeros = (0,) * len(shape)
    return pl.BlockSpec(shape, lambda *_: zeros, pipeline_mode=pl.Buffered(1))


def _cast_slice_specs(side, n_steps, step_of):
    in_specs, out_specs, out_shapes = [], [], []
    for arr, col in side:
        rows = arr.shape[0] // n_steps
        assert rows * n_steps == arr.shape[0] and rows % 16 == 0
        cidx, width = col if col is not None else (0, arr.shape[1])
        in_specs.append(pl.BlockSpec(
            (rows, width), functools.partial(lambda *ids, cidx: (step_of(*ids), cidx), cidx=cidx)))
        out_specs.append(pl.BlockSpec((rows, width), lambda *ids: (step_of(*ids), 0)))
        out_shapes.append(jax.ShapeDtypeStruct((arr.shape[0], width), BF16))
    return in_specs, out_specs, out_shapes


def _cast_slices(src_refs, dst_refs):
    for src, dst in zip(src_refs, dst_refs):
        dst[...] = src[...].astype(dst.dtype)


def _phase_major(ref, tm, d):
    if d == 1:
        return ref[...]
    n = tm // d
    return jnp.concatenate([ref[pl.ds(r, n, stride=d), :] for r in range(d)], axis=0)


def _block_phase_order(ref, tm):
    dmax = max(DILATIONS)
    tile = ATTN_BLOCK // dmax
    return jnp.concatenate([ref[pl.ds(b0 + r, tile, stride=dmax), :]
                            for b0 in range(0, tm, ATTN_BLOCK) for r in range(dmax)], axis=0)


def _proj_kernel(*refs, tm, n_slabs, n_side):
    refs = list(refs)
    x_refs = [refs.pop(0) for _ in range(n_slabs)]
    cos_ref, sin_ref, w32_ref = [refs.pop(0) for _ in range(3)]
    side_in = [refs.pop(0) for _ in range(n_side)]
    o_refs = [refs.pop(0) for _ in range(N_GROUPS)]
    u_ref = refs.pop(0)
    side_out = [refs.pop(0) for _ in range(n_side)]
    w_ref, = refs

    @pl.when((pl.program_id(0) == 0) & (pl.program_id(1) == 0))
    def _():
        for c0 in range(0, w_ref.shape[1], ATTN_WIDTH):
            w_ref[:, c0:c0 + ATTN_WIDTH] = w32_ref[:, c0:c0 + ATTN_WIDTH].astype(BF16)

    _cast_slices(side_in, side_out)

    lane = lax.broadcasted_iota(jnp.int32, (1, LANES), 1)
    first_half = (lane % HEAD_DIM) < (HEAD_DIM // 2)
    head_a = lane < HEAD_DIM

    for gi, d in enumerate(DILATIONS):
        n = tm // d
        xg = jnp.concatenate(
            [_phase_major(xr, tm, d).astype(BF16) for xr in x_refs], axis=1)
        cos = _phase_major(cos_ref, tm, d)
        sin = _phase_major(sin_ref, tm, d)
        if gi == 0:
            u_ref[...] = jnp.dot(xg, w_ref[:, 3 * QKV_WIDTH:], preferred_element_type=F32)
        for part in range(3):
            c0 = part * ATTN_WIDTH
            w0 = part * QKV_WIDTH + gi * ATTN_WIDTH
            lhs, rope_cos, rope_sin = xg, cos, sin
            if gi == 0 and part == 0:
                lhs = jnp.concatenate(
                    [_block_phase_order(xr, tm).astype(BF16) for xr in x_refs], axis=1)
                rope_cos = _block_phase_order(cos_ref, tm)
                rope_sin = _block_phase_order(sin_ref, tm)
            acc = jnp.dot(lhs, w_ref[:, w0:w0 + ATTN_WIDTH], preferred_element_type=F32)
            for c in range(ATTN_WIDTH // LANES):
                t = acc[:, c * LANES:(c + 1) * LANES]
                if part < 2:
                    rot = jnp.where(first_half,
                                    pltpu.roll(t, LANES - HEAD_DIM // 2, 1),
                                    pltpu.roll(t, HEAD_DIM // 2, 1))
                    t = t * rope_cos + rot * rope_sin
                if part == 0:
                    t = t * Q_SCALE
                if part == 1:
                    pieces = [(K_COL, t)]
                else:
                    col_a, col_b = (QA_COL, QB_COL) if part == 0 else (VA_COL, VB_COL)
                    pieces = [(col_a, jnp.where(head_a, t, 0.0)), (col_b, jnp.where(head_a, 0.0, t))]
                for col0, piece in pieces:
                    piece = piece.astype(BF16)
                    col = col0 + c * LANES
                    if d == 1:
                        o_refs[gi][:, col:col + LANES] = piece
                    else:
                        for r in range(d):
                            o_refs[gi][r, :, col:col + LANES] = piece[r * n:(r + 1) * n]


def _project(x, cos_t, sin_t, w_in, side, tm=512):
    b, s, dm = x.shape
    n_slabs = dm // LANES
    n_proj = 3 * QKV_WIDTH + SSM_WIDTH
    grid = (b, s // tm)
    side_in, side_out, side_shapes = _cast_slice_specs(
        side, b * (s // tm), lambda bi, i: bi * (s // tm) + i)
    x_specs = [pl.BlockSpec((None, tm, LANES), functools.partial(lambda bi, i, c: (bi, i, c), c=c))
               for c in range(n_slabs)]
    tab_spec = pl.BlockSpec((tm, LANES), lambda bi, i: (i, 0))
    out_shapes = []
    out_specs = []
    for d in DILATIONS:
        if d == 1:
            out_shapes.append(jax.ShapeDtypeStruct((b, s, ATTN_IN_WIDTH), BF16))
            out_specs.append(pl.BlockSpec((None, tm, ATTN_IN_WIDTH), lambda bi, i: (bi, i, 0)))
        else:
            out_shapes.append(jax.ShapeDtypeStruct((b, d, s // d, ATTN_IN_WIDTH), BF16))
            out_specs.append(pl.BlockSpec((None, d, tm // d, ATTN_IN_WIDTH),
                                          lambda bi, i: (bi, 0, i, 0)))
    out_shapes.append(jax.ShapeDtypeStruct((b, s, SSM_WIDTH), F32))
    out_specs.append(pl.BlockSpec((None, tm, SSM_WIDTH), lambda bi, i: (bi, i, 0)))
    outs = pl.pallas_call(
        functools.partial(_proj_kernel, tm=tm, n_slabs=n_slabs, n_side=len(side)),
        grid=grid,
        in_specs=x_specs + [tab_spec, tab_spec,
                            pl.BlockSpec((dm, n_proj), lambda bi, i: (0, 0),
                                         pipeline_mode=pl.Buffered(1))] + side_in,
        out_specs=out_specs + side_out,
        out_shape=out_shapes + side_shapes,
        scratch_shapes=[pltpu.VMEM((dm, n_proj), BF16)],
        compiler_params=pltpu.CompilerParams(
            dimension_semantics=("arbitrary", "arbitrary"), vmem_limit_bytes=VMEM_LIMIT),
        name="in_proj",
    )(*([x] * n_slabs), cos_t, sin_t, w_in, *[arr for arr, _ in side])
    n_main = N_GROUPS + 1
    return outs[:n_main], outs[n_main:]


def _attn_bias():
    blk = ATTN_BLOCK
    c = np.arange(2 * blk)[None, :]

    def tables(position):
        a = np.tile(position, 2)[:, None]
        in_window = np.where(c < blk, c >= a, c - blk <= a)
        windowed = np.where(in_window, 0.0, NEG_INF).astype(np.float32)
        causal = np.where(c[:, :blk] <= a, 0.0, NEG_INF).astype(np.float32)
        return windowed, causal

    return tables(np.arange(blk)), tables(_PHASE_ORDER)


_PHASE_ORDER = np.arange(ATTN_BLOCK).reshape(-1, max(DILATIONS)).T.reshape(-1)


def _attn_kernel(*refs, s, lags):
    n_in = 5 * N_GROUPS
    qkv = refs[:n_in]
    mask_w_ref, mask_c_ref, pmask_w_ref, pmask_c_ref, ones_ref = refs[n_in:n_in + 5]
    o_ref = refs[n_in + 5]
    acc_s, m_s, l_s, merged_s = refs[n_in + 6:]
    blk = ATTN_BLOCK
    n_blocks = s // blk
    dmax = max(DILATIONS)
    tile = blk // dmax

    lane = lax.broadcasted_iota(jnp.int32, (1, LANES), 1)
    head_a = lane < HEAD_DIM

    def window(ref, i, nblk):
        start = i if i % nblk == 0 else i - 1
        return ref[start * blk:(i + 1) * blk, :]

    def scores(gi, i, nblk):
        qa_ref, qb_ref, k_ref = qkv[5 * gi:5 * gi + 3]
        if gi == 0:
            mask = pmask_c_ref[...] if i % nblk == 0 else pmask_w_ref[...]
        else:
            mask = mask_c_ref[...] if i % nblk == 0 else mask_w_ref[...]
        q2 = jnp.concatenate([qa_ref[i * blk:(i + 1) * blk, :], qb_ref[i * blk:(i + 1) * blk, :]],
                             axis=0)
        return lax.dot_general(q2, window(k_ref, i, nblk), (((1,), (1,)), ((), ())),
                               preferred_element_type=F32) + mask

    def softmax(sc):
        m = jnp.max(sc, axis=1, keepdims=True)
        return m, jnp.exp2((sc - m).astype(BF16))

    def finish(gi, d, i, nblk, m, p):
        va, vb = (window(ref, i, nblk) for ref in qkv[5 * gi + 3:5 * gi + 5])
        w = va.shape[0]
        v_ext = jnp.concatenate(
            [jnp.concatenate([va, ones_ref[0, :w, :]], axis=1),
             jnp.concatenate([vb, ones_ref[1, :w, :]], axis=1)], axis=0)
        p2 = jnp.concatenate([p[:blk], p[blk:]], axis=1)
        res = jnp.dot(p2, v_ext, preferred_element_type=F32)
        acc, l = res[:, :LANES], res[:, LANES:]
        m = jnp.where(head_a, m[:blk], m[blk:])
        rows = slice(i * blk, (i + 1) * blk)
        if d > 1:
            acc_s[gi - 1, rows, :] = acc
            l_s[gi - 1, rows, :] = l
            m_s[gi - 1, rows, :] = m
            return

        def parked(ref, g):
            dg = DILATIONS[g + 1]
            tiles = []
            for r in range(dmax):
                start = (r % dg) * (s // dg) + (blk // dg) * i + r // dg
                idx = pl.ds(start, tile, stride=dmax // dg) if dg < dmax else pl.ds(start, tile)
                tiles.append(ref[g, idx, :])
            return jnp.concatenate(tiles, axis=0)

        parts = [(m, l, acc)] + [(parked(m_s, g), parked(l_s, g), parked(acc_s, g))
                                 for g in range(N_GROUPS - 1)]
        m_all = functools.reduce(jnp.maximum, [pm for pm, _, _ in parts])
        num = jnp.zeros((blk, LANES), F32)
        den = jnp.zeros((blk, LANES), F32)
        for pm, pl_, pacc in parts:
            wgt = jnp.exp2(pm - m_all)
            num = num + wgt * pacc
            den = den + wgt * pl_
        merged_s[i] = num / den
        natural = [merged_s[i, pl.ds((a0 % dmax) * tile + a0 // dmax, tile, stride=tile), :]
                   for a0 in range(0, blk, tile)]
        o_ref[rows, :] = jnp.concatenate(natural, axis=0).astype(o_ref.dtype)

    assert DILATIONS[0] == 1
    block = lambda gi, i: (gi, DILATIONS[gi], i, (s // DILATIONS[gi]) // blk)
    ready = [0] * n_blocks
    dilated = []
    for gi in reversed(range(1, N_GROUPS)):
        nblk = (s // DILATIONS[gi]) // blk
        order = sorted(range(n_blocks), key=lambda b: (b % nblk, b // nblk))
        for pos, b in enumerate(order):
            dilated.append(block(gi, b))
            for i in range(n_blocks):
                if i * nblk // n_blocks == b % nblk:
                    ready[i] = max(ready[i], len(dilated))
    work, done = [], 0
    for i in range(n_blocks):
        take = max(ready[i] - done, 1 if done < len(dilated) else 0)
        work += dilated[done:done + take]
        done += take
        work.append(block(0, i))
    work += dilated[done:]
    lag_softmax, lag_values = lags
    sc_q, p_q = {}, {}
    for step in range(len(work) + lag_values):
        if step < len(work):
            gi, d, i, nblk = work[step]
            sc_q[step] = scores(gi, i, nblk)
        if 0 <= step - lag_softmax < len(work):
            p_q[step - lag_softmax] = softmax(sc_q.pop(step - lag_softmax))
        if step >= lag_values:
            gi, d, i, nblk = work[step - lag_values]
            finish(gi, d, i, nblk, *p_q.pop(step - lag_values))


def _attention(qkv_groups, b, s, lags=(1, 4)):
    n_hp = ATTN_WIDTH // LANES
    in_specs, args = [], []
    for gi, d in enumerate(DILATIONS):
        arr = qkv_groups[gi].reshape(b, s, ATTN_IN_WIDTH)
        for col0 in (QA_COL, QB_COL, K_COL, VA_COL, VB_COL):
            in_specs.append(pl.BlockSpec(
                (None, s, LANES),
                functools.partial(lambda bi, hp, c: (bi, 0, c + hp), c=col0 // LANES)))
            args.append(arr)
    (mask_w, mask_c), (pmask_w, pmask_c) = _attn_bias()
    head_ones = np.stack([np.arange(LANES) < HEAD_DIM, np.arange(LANES) >= HEAD_DIM])
    head_ones = jnp.asarray(np.broadcast_to(head_ones[:, None, :], (2, 2 * ATTN_BLOCK, LANES)),
                            dtype=BF16)
    masks = [mask_w, mask_c, pmask_w, pmask_c, head_ones]
    return pl.pallas_call(
        functools.partial(_attn_kernel, s=s, lags=lags),
        grid=(b, n_hp),
        in_specs=in_specs + [_const_spec(t.shape) for t in masks],
        out_specs=pl.BlockSpec((None, s, LANES), lambda bi, hp: (bi, 0, hp)),
        out_shape=jax.ShapeDtypeStruct((b, s, ATTN_WIDTH), BF16),
        scratch_shapes=[pltpu.VMEM((N_GROUPS - 1, s, LANES), F32)] * 3
                       + [pltpu.VMEM((s // ATTN_BLOCK, ATTN_BLOCK, LANES), F32)],
        compiler_params=pltpu.CompilerParams(
            dimension_semantics=("parallel", "parallel"), vmem_limit_bytes=VMEM_LIMIT),
        name="attention",
    )(*args, *masks)


def _gelu_tanh(x):
    return 0.5 * x * (1.0 + jnp.tanh(math.sqrt(2.0 / math.pi) * (x + 0.044715 * (x * x * x))))


def _ssm_kernel(*refs, tc, bsz, n_side):
    u_ref, bm_ref, are_ref, aim_ref, cm_ref, dskip_ref = refs[:6]
    o_ref = refs[6 + n_side]
    st_a, st_b, ub_s, y_s, h_ref = refs[7 + 2 * n_side:]
    _cast_slices(refs[6:6 + n_side], refs[7 + n_side:7 + 2 * n_side])
    rows = tc * bsz
    half = SSM_PAIRS // 2
    pairs_per_slab = LANES // (2 * SSM_GROUP)
    n_slabs = SSM_WIDTH // LANES
    splits = 2
    n_slices = half * splits
    steps = tc // n_slices
    srows = rows // splits

    @pl.when(pl.program_id(0) == 0)
    def _():
        h_ref[...] = jnp.zeros_like(h_ref)

    for c in range(n_slabs):
        cols = slice(c * LANES, (c + 1) * LANES)
        for b in range(bsz):
            y_s[c, pl.ds(b, tc, stride=bsz), :] = u_ref[b, :, cols]
        u = y_s[c]
        ub_s[c] = u.astype(BF16)
        y_s[c] = dskip_ref[:, cols] * u

    def project_in(buf, j, jl, rs):
        bu = jnp.dot(ub_s[j // pairs_per_slab, rs, :], bm_ref[j], preferred_element_type=F32)
        buf[0, jl, rs, :] = bu[:, :LANES]
        buf[1, jl, rs, :] = bu[:, LANES:]

    def project_out(buf, j, jl, rs):
        st = jnp.concatenate([buf[0, jl, rs, :].astype(BF16), buf[1, jl, rs, :].astype(BF16)],
                             axis=1)
        y_s[j // pairs_per_slab, rs, :] += jnp.dot(st, cm_ref[j], preferred_element_type=F32)

    def scan(buf, j0, other):
        a_re = [are_ref[j0 + i] for i in range(half)]
        a_im = [aim_ref[j0 + i] for i in range(half)]

        def body(it, carry):
            h_re, h_im = carry
            for k in range(steps):
                r0 = pl.multiple_of((it * steps + k) * bsz, bsz)
                new_re, new_im = [], []
                for i in range(half):
                    n_re = a_re[i] * h_re[i] - a_im[i] * h_im[i] + buf[0, i, pl.ds(r0, bsz), :]
                    n_im = a_re[i] * h_im[i] + a_im[i] * h_re[i] + buf[1, i, pl.ds(r0, bsz), :]
                    buf[0, i, pl.ds(r0, bsz), :] = n_re
                    buf[1, i, pl.ds(r0, bsz), :] = n_im
                    new_re.append(n_re)
                    new_im.append(n_im)
                h_re, h_im = tuple(new_re), tuple(new_im)
            other(it // splits, pl.ds((it % splits) * srows, srows))
            return h_re, h_im

        carry = (tuple(h_ref[0, j0 + i] for i in range(half)),
                 tuple(h_ref[1, j0 + i] for i in range(half)))
        for it in range(n_slices):
            carry = body(it, carry)
        h_re, h_im = carry
        for i in range(half):
            h_ref[0, j0 + i] = h_re[i]
            h_ref[1, j0 + i] = h_im[i]

    for jl in range(half):
        project_in(st_a, jl, jl, slice(None))
    scan(st_a, 0, lambda jl, rs: project_in(st_b, half + jl, jl, rs))
    scan(st_b, half, lambda jl, rs: project_out(st_a, jl, jl, rs))
    for jl in range(half):
        project_out(st_b, half + jl, jl, slice(None))

    for c in range(n_slabs):
        for b in range(bsz):
            y = y_s[c, pl.ds(b, tc, stride=bsz), :]
            o_ref[b, :, c * LANES:(c + 1) * LANES] = _gelu_tanh(y).astype(o_ref.dtype)


def _ssm(u, bm, a_re, a_im, cm, d_skip, side, tc=128):
    bsz, s, _ = u.shape
    rows = tc * bsz
    half = SSM_PAIRS // 2
    chunk_spec = pl.BlockSpec((bsz, tc, SSM_WIDTH), lambda i: (0, i, 0))
    side_in, side_out, side_shapes = _cast_slice_specs(side, s // tc, lambda i: i)
    outs = pl.pallas_call(
        functools.partial(_ssm_kernel, tc=tc, bsz=bsz, n_side=len(side)),
        grid=(s // tc,),
        in_specs=[chunk_spec,
                  _const_spec(bm.shape), _const_spec(a_re.shape), _const_spec(a_im.shape),
                  _const_spec(cm.shape), _const_spec(d_skip.shape)] + side_in,
        out_specs=[chunk_spec] + side_out,
        out_shape=[jax.ShapeDtypeStruct((bsz, s, SSM_WIDTH), BF16)] + side_shapes,
        scratch_shapes=[pltpu.VMEM((2, half, rows, LANES), F32),
                        pltpu.VMEM((2, half, rows, LANES), F32),
                        pltpu.VMEM((SSM_WIDTH // LANES, rows, LANES), BF16),
                        pltpu.VMEM((SSM_WIDTH // LANES, rows, LANES), F32),
                        pltpu.VMEM((2, SSM_PAIRS, bsz, LANES), F32)],
        compiler_params=pltpu.CompilerParams(
            dimension_semantics=("arbitrary",), vmem_limit_bytes=VMEM_LIMIT),
        name="ssm",
    )(u, bm, a_re, a_im, cm, d_skip, *[arr for arr, _ in side])
    return outs[0], outs[1:]


def _ssm_matrices(a_re, a_im, log_dt, b_re, b_im, c_re, c_im, bsz):
    g, p, h = SSM_GROUPS, SSM_STATE, SSM_GROUP
    lam_re, lam_im = a_re.astype(F32), a_im.astype(F32)
    dt = jnp.exp(log_dt.astype(F32))[:, None]
    mag = jnp.exp(lam_re * dt)
    abar_re, abar_im = mag * jnp.cos(lam_im * dt), mag * jnp.sin(lam_im * dt)
    num_re, num_im = abar_re - 1.0, abar_im
    den = lam_re * lam_re + lam_im * lam_im
    coef_re = ((num_re * lam_re + num_im * lam_im) / den)[..., None]
    coef_im = ((num_im * lam_re - num_re * lam_im) / den)[..., None]
    b_re, b_im = b_re.astype(F32), b_im.astype(F32)
    bbar_re = coef_re * b_re - coef_im * b_im
    bbar_im = coef_re * b_im + coef_im * b_re

    groups_per_slab = LANES // h
    pairs_per_slab = groups_per_slab // 2
    n_slabs = g // groups_per_slab
    eye = jnp.eye(groups_per_slab, dtype=F32)

    def in_matrix(bpart):
        bt = jnp.transpose(bpart, (0, 2, 1)).reshape(n_slabs, groups_per_slab, h, p)
        dense = bt[:, :, :, None, :] * eye[None, :, None, :, None]
        dense = dense.reshape(n_slabs, LANES, pairs_per_slab, 2 * p)
        return jnp.transpose(dense, (0, 2, 1, 3)).reshape(SSM_PAIRS, LANES, 2 * p)

    def out_matrix(cpart):
        ct = jnp.transpose(cpart, (0, 2, 1)).reshape(n_slabs, groups_per_slab, p, h)
        dense = ct[:, :, :, None, :] * eye[None, :, None, :, None]
        return dense.reshape(SSM_PAIRS, 2 * p, LANES)

    bm = jnp.concatenate([in_matrix(bbar_re), in_matrix(bbar_im)], axis=2)
    cm = jnp.concatenate([out_matrix(c_re.astype(F32)), out_matrix(-c_im.astype(F32))], axis=1)

    def bcast(apart):
        return jnp.broadcast_to(apart.reshape(SSM_PAIRS, 1, 2 * p), (SSM_PAIRS, bsz, 2 * p))

    return bm.astype(BF16), bcast(abar_re), bcast(abar_im), cm.astype(BF16)


def _layer_norm(z, g, b):
    mu = jnp.mean(z, axis=-1, keepdims=True)
    zc = z - mu
    var = jnp.mean(zc * zc, axis=-1, keepdims=True)
    return zc * lax.rsqrt(var + LN_EPS) * g + b


def _mix_ffn_kernel(x_ref, attn_ref, g_ref, wga_ref, wgs_ref, bgate_ref, wattn_ref, wglu_ref,
                    wssm_ref, wout_ref, ln1g_ref, ln1b_ref, wfg_ref, wfu_ref, wfd_ref,
                    ln2g_ref, ln2b_ref, o_ref, *, alpha, sub):
    dm = x_ref.shape[1]
    tiles = [slice(r0, r0 + sub) for r0 in range(0, x_ref.shape[0], sub)]
    dot = functools.partial(jnp.dot, preferred_element_type=F32)
    y_s = [None] * len(tiles)
    for t, rows in enumerate(tiles):
        glu = dot(g_ref[rows, :], wglu_ref[...])
        y_s[t] = (glu[:, :SSM_WIDTH] * jax.nn.sigmoid(glu[:, SSM_WIDTH:])).astype(BF16)
    gated_ssm = [None] * len(tiles)
    gate_attn = [None] * len(tiles)
    for t, rows in enumerate(tiles):
        xb = x_ref[rows, :].astype(BF16)
        gate_attn[t] = jax.nn.sigmoid(dot(xb, wga_ref[...]) + bgate_ref[:, :dm])
        gate_ssm = jax.nn.sigmoid(dot(xb, wgs_ref[...]) + bgate_ref[:, dm:])
        gated_ssm[t] = gate_ssm * dot(y_s[t], wssm_ref[...])
    mixed = [None] * len(tiles)
    for t, rows in enumerate(tiles):
        y_attn = dot(attn_ref[rows, :], wattn_ref[...])
        mixed[t] = (gate_attn[t] * y_attn + gated_ssm[t]).astype(BF16)
    h = [None] * len(tiles)
    for t, rows in enumerate(tiles):
        mix_out = dot(mixed[t], wout_ref[...])
        h[t] = _layer_norm(alpha * x_ref[rows, :] + mix_out, ln1g_ref[...], ln1b_ref[...])
    act = [None] * len(tiles)
    for t, rows in enumerate(tiles):
        hb = h[t].astype(BF16)
        gate = dot(hb, wfg_ref[...])
        up = dot(hb, wfu_ref[...])
        act[t] = ((gate * jax.nn.sigmoid(gate)) * up).astype(BF16)
    for t, rows in enumerate(tiles):
        ff = dot(act[t], wfd_ref[...])
        o_ref[rows, :] = _layer_norm(alpha * h[t] + ff, ln2g_ref[...], ln2b_ref[...])


def _mix_ffn(x2, attn2, g2, consts, alpha, tm=512, sub=256):
    rows, dm = x2.shape
    row_spec = lambda w: pl.BlockSpec((tm, w), lambda i: (i, 0))
    return pl.pallas_call(
        functools.partial(_mix_ffn_kernel, alpha=alpha, sub=sub),
        grid=(rows // tm,),
        in_specs=[row_spec(dm), row_spec(ATTN_WIDTH), row_spec(SSM_WIDTH)]
                 + [_const_spec(c.shape) for c in consts],
        out_specs=row_spec(dm),
        out_shape=jax.ShapeDtypeStruct((rows, dm), F32),
        compiler_params=pltpu.CompilerParams(
            dimension_semantics=("parallel",), vmem_limit_bytes=VMEM_LIMIT),
        name="mix_ffn",
    )(x2, attn2, g2, *consts)


def _rope_tables(s):
    half = HEAD_DIM // 2
    f32 = np.float32
    pos = np.arange(s, dtype=f32)
    inv_freq = f32(ROPE_THETA) ** (-np.arange(half, dtype=f32) / f32(half))
    ang = pos[:, None] * inv_freq[None, :]
    cos, sin = np.cos(ang), np.sin(ang)
    reps = LANES // HEAD_DIM
    cos_t = np.concatenate([cos, cos] * reps, axis=1)
    sin_t = np.concatenate([-sin, sin] * reps, axis=1)
    return cos_t.astype(f32), sin_t.astype(f32)


def kernel(x, w_in, b_gate, w_attn_br, w_ssm_br, w_out, ssm_a_re, ssm_a_im, ssm_log_dt, ssm_b_re, ssm_b_im, ssm_c_re, ssm_c_im, ssm_d, w_glu, ln1_g, ln1_b, w_ff_gate, w_ff_up, w_ff_down, ln2_g, ln2_b):
    bsz, s, dm = x.shape
    depth = w_in.shape[0]
    alpha = (2.0 * depth) ** 0.25
    assert all(w // d == ATTN_BLOCK for w, d in zip(WINDOWS, DILATIONS))
    assert s % (ATTN_BLOCK * max(DILATIONS)) == 0
    cos_t, sin_t = _rope_tables(s)
    assert w_in.shape[2] == 3 * QKV_WIDTH + SSM_WIDTH + 2 * dm

    gate_col0 = (3 * QKV_WIDTH + SSM_WIDTH) // dm

    for layer in range(depth):
        outs, (wga, wgs, wattn, wglu, wssm, wout) = _project(
            x, cos_t, sin_t, w_in[layer],
            [(w_in[layer], (gate_col0, dm)), (w_in[layer], (gate_col0 + 1, dm)),
             (w_attn_br[layer], None), (w_glu[layer], None), (w_ssm_br[layer], None),
             (w_out[layer], None)])
        attn = _attention(outs[:N_GROUPS], bsz, s)

        bm, a_re, a_im, cm = _ssm_matrices(
            ssm_a_re[layer], ssm_a_im[layer], ssm_log_dt[layer], ssm_b_re[layer],
            ssm_b_im[layer], ssm_c_re[layer], ssm_c_im[layer], bsz)
        g, (wff_gate, wff_up, wff_down) = _ssm(
            outs[N_GROUPS], bm, a_re, a_im, cm, ssm_d[layer].reshape(1, SSM_WIDTH).astype(F32),
            [(w_ff_gate[layer], None), (w_ff_up[layer], None), (w_ff_down[layer], None)])

        row = lambda v: v[layer].reshape(1, -1).astype(F32)
        x2 = _mix_ffn(
            x.reshape(bsz * s, dm), attn.reshape(bsz * s, ATTN_WIDTH),
            g.reshape(bsz * s, SSM_WIDTH),
            [wga, wgs, row(b_gate), wattn, wglu, wssm, wout, row(ln1_g), row(ln1_b),
             wff_gate, wff_up, wff_down, row(ln2_g), row(ln2_b)],
            alpha)
        x = x2.reshape(bsz, s, dm)
    return x
```

```python
import functools
import math

import jax
import jax.numpy as jnp
import numpy as np
from jax import lax
from jax.experimental import pallas as pl
from jax.experimental.pallas import tpu as pltpu

F32 = jnp.float32
BF16 = jnp.bfloat16

LANES = 128
HEAD_DIM = 64
ATTN_HEADS = 8
ATTN_WIDTH = ATTN_HEADS * HEAD_DIM
DILATIONS = (1, 4, 16)
WINDOWS = (128, 512, 2048)
N_GROUPS = len(DILATIONS)
QKV_WIDTH = N_GROUPS * ATTN_WIDTH
QA_COL, QB_COL, K_COL, VA_COL, VB_COL = (i * ATTN_WIDTH for i in range(5))
ATTN_IN_WIDTH = 5 * ATTN_WIDTH
ATTN_BLOCK = 128
ROPE_THETA = 10000.0
NEG_INF = -1e30
Q_SCALE = math.log2(math.e) / math.sqrt(HEAD_DIM)
SSM_GROUP = 16
SSM_GROUPS = 32
SSM_WIDTH = SSM_GROUP * SSM_GROUPS
SSM_STATE = 64
SSM_PAIRS = SSM_GROUPS // 2
LN_EPS = 1e-5

VMEM_LIMIT = 56 * 1024 * 1024


def _const_spec(shape):
    zeros = (0,) * len(shape)
    return pl.BlockSpec(shape, lambda *_: zeros, pipeline_mode=pl.Buffered(1))


def _cast_slice_specs(side, n_steps, step_of):
    in_specs, out_specs, out_shapes = [], [], []
    for arr, col in side:
        rows = arr.shape[0] // n_steps
        assert rows * n_steps == arr.shape[0] and rows % 16 == 0
        cidx, width = col if col is not None else (0, arr.shape[1])
        in_specs.append(pl.BlockSpec(
            (rows, width), functools.partial(lambda *ids, cidx: (step_of(*ids), cidx), cidx=cidx)))
        out_specs.append(pl.BlockSpec((rows, width), lambda *ids: (step_of(*ids), 0)))
        out_shapes.append(jax.ShapeDtypeStruct((arr.shape[0], width), BF16))
    return in_specs, out_specs, out_shapes


def _cast_slices(src_refs, dst_refs):
    for src, dst in zip(src_refs, dst_refs):
        dst[...] = src[...].astype(dst.dtype)


def _phase_major(ref, tm, d):
    if d == 1:
        return ref[...]
    n = tm // d
    return jnp.concatenate([ref[pl.ds(r, n, stride=d), :] for r in range(d)], axis=0)


def _block_phase_order(ref, tm):
    dmax = max(DILATIONS)
    tile = ATTN_BLOCK // dmax
    return jnp.concatenate([ref[pl.ds(b0 + r, tile, stride=dmax), :]
                            for b0 in range(0, tm, ATTN_BLOCK) for r in range(dmax)], axis=0)


def _proj_kernel(*refs, tm, n_slabs, n_side):
    refs = list(refs)
    x_refs = [refs.pop(0) for _ in range(n_slabs)]
    cos_ref, sin_ref, w32_ref = [refs.pop(0) for _ in range(3)]
    side_in = [refs.pop(0) for _ in range(n_side)]
    o_refs = [refs.pop(0) for _ in range(N_GROUPS)]
    u_ref = refs.pop(0)
    side_out = [refs.pop(0) for _ in range(n_side)]
    w_ref, = refs

    @pl.when((pl.program_id(0) == 0) & (pl.program_id(1) == 0))
    def _():
        for c0 in range(0, w_ref.shape[1], ATTN_WIDTH):
            w_ref[:, c0:c0 + ATTN_WIDTH] = w32_ref[:, c0:c0 + ATTN_WIDTH].astype(BF16)

    _cast_slices(side_in, side_out)

    lane = lax.broadcasted_iota(jnp.int32, (1, LANES), 1)
    first_half = (lane % HEAD_DIM) < (HEAD_DIM // 2)
    head_a = lane < HEAD_DIM

    for gi, d in enumerate(DILATIONS):
        n = tm // d
        xg = jnp.concatenate(
            [_phase_major(xr, tm, d).astype(BF16) for xr in x_refs], axis=1)
        cos = _phase_major(cos_ref, tm, d)
        sin = _phase_major(sin_ref, tm, d)
        if gi == 0:
            u_ref[...] = jnp.dot(xg, w_ref[:, 3 * QKV_WIDTH:], preferred_element_type=F32)
        for part in range(3):
            c0 = part * ATTN_WIDTH
            w0 = part * QKV_WIDTH + gi * ATTN_WIDTH
            lhs, rope_cos, rope_sin = xg, cos, sin
            if gi == 0 and part == 0:
                lhs = jnp.concatenate(
                    [_block_phase_order(xr, tm).astype(BF16) for xr in x_refs], axis=1)
                rope_cos = _block_phase_order(cos_ref, tm)
                rope_sin = _block_phase_order(sin_ref, tm)
            acc = jnp.dot(lhs, w_ref[:, w0:w0 + ATTN_WIDTH], preferred_element_type=F32)
            for c in range(ATTN_WIDTH // LANES):
                t = acc[:, c * LANES:(c + 1) * LANES]
                if part < 2:
                    rot = jnp.where(first_half,
                                    pltpu.roll(t, LANES - HEAD_DIM // 2, 1),
                                    pltpu.roll(t, HEAD_DIM // 2, 1))
                    t = t * rope_cos + rot * rope_sin
                if part == 0:
                    t = t * Q_SCALE
                if part == 1:
                    pieces = [(K_COL, t)]
                else:
                    col_a, col_b = (QA_COL, QB_COL) if part == 0 else (VA_COL, VB_COL)
                    pieces = [(col_a, jnp.where(head_a, t, 0.0)), (col_b, jnp.where(head_a, 0.0, t))]
                for col0, piece in pieces:
                    piece = piece.astype(BF16)
                    col = col0 + c * LANES
                    if d == 1:
                        o_refs[gi][:, col:col + LANES] = piece
                    else:
                        for r in range(d):
                            o_refs[gi][r, :, col:col + LANES] = piece[r * n:(r + 1) * n]


def _project(x, cos_t, sin_t, w_in, side, tm=512):
    b, s, dm = x.shape
    n_slabs = dm // LANES
    n_proj = 3 * QKV_WIDTH + SSM_WIDTH
    grid = (b, s // tm)
    side_in, side_out, side_shapes = _cast_slice_specs(
        side, b * (s // tm), lambda bi, i: bi * (s // tm) + i)
    x_specs = [pl.BlockSpec((None, tm, LANES), functools.partial(lambda bi, i, c: (bi, i, c), c=c))
               for c in range(n_slabs)]
    tab_spec = pl.BlockSpec((tm, LANES), lambda bi, i: (i, 0))
    out_shapes = []
    out_specs = []
    for d in DILATIONS:
        if d == 1:
            out_shapes.append(jax.ShapeDtypeStruct((b, s, ATTN_IN_WIDTH), BF16))
            out_specs.append(pl.BlockSpec((None, tm, ATTN_IN_WIDTH), lambda bi, i: (bi, i, 0)))
        else:
            out_shapes.append(jax.ShapeDtypeStruct((b, d, s // d, ATTN_IN_WIDTH), BF16))
            out_specs.append(pl.BlockSpec((None, d, tm // d, ATTN_IN_WIDTH),
                                          lambda bi, i: (bi, 0, i, 0)))
    out_shapes.append(jax.ShapeDtypeStruct((b, s, SSM_WIDTH), F32))
    out_specs.append(pl.BlockSpec((None, tm, SSM_WIDTH), lambda bi, i: (bi, i, 0)))
    outs = pl.pallas_call(
        functools.partial(_proj_kernel, tm=tm, n_slabs=n_slabs, n_side=len(side)),
        grid=grid,
        in_specs=x_specs + [tab_spec, tab_spec,
                            pl.BlockSpec((dm, n_proj), lambda bi, i: (0, 0),
                                         pipeline_mode=pl.Buffered(1))] + side_in,
        out_specs=out_specs + side_out,
        out_shape=out_shapes + side_shapes,
        scratch_shapes=[pltpu.VMEM((dm, n_proj), BF16)],
        compiler_params=pltpu.CompilerParams(
            dimension_semantics=("arbitrary", "arbitrary"), vmem_limit_bytes=VMEM_LIMIT),
        name="in_proj",
    )(*([x] * n_slabs), cos_t, sin_t, w_in, *[arr for arr, _ in side])
    n_main = N_GROUPS + 1
    return outs[:n_main], outs[n_main:]


def _attn_bias():
    blk = ATTN_BLOCK
    c = np.arange(2 * blk)[None, :]

    def tables(position):
        a = np.tile(position, 2)[:, None]
        in_window = np.where(c < blk, c >= a, c - blk <= a)
        windowed = np.where(in_window, 0.0, NEG_INF).astype(np.float32)
        causal = np.where(c[:, :blk] <= a, 0.0, NEG_INF).astype(np.float32)
        return windowed, causal

    return tables(np.arange(blk)), tables(_PHASE_ORDER)


_PHASE_ORDER = np.arange(ATTN_BLOCK).reshape(-1, max(DILATIONS)).T.reshape(-1)


def _attn_kernel(*refs, s, lags, pairs):
    n_in = 5 * N_GROUPS
    qkv = refs[:n_in]
    mask_w_ref, mask_c_ref, pmask_w_ref, pmask_c_ref, ones_ref = refs[n_in:n_in + 5]
    o_ref = refs[n_in + 5]
    acc_s, m_s, l_s, merged_s = refs[n_in + 6:]
    blk = ATTN_BLOCK
    n_blocks = s // blk
    dmax = max(DILATIONS)
    tile = blk // dmax

    lane = lax.broadcasted_iota(jnp.int32, (1, LANES), 1)
    head_a = lane < HEAD_DIM

    def window(ref, i, nblk, hp):
        start = i if i % nblk == 0 else i - 1
        return ref[start * blk:(i + 1) * blk, hp * LANES:(hp + 1) * LANES]

    def scores(gi, i, nblk, hp):
        qa_ref, qb_ref, k_ref = qkv[5 * gi:5 * gi + 3]
        if gi == 0:
            mask = pmask_c_ref[...] if i % nblk == 0 else pmask_w_ref[...]
        else:
            mask = mask_c_ref[...] if i % nblk == 0 else mask_w_ref[...]
        rows, lanes = slice(i * blk, (i + 1) * blk), slice(hp * LANES, (hp + 1) * LANES)
        q2 = jnp.concatenate([qa_ref[rows, lanes], qb_ref[rows, lanes]], axis=0)
        return lax.dot_general(q2, window(k_ref, i, nblk, hp), (((1,), (1,)), ((), ())),
                               preferred_element_type=F32) + mask

    def softmax(sc):
        m = jnp.max(sc, axis=1, keepdims=True)
        return m, jnp.exp2((sc - m).astype(BF16))

    def finish(gi, d, i, nblk, hp, m, p):
        va, vb = (window(ref, i, nblk, hp) for ref in qkv[5 * gi + 3:5 * gi + 5])
        w = va.shape[0]
        v_ext = jnp.concatenate(
            [jnp.concatenate([va, ones_ref[0, :w, :]], axis=1),
             jnp.concatenate([vb, ones_ref[1, :w, :]], axis=1)], axis=0)
        p2 = jnp.concatenate([p[:blk], p[blk:]], axis=1)
        res = jnp.dot(p2, v_ext, preferred_element_type=F32)
        acc, l = res[:, :LANES], res[:, LANES:]
        m = jnp.where(head_a, m[:blk], m[blk:])
        rows = slice(i * blk, (i + 1) * blk)
        if d > 1:
            acc_s[hp, gi - 1, rows, :] = acc
            l_s[hp, gi - 1, rows, :] = l
            m_s[hp, gi - 1, rows, :] = m
            return

        def parked(ref, g):
            dg = DILATIONS[g + 1]
            tiles = []
            for r in range(dmax):
                start = (r % dg) * (s // dg) + (blk // dg) * i + r // dg
                idx = pl.ds(start, tile, stride=dmax // dg) if dg < dmax else pl.ds(start, tile)
                tiles.append(ref[hp, g, idx, :])
            return jnp.concatenate(tiles, axis=0)

        parts = [(m, l, acc)] + [(parked(m_s, g), parked(l_s, g), parked(acc_s, g))
                                 for g in range(N_GROUPS - 1)]
        m_all = functools.reduce(jnp.maximum, [pm for pm, _, _ in parts])
        num = jnp.zeros((blk, LANES), F32)
        den = jnp.zeros((blk, LANES), F32)
        for pm, pl_, pacc in parts:
            wgt = jnp.exp2(pm - m_all)
            num = num + wgt * pacc
            den = den + wgt * pl_
        merged_s[hp, i] = num / den
        natural = [merged_s[hp, i, pl.ds((a0 % dmax) * tile + a0 // dmax, tile, stride=tile), :]
                   for a0 in range(0, blk, tile)]
        o_ref[rows, hp * LANES:(hp + 1) * LANES] = (
            jnp.concatenate(natural, axis=0).astype(o_ref.dtype))

    assert DILATIONS[0] == 1
    block = lambda gi, i: (gi, DILATIONS[gi], i, (s // DILATIONS[gi]) // blk)
    ready = [0] * n_blocks
    dilated = []
    for gi in reversed(range(1, N_GROUPS)):
        nblk = (s // DILATIONS[gi]) // blk
        order = sorted(range(n_blocks), key=lambda b: (b % nblk, b // nblk))
        for pos, b in enumerate(order):
            dilated.append(block(gi, b))
            for i in range(n_blocks):
                if i * nblk // n_blocks == b % nblk:
                    ready[i] = max(ready[i], len(dilated))
    work, done = [], 0
    for i in range(n_blocks):
        take = max(ready[i] - done, 1 if done < len(dilated) else 0)
        work += dilated[done:done + take]
        done += take
        work.append(block(0, i))
    work += dilated[done:]
    work = [item + (hp,) for hp in range(pairs) for item in work]
    lag_softmax, lag_values = lags
    sc_q, p_q = {}, {}
    for step in range(len(work) + lag_values):
        if step < len(work):
            gi, d, i, nblk, hp = work[step]
            sc_q[step] = scores(gi, i, nblk, hp)
        if 0 <= step - lag_softmax < len(work):
            p_q[step - lag_softmax] = softmax(sc_q.pop(step - lag_softmax))
        if step >= lag_values:
            gi, d, i, nblk, hp = work[step - lag_values]
            finish(gi, d, i, nblk, hp, *p_q.pop(step - lag_values))


def _attention(qkv_groups, b, s, lags=(1, 4), pairs=2):
    width = pairs * LANES
    in_specs, args = [], []
    for gi, d in enumerate(DILATIONS):
        arr = qkv_groups[gi].reshape(b, s, ATTN_IN_WIDTH)
        for col0 in (QA_COL, QB_COL, K_COL, VA_COL, VB_COL):
            in_specs.append(pl.BlockSpec(
                (None, s, width),
                functools.partial(lambda bi, hp, c: (bi, 0, c + hp), c=col0 // width)))
            args.append(arr)
    (mask_w, mask_c), (pmask_w, pmask_c) = _attn_bias()
    head_ones = np.stack([np.arange(LANES) < HEAD_DIM, np.arange(LANES) >= HEAD_DIM])
    head_ones = jnp.asarray(np.broadcast_to(head_ones[:, None, :], (2, 2 * ATTN_BLOCK, LANES)),
                            dtype=BF16)
    masks = [mask_w, mask_c, pmask_w, pmask_c, head_ones]
    return pl.pallas_call(
        functools.partial(_attn_kernel, s=s, lags=lags, pairs=pairs),
        grid=(b, ATTN_WIDTH // width),
        in_specs=in_specs + [_const_spec(t.shape) for t in masks],
        out_specs=pl.BlockSpec((None, s, width), lambda bi, hp: (bi, 0, hp)),
        out_shape=jax.ShapeDtypeStruct((b, s, ATTN_WIDTH), BF16),
        scratch_shapes=[pltpu.VMEM((pairs, N_GROUPS - 1, s, LANES), F32)] * 3
                       + [pltpu.VMEM((pairs, s // ATTN_BLOCK, ATTN_BLOCK, LANES), F32)],
        compiler_params=pltpu.CompilerParams(
            dimension_semantics=("parallel", "parallel"), vmem_limit_bytes=VMEM_LIMIT),
        name="attention",
    )(*args, *masks)


def _gelu_tanh(x):
    return 0.5 * x * (1.0 + jnp.tanh(math.sqrt(2.0 / math.pi) * (x + 0.044715 * (x * x * x))))


def _ssm_kernel(*refs, tc, bsz, n_side):
    u_ref, bm_ref, are_ref, aim_ref, cm_ref, dskip_ref = refs[:6]
    o_ref = refs[6 + n_side]
    st_a, st_b, ub_s, y_s, h_ref = refs[7 + 2 * n_side:]
    _cast_slices(refs[6:6 + n_side], refs[7 + n_side:7 + 2 * n_side])
    rows = tc * bsz
    half = SSM_PAIRS // 2
    pairs_per_slab = LANES // (2 * SSM_GROUP)
    n_slabs = SSM_WIDTH // LANES
    splits = 2
    n_slices = half * splits
    steps = tc // n_slices
    srows = rows // splits

    @pl.when(pl.program_id(0) == 0)
    def _():
        h_ref[...] = jnp.zeros_like(h_ref)

    for c in range(n_slabs):
        cols = slice(c * LANES, (c + 1) * LANES)
        for b in range(bsz):
            y_s[c, pl.ds(b, tc, stride=bsz), :] = u_ref[b, :, cols]
        u = y_s[c]
        ub_s[c] = u.astype(BF16)
        y_s[c] = dskip_ref[:, cols] * u

    def project_in(buf, j, jl, rs):
        bu = jnp.dot(ub_s[j // pairs_per_slab, rs, :], bm_ref[j], preferred_element_type=F32)
        buf[0, jl, rs, :] = bu[:, :LANES]
        buf[1, jl, rs, :] = bu[:, LANES:]

    def project_out(buf, j, jl, rs):
        st = jnp.concatenate([buf[0, jl, rs, :].astype(BF16), buf[1, jl, rs, :].astype(BF16)],
                             axis=1)
        y_s[j // pairs_per_slab, rs, :] += jnp.dot(st, cm_ref[j], preferred_element_type=F32)

    def scan(buf, j0, other):
        a_re = [are_ref[j0 + i] for i in range(half)]
        a_im = [aim_ref[j0 + i] for i in range(half)]

        def body(it, carry):
            h_re, h_im = carry
            for k in range(steps):
                r0 = pl.multiple_of((it * steps + k) * bsz, bsz)
                new_re, new_im = [], []
                for i in range(half):
                    n_re = a_re[i] * h_re[i] - a_im[i] * h_im[i] + buf[0, i, pl.ds(r0, bsz), :]
                    n_im = a_re[i] * h_im[i] + a_im[i] * h_re[i] + buf[1, i, pl.ds(r0, bsz), :]
                    buf[0, i, pl.ds(r0, bsz), :] = n_re
                    buf[1, i, pl.ds(r0, bsz), :] = n_im
                    new_re.append(n_re)
                    new_im.append(n_im)
                h_re, h_im = tuple(new_re), tuple(new_im)
            other(it // splits, pl.ds((it % splits) * srows, srows))
            return h_re, h_im

        carry = (tuple(h_ref[0, j0 + i] for i in range(half)),
                 tuple(h_ref[1, j0 + i] for i in range(half)))
        for it in range(n_slices):
            carry = body(it, carry)
        h_re, h_im = carry
        for i in range(half):
            h_ref[0, j0 + i] = h_re[i]
            h_ref[1, j0 + i] = h_im[i]

    for jl in range(half):
        project_in(st_a, jl, jl, slice(None))
    scan(st_a, 0, lambda jl, rs: project_in(st_b, half + jl, jl, rs))
    scan(st_b, half, lambda jl, rs: project_out(st_a, jl, jl, rs))
    for jl in range(half):
        project_out(st_b, half + jl, jl, slice(None))

    for c in range(n_slabs):
        for b in range(bsz):
            y = y_s[c, pl.ds(b, tc, stride=bsz), :]
            o_ref[b, :, c * LANES:(c + 1) * LANES] = _gelu_tanh(y).astype(o_ref.dtype)


def _ssm(u, bm, a_re, a_im, cm, d_skip, side, tc=128):
    bsz, s, _ = u.shape
    rows = tc * bsz
    half = SSM_PAIRS // 2
    chunk_spec = pl.BlockSpec((bsz, tc, SSM_WIDTH), lambda i: (0, i, 0))
    side_in, side_out, side_shapes = _cast_slice_specs(side, s // tc, lambda i: i)
    outs = pl.pallas_call(
        functools.partial(_ssm_kernel, tc=tc, bsz=bsz, n_side=len(side)),
        grid=(s // tc,),
        in_specs=[chunk_spec,
                  _const_spec(bm.shape), _const_spec(a_re.shape), _const_spec(a_im.shape),
                  _const_spec(cm.shape), _const_spec(d_skip.shape)] + side_in,
        out_specs=[chunk_spec] + side_out,
        out_shape=[jax.ShapeDtypeStruct((bsz, s, SSM_WIDTH), BF16)] + side_shapes,
        scratch_shapes=[pltpu.VMEM((2, half, rows, LANES), F32),
                        pltpu.VMEM((2, half, rows, LANES), F32),
                        pltpu.VMEM((SSM_WIDTH // LANES, rows, LANES), BF16),
                        pltpu.VMEM((SSM_WIDTH // LANES, rows, LANES), F32),
                        pltpu.VMEM((2, SSM_PAIRS, bsz, LANES), F32)],
        compiler_params=pltpu.CompilerParams(
            dimension_semantics=("arbitrary",), vmem_limit_bytes=VMEM_LIMIT),
        name="ssm",
    )(u, bm, a_re, a_im, cm, d_skip, *[arr for arr, _ in side])
    return outs[0], outs[1:]


def _ssm_matrices(a_re, a_im, log_dt, b_re, b_im, c_re, c_im, bsz):
    g, p, h = SSM_GROUPS, SSM_STATE, SSM_GROUP
    lam_re, lam_im = a_re.astype(F32), a_im.astype(F32)
    dt = jnp.exp(log_dt.astype(F32))[:, None]
    mag = jnp.exp(lam_re * dt)
    abar_re, abar_im = mag * jnp.cos(lam_im * dt), mag * jnp.sin(lam_im * dt)
    num_re, num_im = abar_re - 1.0, abar_im
    den = lam_re * lam_re + lam_im * lam_im
    coef_re = ((num_re * lam_re + num_im * lam_im) / den)[..., None]
    coef_im = ((num_im * lam_re - num_re * lam_im) / den)[..., None]
    b_re, b_im = b_re.astype(F32), b_im.astype(F32)
    bbar_re = coef_re * b_re - coef_im * b_im
    bbar_im = coef_re * b_im + coef_im * b_re

    groups_per_slab = LANES // h
    pairs_per_slab = groups_per_slab // 2
    n_slabs = g // groups_per_slab
    eye = jnp.eye(groups_per_slab, dtype=F32)

    def in_matrix(bpart):
        bt = jnp.transpose(bpart, (0, 2, 1)).reshape(n_slabs, groups_per_slab, h, p)
        dense = bt[:, :, :, None, :] * eye[None, :, None, :, None]
        dense = dense.reshape(n_slabs, LANES, pairs_per_slab, 2 * p)
        return jnp.transpose(dense, (0, 2, 1, 3)).reshape(SSM_PAIRS, LANES, 2 * p)

    def out_matrix(cpart):
        ct = jnp.transpose(cpart, (0, 2, 1)).reshape(n_slabs, groups_per_slab, p, h)
        dense = ct[:, :, :, None, :] * eye[None, :, None, :, None]
        return dense.reshape(SSM_PAIRS, 2 * p, LANES)

    bm = jnp.concatenate([in_matrix(bbar_re), in_matrix(bbar_im)], axis=2)
    cm = jnp.concatenate([out_matrix(c_re.astype(F32)), out_matrix(-c_im.astype(F32))], axis=1)

    def bcast(apart):
        return jnp.broadcast_to(apart.reshape(SSM_PAIRS, 1, 2 * p), (SSM_PAIRS, bsz, 2 * p))

    return bm.astype(BF16), bcast(abar_re), bcast(abar_im), cm.astype(BF16)


def _layer_norm(z, g, b):
    mu = jnp.mean(z, axis=-1, keepdims=True)
    zc = z - mu
    var = jnp.mean(zc * zc, axis=-1, keepdims=True)
    return zc * lax.rsqrt(var + LN_EPS) * g + b


def _mix_ffn_kernel(x_ref, attn_ref, g_ref, wga_ref, wgs_ref, bgate_ref, wattn_ref, wglu_ref,
                    wssm_ref, wout_ref, ln1g_ref, ln1b_ref, wfg_ref, wfu_ref, wfd_ref,
                    ln2g_ref, ln2b_ref, o_ref, *, alpha, sub):
    dm = x_ref.shape[1]
    tiles = [slice(r0, r0 + sub) for r0 in range(0, x_ref.shape[0], sub)]
    dot = functools.partial(jnp.dot, preferred_element_type=F32)
    y_s = [None] * len(tiles)
    for t, rows in enumerate(tiles):
        glu = dot(g_ref[rows, :], wglu_ref[...])
        y_s[t] = (glu[:, :SSM_WIDTH] * jax.nn.sigmoid(glu[:, SSM_WIDTH:])).astype(BF16)
    gated_ssm = [None] * len(tiles)
    gate_attn = [None] * len(tiles)
    for t, rows in enumerate(tiles):
        xb = x_ref[rows, :].astype(BF16)
        gate_attn[t] = jax.nn.sigmoid(dot(xb, wga_ref[...]) + bgate_ref[:, :dm])
        gate_ssm = jax.nn.sigmoid(dot(xb, wgs_ref[...]) + bgate_ref[:, dm:])
        gated_ssm[t] = gate_ssm * dot(y_s[t], wssm_ref[...])
    mixed = [None] * len(tiles)
    for t, rows in enumerate(tiles):
        y_attn = dot(attn_ref[rows, :], wattn_ref[...])
        mixed[t] = (gate_attn[t] * y_attn + gated_ssm[t]).astype(BF16)
    h = [None] * len(tiles)
    for t, rows in enumerate(tiles):
        mix_out = dot(mixed[t], wout_ref[...])
        h[t] = _layer_norm(alpha * x_ref[rows, :] + mix_out, ln1g_ref[...], ln1b_ref[...])
    act = [None] * len(tiles)
    for t, rows in enumerate(tiles):
        hb = h[t].astype(BF16)
        gate = dot(hb, wfg_ref[...])
        up = dot(hb, wfu_ref[...])
        act[t] = ((gate * jax.nn.sigmoid(gate)) * up).astype(BF16)
    for t, rows in enumerate(tiles):
        ff = dot(act[t], wfd_ref[...])
        o_ref[rows, :] = _layer_norm(alpha * h[t] + ff, ln2g_ref[...], ln2b_ref[...])


def _mix_ffn(x2, attn2, g2, consts, alpha, tm=512, sub=256):
    rows, dm = x2.shape
    row_spec = lambda w: pl.BlockSpec((tm, w), lambda i: (i, 0))
    return pl.pallas_call(
        functools.partial(_mix_ffn_kernel, alpha=alpha, sub=sub),
        grid=(rows // tm,),
        in_specs=[row_spec(dm), row_spec(ATTN_WIDTH), row_spec(SSM_WIDTH)]
                 + [_const_spec(c.shape) for c in consts],
        out_specs=row_spec(dm),
        out_shape=jax.ShapeDtypeStruct((rows, dm), F32),
        compiler_params=pltpu.CompilerParams(
            dimension_semantics=("parallel",), vmem_limit_bytes=VMEM_LIMIT),
        name="mix_ffn",
    )(x2, attn2, g2, *consts)


def _rope_tables(s):
    half = HEAD_DIM // 2
    f32 = np.float32
    pos = np.arange(s, dtype=f32)
    inv_freq = f32(ROPE_THETA) ** (-np.arange(half, dtype=f32) / f32(half))
    ang = pos[:, None] * inv_freq[None, :]
    cos, sin = np.cos(ang), np.sin(ang)
    reps = LANES // HEAD_DIM
    cos_t = np.concatenate([cos, cos] * reps, axis=1)
    sin_t = np.concatenate([-sin, sin] * reps, axis=1)
    return cos_t.astype(f32), sin_t.astype(f32)


def kernel(x, w_in, b_gate, w_attn_br, w_ssm_br, w_out, ssm_a_re, ssm_a_im, ssm_log_dt, ssm_b_re, ssm_b_im, ssm_c_re, ssm_c_im, ssm_d, w_glu, ln1_g, ln1_b, w_ff_gate, w_ff_up, w_ff_down, ln2_g, ln2_b):
    bsz, s, dm = x.shape
    depth = w_in.shape[0]
    alpha = (2.0 * depth) ** 0.25
    assert all(w // d == ATTN_BLOCK for w, d in zip(WINDOWS, DILATIONS))
    assert s % (ATTN_BLOCK * max(DILATIONS)) == 0
    cos_t, sin_t = _rope_tables(s)
    assert w_in.shape[2] == 3 * QKV_WIDTH + SSM_WIDTH + 2 * dm

    gate_col0 = (3 * QKV_WIDTH + SSM_WIDTH) // dm

    for layer in range(depth):
        outs, (wga, wgs, wattn, wglu, wssm, wout) = _project(
            x, cos_t, sin_t, w_in[layer],
            [(w_in[layer], (gate_col0, dm)), (w_in[layer], (gate_col0 + 1, dm)),
             (w_attn_br[layer], None), (w_glu[layer], None), (w_ssm_br[layer], None),
             (w_out[layer], None)])
        attn = _attention(outs[:N_GROUPS], bsz, s)

        bm, a_re, a_im, cm = _ssm_matrices(
            ssm_a_re[layer], ssm_a_im[layer], ssm_log_dt[layer], ssm_b_re[layer],
            ssm_b_im[layer], ssm_c_re[layer], ssm_c_im[layer], bsz)
        g, (wff_gate, wff_up, wff_down) = _ssm(
            outs[N_GROUPS], bm, a_re, a_im, cm, ssm_d[layer].reshape(1, SSM_WIDTH).astype(F32),
            [(w_ff_gate[layer], None), (w_ff_up[layer], None), (w_ff_down[layer], None)])

        row = lambda v: v[layer].reshape(1, -1).astype(F32)
        x2 = _mix_ffn(
            x.reshape(bsz * s, dm), attn.reshape(bsz * s, ATTN_WIDTH),
            g.reshape(bsz * s, SSM_WIDTH),
            [wga, wgs, row(b_gate), wattn, wglu, wssm, wout, row(ln1_g), row(ln1_b),
             wff_gate, wff_up, wff_down, row(ln2_g), row(ln2_b)],
            alpha)
        x = x2.reshape(bsz, s, dm)
    return x
```

```python
import functools
import math

import jax
import jax.numpy as jnp
import numpy as np
from jax import lax
from jax.experimental import pallas as pl
from jax.experimental.pallas import tpu as pltpu

F32 = jnp.float32
BF16 = jnp.bfloat16

LANES = 128
HEAD_DIM = 64
ATTN_HEADS = 8
ATTN_WIDTH = ATTN_HEADS * HEAD_DIM
DILATIONS = (1, 4, 16)
WINDOWS = (128, 512, 2048)
N_GROUPS = len(DILATIONS)
QKV_WIDTH = N_GROUPS * ATTN_WIDTH
QA_COL, QB_COL, K_COL, VA_COL, VB_COL = (i * ATTN_WIDTH for i in range(5))
ATTN_IN_WIDTH = 5 * ATTN_WIDTH
ATTN_BLOCK = 128
ROPE_THETA = 10000.0
NEG_INF = -1e30
Q_SCALE = math.log2(math.e) / math.sqrt(HEAD_DIM)
SSM_GROUP = 16
SSM_GROUPS = 32
SSM_WIDTH = SSM_GROUP * SSM_GROUPS
SSM_STATE = 64
SSM_PAIRS = SSM_GROUPS // 2
LN_EPS = 1e-5

VMEM_LIMIT = 56 * 1024 * 1024


def _const_spec(shape):
    zeros = (0,) * len(shape)
    return pl.BlockSpec(shape, lambda *_: zeros, pipeline_mode=pl.Buffered(1))


def _cast_slice_specs(side, n_steps, step_of):
    in_specs, out_specs, out_shapes = [], [], []
    for arr, col in side:
        rows = arr.shape[0] // n_steps
        assert rows * n_steps == arr.shape[0] and rows % 16 == 0
        cidx, width = col if col is not None else (0, arr.shape[1])
        in_specs.append(pl.BlockSpec(
            (rows, width), functools.partial(lambda *ids, cidx: (step_of(*ids), cidx), cidx=cidx)))
        out_specs.append(pl.BlockSpec((rows, width), lambda *ids: (step_of(*ids), 0)))
        out_shapes.append(jax.ShapeDtypeStruct((arr.shape[0], width), BF16))
    return in_specs, out_specs, out_shapes


def _cast_slices(src_refs, dst_refs):
    for src, dst in zip(src_refs, dst_refs):
        dst[...] = src[...].astype(dst.dtype)


def _phase_major(ref, tm, d):
    if d == 1:
        return ref[...]
    n = tm // d
    return jnp.concatenate([ref[pl.ds(r, n, stride=d), :] for r in range(d)], axis=0)


def _block_phase_order(ref, tm):
    dmax = max(DILATIONS)
    tile = ATTN_BLOCK // dmax
    return jnp.concatenate([ref[pl.ds(b0 + r, tile, stride=dmax), :]
                            for b0 in range(0, tm, ATTN_BLOCK) for r in range(dmax)], axis=0)


def _proj_kernel(*refs, tm, n_slabs, n_side):
    refs = list(refs)
    x_refs = [refs.pop(0) for _ in range(n_slabs)]
    cos_ref, sin_ref, w32_ref = [refs.pop(0) for _ in range(3)]
    side_in = [refs.pop(0) for _ in range(n_side)]
    o_refs = [refs.pop(0) for _ in range(N_GROUPS)]
    u_ref = refs.pop(0)
    side_out = [refs.pop(0) for _ in range(n_side)]
    w_ref, = refs

    @pl.when((pl.program_id(0) == 0) & (pl.program_id(1) == 0))
    def _():
        for c0 in range(0, w_ref.shape[1], ATTN_WIDTH):
            w_ref[:, c0:c0 + ATTN_WIDTH] = w32_ref[:, c0:c0 + ATTN_WIDTH].astype(BF16)

    _cast_slices(side_in, side_out)

    lane = lax.broadcasted_iota(jnp.int32, (1, LANES), 1)
    first_half = (lane % HEAD_DIM) < (HEAD_DIM // 2)
    head_a = lane < HEAD_DIM

    for gi, d in enumerate(DILATIONS):
        n = tm // d
        xg = jnp.concatenate(
            [_phase_major(xr, tm, d).astype(BF16) for xr in x_refs], axis=1)
        cos = _phase_major(cos_ref, tm, d)
        sin = _phase_major(sin_ref, tm, d)
        if gi == N_GROUPS - 1:
            x_nat = jnp.concatenate([xr[...].astype(BF16) for xr in x_refs], axis=1)
            u_ref[...] = jnp.dot(x_nat, w_ref[:, 3 * QKV_WIDTH:], preferred_element_type=F32)
        for part in range(3):
            c0 = part * ATTN_WIDTH
            w0 = part * QKV_WIDTH + gi * ATTN_WIDTH
            lhs, rope_cos, rope_sin = xg, cos, sin
            if gi == 0 and part == 0:
                lhs = jnp.concatenate(
                    [_block_phase_order(xr, tm).astype(BF16) for xr in x_refs], axis=1)
                rope_cos = _block_phase_order(cos_ref, tm)
                rope_sin = _block_phase_order(sin_ref, tm)
            acc = jnp.dot(lhs, w_ref[:, w0:w0 + ATTN_WIDTH], preferred_element_type=F32)
            for c in range(ATTN_WIDTH // LANES):
                t = acc[:, c * LANES:(c + 1) * LANES]
                if part < 2:
                    rot = jnp.where(first_half,
                                    pltpu.roll(t, LANES - HEAD_DIM // 2, 1),
                                    pltpu.roll(t, HEAD_DIM // 2, 1))
                    t = t * rope_cos + rot * rope_sin
                if part == 0:
                    t = t * Q_SCALE
                if part == 1:
                    pieces = [(K_COL, t)]
                else:
                    col_a, col_b = (QA_COL, QB_COL) if part == 0 else (VA_COL, VB_COL)
                    pieces = [(col_a, jnp.where(head_a, t, 0.0)), (col_b, jnp.where(head_a, 0.0, t))]
                for col0, piece in pieces:
                    piece = piece.astype(BF16)
                    col = col0 + c * LANES
                    if d == 1:
                        o_refs[gi][:, col:col + LANES] = piece
                    else:
                        for r in range(d):
                            o_refs[gi][r, :, col:col + LANES] = piece[r * n:(r + 1) * n]


def _project(x, cos_t, sin_t, w_in, side, tm=512):
    b, s, dm = x.shape
    n_slabs = dm // LANES
    n_proj = 3 * QKV_WIDTH + SSM_WIDTH
    grid = (b, s // tm)
    side_in, side_out, side_shapes = _cast_slice_specs(
        side, b * (s // tm), lambda bi, i: bi * (s // tm) + i)
    x_specs = [pl.BlockSpec((None, tm, LANES), functools.partial(lambda bi, i, c: (bi, i, c), c=c))
               for c in range(n_slabs)]
    tab_spec = pl.BlockSpec((tm, LANES), lambda bi, i: (i, 0))
    out_shapes = []
    out_specs = []
    for d in DILATIONS:
        if d == 1:
            out_shapes.append(jax.ShapeDtypeStruct((b, s, ATTN_IN_WIDTH), BF16))
            out_specs.append(pl.BlockSpec((None, tm, ATTN_IN_WIDTH), lambda bi, i: (bi, i, 0)))
        else:
            out_shapes.append(jax.ShapeDtypeStruct((b, d, s // d, ATTN_IN_WIDTH), BF16))
            out_specs.append(pl.BlockSpec((None, d, tm // d, ATTN_IN_WIDTH),
                                          lambda bi, i: (bi, 0, i, 0)))
    out_shapes.append(jax.ShapeDtypeStruct((b, s, SSM_WIDTH), F32))
    out_specs.append(pl.BlockSpec((None, tm, SSM_WIDTH), lambda bi, i: (bi, i, 0)))
    outs = pl.pallas_call(
        functools.partial(_proj_kernel, tm=tm, n_slabs=n_slabs, n_side=len(side)),
        grid=grid,
        in_specs=x_specs + [tab_spec, tab_spec,
                            pl.BlockSpec((dm, n_proj), lambda bi, i: (0, 0),
                                         pipeline_mode=pl.Buffered(1))] + side_in,
        out_specs=out_specs + side_out,
        out_shape=out_shapes + side_shapes,
        scratch_shapes=[pltpu.VMEM((dm, n_proj), BF16)],
        compiler_params=pltpu.CompilerParams(
            dimension_semantics=("arbitrary", "arbitrary"), vmem_limit_bytes=VMEM_LIMIT),
        name="in_proj",
    )(*([x] * n_slabs), cos_t, sin_t, w_in, *[arr for arr, _ in side])
    n_main = N_GROUPS + 1
    return outs[:n_main], outs[n_main:]


def _attn_bias():
    blk = ATTN_BLOCK
    c = np.arange(2 * blk)[None, :]

    def tables(position):
        a = np.tile(position, 2)[:, None]
        in_window = np.where(c < blk, c >= a, c - blk <= a)
        windowed = np.where(in_window, 0.0, NEG_INF).astype(np.float32)
        causal = np.where(c[:, :blk] <= a, 0.0, NEG_INF).astype(np.float32)
        return windowed, causal

    return tables(np.arange(blk)), tables(_PHASE_ORDER)


_PHASE_ORDER = np.arange(ATTN_BLOCK).reshape(-1, max(DILATIONS)).T.reshape(-1)


def _attn_kernel(*refs, s, lags):
    n_in = 5 * N_GROUPS
    qkv = refs[:n_in]
    mask_w_ref, mask_c_ref, pmask_w_ref, pmask_c_ref, ones_ref = refs[n_in:n_in + 5]
    o_ref = refs[n_in + 5]
    acc_s, m_s, l_s, merged_s = refs[n_in + 6:]
    blk = ATTN_BLOCK
    n_blocks = s // blk
    dmax = max(DILATIONS)
    tile = blk // dmax

    lane = lax.broadcasted_iota(jnp.int32, (1, LANES), 1)
    head_a = lane < HEAD_DIM

    def window(ref, i, nblk):
        start = i if i % nblk == 0 else i - 1
        return ref[start * blk:(i + 1) * blk, :]

    def scores(gi, i, nblk):
        qa_ref, qb_ref, k_ref = qkv[5 * gi:5 * gi + 3]
        if gi == 0:
            mask = pmask_c_ref[...] if i % nblk == 0 else pmask_w_ref[...]
        else:
            mask = mask_c_ref[...] if i % nblk == 0 else mask_w_ref[...]
        q2 = jnp.concatenate([qa_ref[i * blk:(i + 1) * blk, :], qb_ref[i * blk:(i + 1) * blk, :]],
                             axis=0)
        return lax.dot_general(q2, window(k_ref, i, nblk), (((1,), (1,)), ((), ())),
                               preferred_element_type=F32) + mask

    def softmax(sc):
        m = jnp.max(sc, axis=1, keepdims=True)
        return m, jnp.exp2((sc - m).astype(BF16))

    def finish(gi, d, i, nblk, m, p):
        va, vb = (window(ref, i, nblk) for ref in qkv[5 * gi + 3:5 * gi + 5])
        w = va.shape[0]
        v_ext = jnp.concatenate(
            [jnp.concatenate([va, ones_ref[0, :w, :]], axis=1),
             jnp.concatenate([vb, ones_ref[1, :w, :]], axis=1)], axis=0)
        p2 = jnp.concatenate([p[:blk], p[blk:]], axis=1)
        res = jnp.dot(p2, v_ext, preferred_element_type=F32)
        acc, l = res[:, :LANES], res[:, LANES:]
        m = jnp.where(head_a, m[:blk], m[blk:])
        rows = slice(i * blk, (i + 1) * blk)
        if d > 1:
            acc_s[gi - 1, rows, :] = acc
            l_s[gi - 1, rows, :] = l
            m_s[gi - 1, rows, :] = m
            return

        def parked(ref, g):
            dg = DILATIONS[g + 1]
            tiles = []
            for r in range(dmax):
                start = (r % dg) * (s // dg) + (blk // dg) * i + r // dg
                idx = pl.ds(start, tile, stride=dmax // dg) if dg < dmax else pl.ds(start, tile)
                tiles.append(ref[g, idx, :])
            return jnp.concatenate(tiles, axis=0)

        parts = [(m, l, acc)] + [(parked(m_s, g), parked(l_s, g), parked(acc_s, g))
                                 for g in range(N_GROUPS - 1)]
        m_all = functools.reduce(jnp.maximum, [pm for pm, _, _ in parts])
        num = jnp.zeros((blk, LANES), F32)
        den = jnp.zeros((blk, LANES), F32)
        for pm, pl_, pacc in parts:
            wgt = jnp.exp2(pm - m_all)
            num = num + wgt * pacc
            den = den + wgt * pl_
        merged_s[i] = num / den
        natural = [merged_s[i, pl.ds((a0 % dmax) * tile + a0 // dmax, tile, stride=tile), :]
                   for a0 in range(0, blk, tile)]
        o_ref[rows, :] = jnp.concatenate(natural, axis=0).astype(o_ref.dtype)

    assert DILATIONS[0] == 1
    block = lambda gi, i: (gi, DILATIONS[gi], i, (s // DILATIONS[gi]) // blk)
    ready = [0] * n_blocks
    dilated = []
    for gi in reversed(range(1, N_GROUPS)):
        nblk = (s // DILATIONS[gi]) // blk
        order = sorted(range(n_blocks), key=lambda b: (b % nblk, b // nblk))
        for pos, b in enumerate(order):
            dilated.append(block(gi, b))
            for i in range(n_blocks):
                if i * nblk // n_blocks == b % nblk:
                    ready[i] = max(ready[i], len(dilated))
    work, done = [], 0
    for i in range(n_blocks):
        take = max(ready[i] - done, 1 if done < len(dilated) else 0)
        work += dilated[done:done + take]
        done += take
        work.append(block(0, i))
    work += dilated[done:]
    lag_softmax, lag_values = lags
    sc_q, p_q = {}, {}
    for step in range(len(work) + lag_values):
        if step < len(work):
            gi, d, i, nblk = work[step]
            sc_q[step] = scores(gi, i, nblk)
        if 0 <= step - lag_softmax < len(work):
            p_q[step - lag_softmax] = softmax(sc_q.pop(step - lag_softmax))
        if step >= lag_values:
            gi, d, i, nblk = work[step - lag_values]
            finish(gi, d, i, nblk, *p_q.pop(step - lag_values))


def _attention(qkv_groups, b, s, lags=(1, 4)):
    n_hp = ATTN_WIDTH // LANES
    in_specs, args = [], []
    for gi, d in enumerate(DILATIONS):
        arr = qkv_groups[gi].reshape(b, s, ATTN_IN_WIDTH)
        for col0 in (QA_COL, QB_COL, K_COL, VA_COL, VB_COL):
            in_specs.append(pl.BlockSpec(
                (None, s, LANES),
                functools.partial(lambda bi, hp, c: (bi, 0, c + hp), c=col0 // LANES)))
            args.append(arr)
    (mask_w, mask_c), (pmask_w, pmask_c) = _attn_bias()
    head_ones = np.stack([np.arange(LANES) < HEAD_DIM, np.arange(LANES) >= HEAD_DIM])
    head_ones = jnp.asarray(np.broadcast_to(head_ones[:, None, :], (2, 2 * ATTN_BLOCK, LANES)),
                            dtype=BF16)
    masks = [mask_w, mask_c, pmask_w, pmask_c, head_ones]
    return pl.pallas_call(
        functools.partial(_attn_kernel, s=s, lags=lags),
        grid=(b, n_hp),
        in_specs=in_specs + [_const_spec(t.shape) for t in masks],
        out_specs=pl.BlockSpec((None, s, LANES), lambda bi, hp: (bi, 0, hp)),
        out_shape=jax.ShapeDtypeStruct((b, s, ATTN_WIDTH), BF16),
        scratch_shapes=[pltpu.VMEM((N_GROUPS - 1, s, LANES), F32)] * 3
                       + [pltpu.VMEM((s // ATTN_BLOCK, ATTN_BLOCK, LANES), F32)],
        compiler_params=pltpu.CompilerParams(
            dimension_semantics=("parallel", "parallel"), vmem_limit_bytes=VMEM_LIMIT),
        name="attention",
    )(*args, *masks)


def _gelu_tanh(x):
    return 0.5 * x * (1.0 + jnp.tanh(math.sqrt(2.0 / math.pi) * (x + 0.044715 * (x * x * x))))


def _ssm_kernel(*refs, tc, bsz, n_side):
    u_ref, bm_ref, are_ref, aim_ref, cm_ref, dskip_ref = refs[:6]
    o_ref = refs[6 + n_side]
    st_a, st_b, ub_s, y_s, h_ref = refs[7 + 2 * n_side:]
    _cast_slices(refs[6:6 + n_side], refs[7 + n_side:7 + 2 * n_side])
    rows = tc * bsz
    half = SSM_PAIRS // 2
    pairs_per_slab = LANES // (2 * SSM_GROUP)
    n_slabs = SSM_WIDTH // LANES
    splits = 2
    n_slices = half * splits
    steps = tc // n_slices
    srows = rows // splits

    @pl.when(pl.program_id(0) == 0)
    def _():
        h_ref[...] = jnp.zeros_like(h_ref)

    for c in range(n_slabs):
        cols = slice(c * LANES, (c + 1) * LANES)
        for b in range(bsz):
            y_s[c, pl.ds(b, tc, stride=bsz), :] = u_ref[b, :, cols]
        u = y_s[c]
        ub_s[c] = u.astype(BF16)
        y_s[c] = dskip_ref[:, cols] * u

    def project_in(buf, j, jl, rs):
        bu = jnp.dot(ub_s[j // pairs_per_slab, rs, :], bm_ref[j], preferred_element_type=F32)
        buf[0, jl, rs, :] = bu[:, :LANES]
        buf[1, jl, rs, :] = bu[:, LANES:]

    def project_out(buf, j, jl, rs):
        st = jnp.concatenate([buf[0, jl, rs, :].astype(BF16), buf[1, jl, rs, :].astype(BF16)],
                             axis=1)
        y_s[j // pairs_per_slab, rs, :] += jnp.dot(st, cm_ref[j], preferred_element_type=F32)

    def scan(buf, j0, other):
        a_re = [are_ref[j0 + i] for i in range(half)]
        a_im = [aim_ref[j0 + i] for i in range(half)]

        def body(it, carry):
            h_re, h_im = carry
            for k in range(steps):
                r0 = pl.multiple_of((it * steps + k) * bsz, bsz)
                new_re, new_im = [], []
                for i in range(half):
                    n_re = a_re[i] * h_re[i] - a_im[i] * h_im[i] + buf[0, i, pl.ds(r0, bsz), :]
                    n_im = a_re[i] * h_im[i] + a_im[i] * h_re[i] + buf[1, i, pl.ds(r0, bsz), :]
                    buf[0, i, pl.ds(r0, bsz), :] = n_re
                    buf[1, i, pl.ds(r0, bsz), :] = n_im
                    new_re.append(n_re)
                    new_im.append(n_im)
                h_re, h_im = tuple(new_re), tuple(new_im)
            other(it // splits, pl.ds((it % splits) * srows, srows))
            return h_re, h_im

        carry = (tuple(h_ref[0, j0 + i] for i in range(half)),
                 tuple(h_ref[1, j0 + i] for i in range(half)))
        for it in range(n_slices):
            carry = body(it, carry)
        h_re, h_im = carry
        for i in range(half):
            h_ref[0, j0 + i] = h_re[i]
            h_ref[1, j0 + i] = h_im[i]

    for jl in range(half):
        project_in(st_a, jl, jl, slice(None))
    scan(st_a, 0, lambda jl, rs: project_in(st_b, half + jl, jl, rs))
    scan(st_b, half, lambda jl, rs: project_out(st_a, jl, jl, rs))
    for jl in range(half):
        project_out(st_b, half + jl, jl, slice(None))

    for c in range(n_slabs):
        for b in range(bsz):
            y = y_s[c, pl.ds(b, tc, stride=bsz), :]
            o_ref[b, :, c * LANES:(c + 1) * LANES] = _gelu_tanh(y).astype(o_ref.dtype)


def _ssm(u, bm, a_re, a_im, cm, d_skip, side, tc=128):
    bsz, s, _ = u.shape
    rows = tc * bsz
    half = SSM_PAIRS // 2
    chunk_spec = pl.BlockSpec((bsz, tc, SSM_WIDTH), lambda i: (0, i, 0))
    side_in, side_out, side_shapes = _cast_slice_specs(side, s // tc, lambda i: i)
    outs = pl.pallas_call(
        functools.partial(_ssm_kernel, tc=tc, bsz=bsz, n_side=len(side)),
        grid=(s // tc,),
        in_specs=[chunk_spec,
                  _const_spec(bm.shape), _const_spec(a_re.shape), _const_spec(a_im.shape),
                  _const_spec(cm.shape), _const_spec(d_skip.shape)] + side_in,
        out_specs=[chunk_spec] + side_out,
        out_shape=[jax.ShapeDtypeStruct((bsz, s, SSM_WIDTH), BF16)] + side_shapes,
        scratch_shapes=[pltpu.VMEM((2, half, rows, LANES), F32),
                        pltpu.VMEM((2, half, rows, LANES), F32),
                        pltpu.VMEM((SSM_WIDTH // LANES, rows, LANES), BF16),
                        pltpu.VMEM((SSM_WIDTH // LANES, rows, LANES), F32),
                        pltpu.VMEM((2, SSM_PAIRS, bsz, LANES), F32)],
        compiler_params=pltpu.CompilerParams(
            dimension_semantics=("arbitrary",), vmem_limit_bytes=VMEM_LIMIT),
        name="ssm",
    )(u, bm, a_re, a_im, cm, d_skip, *[arr for arr, _ in side])
    return outs[0], outs[1:]


def _ssm_matrices(a_re, a_im, log_dt, b_re, b_im, c_re, c_im, bsz):
    g, p, h = SSM_GROUPS, SSM_STATE, SSM_GROUP
    lam_re, lam_im = a_re.astype(F32), a_im.astype(F32)
    dt = jnp.exp(log_dt.astype(F32))[:, None]
    mag = jnp.exp(lam_re * dt)
    abar_re, abar_im = mag * jnp.cos(lam_im * dt), mag * jnp.sin(lam_im * dt)
    num_re, num_im = abar_re - 1.0, abar_im
    den = lam_re * lam_re + lam_im * lam_im
    coef_re = ((num_re * lam_re + num_im * lam_im) / den)[..., None]
    coef_im = ((num_im * lam_re - num_re * lam_im) / den)[..., None]
    b_re, b_im = b_re.astype(F32), b_im.astype(F32)
    bbar_re = coef_re * b_re - coef_im * b_im
    bbar_im = coef_re * b_im + coef_im * b_re

    groups_per_slab = LANES // h
    pairs_per_slab = groups_per_slab // 2
    n_slabs = g // groups_per_slab
    eye = jnp.eye(groups_per_slab, dtype=F32)

    def in_matrix(bpart):
        bt = jnp.transpose(bpart, (0, 2, 1)).reshape(n_slabs, groups_per_slab, h, p)
        dense = bt[:, :, :, None, :] * eye[None, :, None, :, None]
        dense = dense.reshape(n_slabs, LANES, pairs_per_slab, 2 * p)
        return jnp.transpose(dense, (0, 2, 1, 3)).reshape(SSM_PAIRS, LANES, 2 * p)

    def out_matrix(cpart):
        ct = jnp.transpose(cpart, (0, 2, 1)).reshape(n_slabs, groups_per_slab, p, h)
        dense = ct[:, :, :, None, :] * eye[None, :, None, :, None]
        return dense.reshape(SSM_PAIRS, 2 * p, LANES)

    bm = jnp.concatenate([in_matrix(bbar_re), in_matrix(bbar_im)], axis=2)
    cm = jnp.concatenate([out_matrix(c_re.astype(F32)), out_matrix(-c_im.astype(F32))], axis=1)

    def bcast(apart):
        return jnp.broadcast_to(apart.reshape(SSM_PAIRS, 1, 2 * p), (SSM_PAIRS, bsz, 2 * p))

    return bm.astype(BF16), bcast(abar_re), bcast(abar_im), cm.astype(BF16)


def _layer_norm(z, g, b):
    mu = jnp.mean(z, axis=-1, keepdims=True)
    zc = z - mu
    var = jnp.mean(zc * zc, axis=-1, keepdims=True)
    return zc * lax.rsqrt(var + LN_EPS) * g + b


def _mix_ffn_kernel(x_ref, attn_ref, g_ref, wga_ref, wgs_ref, bgate_ref, wattn_ref, wglu_ref,
                    wssm_ref, wout_ref, ln1g_ref, ln1b_ref, wfg_ref, wfu_ref, wfd_ref,
                    ln2g_ref, ln2b_ref, o_ref, *, alpha, sub):
    dm = x_ref.shape[1]
    tiles = [slice(r0, r0 + sub) for r0 in range(0, x_ref.shape[0], sub)]
    dot = functools.partial(jnp.dot, preferred_element_type=F32)
    y_s = [None] * len(tiles)
    for t, rows in enumerate(tiles):
        glu = dot(g_ref[rows, :], wglu_ref[...])
        y_s[t] = (glu[:, :SSM_WIDTH] * jax.nn.sigmoid(glu[:, SSM_WIDTH:])).astype(BF16)
    gated_ssm = [None] * len(tiles)
    gate_attn = [None] * len(tiles)
    for t, rows in enumerate(tiles):
        xb = x_ref[rows, :].astype(BF16)
        gate_attn[t] = jax.nn.sigmoid(dot(xb, wga_ref[...]) + bgate_ref[:, :dm])
        gate_ssm = jax.nn.sigmoid(dot(xb, wgs_ref[...]) + bgate_ref[:, dm:])
        gated_ssm[t] = gate_ssm * dot(y_s[t], wssm_ref[...])
    mixed = [None] * len(tiles)
    for t, rows in enumerate(tiles):
        y_attn = dot(attn_ref[rows, :], wattn_ref[...])
        mixed[t] = (gate_attn[t] * y_attn + gated_ssm[t]).astype(BF16)
    h = [None] * len(tiles)
    for t, rows in enumerate(tiles):
        mix_out = dot(mixed[t], wout_ref[...])
        h[t] = _layer_norm(alpha * x_ref[rows, :] + mix_out, ln1g_ref[...], ln1b_ref[...])
    act = [None] * len(tiles)
    for t, rows in enumerate(tiles):
        hb = h[t].astype(BF16)
        gate = dot(hb, wfg_ref[...])
        up = dot(hb, wfu_ref[...])
        act[t] = ((gate * jax.nn.sigmoid(gate)) * up).astype(BF16)
    for t, rows in enumerate(tiles):
        ff = dot(act[t], wfd_ref[...])
        o_ref[rows, :] = _layer_norm(alpha * h[t] + ff, ln2g_ref[...], ln2b_ref[...])


def _mix_ffn(x2, attn2, g2, consts, alpha, tm=512, sub=256):
    rows, dm = x2.shape
    row_spec = lambda w: pl.BlockSpec((tm, w), lambda i: (i, 0))
    return pl.pallas_call(
        functools.partial(_mix_ffn_kernel, alpha=alpha, sub=sub),
        grid=(rows // tm,),
        in_specs=[row_spec(dm), row_spec(ATTN_WIDTH), row_spec(SSM_WIDTH)]
                 + [_const_spec(c.shape) for c in consts],
        out_specs=row_spec(dm),
        out_shape=jax.ShapeDtypeStruct((rows, dm), F32),
        compiler_params=pltpu.CompilerParams(
            dimension_semantics=("parallel",), vmem_limit_bytes=VMEM_LIMIT),
        name="mix_ffn",
    )(x2, attn2, g2, *consts)


def _rope_tables(s):
    half = HEAD_DIM // 2
    f32 = np.float32
    pos = np.arange(s, dtype=f32)
    inv_freq = f32(ROPE_THETA) ** (-np.arange(half, dtype=f32) / f32(half))
    ang = pos[:, None] * inv_freq[None, :]
    cos, sin = np.cos(ang), np.sin(ang)
    reps = LANES // HEAD_DIM
    cos_t = np.concatenate([cos, cos] * reps, axis=1)
    sin_t = np.concatenate([-sin, sin] * reps, axis=1)
    return cos_t.astype(f32), sin_t.astype(f32)


def kernel(x, w_in, b_gate, w_attn_br, w_ssm_br, w_out, ssm_a_re, ssm_a_im, ssm_log_dt, ssm_b_re, ssm_b_im, ssm_c_re, ssm_c_im, ssm_d, w_glu, ln1_g, ln1_b, w_ff_gate, w_ff_up, w_ff_down, ln2_g, ln2_b):
    bsz, s, dm = x.shape
    depth = w_in.shape[0]
    alpha = (2.0 * depth) ** 0.25
    assert all(w // d == ATTN_BLOCK for w, d in zip(WINDOWS, DILATIONS))
    assert s % (ATTN_BLOCK * max(DILATIONS)) == 0
    cos_t, sin_t = _rope_tables(s)
    assert w_in.shape[2] == 3 * QKV_WIDTH + SSM_WIDTH + 2 * dm

    gate_col0 = (3 * QKV_WIDTH + SSM_WIDTH) // dm

    for layer in range(depth):
        outs, (wga, wgs, wattn, wglu, wssm, wout) = _project(
            x, cos_t, sin_t, w_in[layer],
            [(w_in[layer], (gate_col0, dm)), (w_in[layer], (gate_col0 + 1, dm)),
             (w_attn_br[layer], None), (w_glu[layer], None), (w_ssm_br[layer], None),
             (w_out[layer], None)])
        attn = _attention(outs[:N_GROUPS], bsz, s)

        bm, a_re, a_im, cm = _ssm_matrices(
            ssm_a_re[layer], ssm_a_im[layer], ssm_log_dt[layer], ssm_b_re[layer],
            ssm_b_im[layer], ssm_c_re[layer], ssm_c_im[layer], bsz)
        g, (wff_gate, wff_up, wff_down) = _ssm(
            outs[N_GROUPS], bm, a_re, a_im, cm, ssm_d[layer].reshape(1, SSM_WIDTH).astype(F32),
            [(w_ff_gate[layer], None), (w_ff_up[layer], None), (w_ff_down[layer], None)])

        row = lambda v: v[layer].reshape(1, -1).astype(F32)
        x2 = _mix_ffn(
            x.reshape(bsz * s, dm), attn.reshape(bsz * s, ATTN_WIDTH),
            g.reshape(bsz * s, SSM_WIDTH),
            [wga, wgs, row(b_gate), wattn, wglu, wssm, wout, row(ln1_g), row(ln1_b),
             wff_gate, wff_up, wff_down, row(ln2_g), row(ln2_b)],
            alpha)
        x = x2.reshape(bsz, s, dm)
    return x
```

```python
import functools
import math

import jax
import jax.numpy as jnp
import numpy as np
from jax import lax
from jax.experimental import pallas as pl
from jax.experimental.pallas import tpu as pltpu

F32 = jnp.float32
BF16 = jnp.bfloat16

LANES = 128
HEAD_DIM = 64
ATTN_HEADS = 8
ATTN_WIDTH = ATTN_HEADS * HEAD_DIM
DILATIONS = (1, 4, 16)
WINDOWS = (128, 512, 2048)
N_GROUPS = len(DILATIONS)
QKV_WIDTH = N_GROUPS * ATTN_WIDTH
QA_COL, QB_COL, K_COL, VA_COL, VB_COL = (i * ATTN_WIDTH for i in range(5))
ATTN_IN_WIDTH = 5 * ATTN_WIDTH
ATTN_BLOCK = 128
ROPE_THETA = 10000.0
NEG_INF = -1e30
Q_SCALE = math.log2(math.e) / math.sqrt(HEAD_DIM)
SSM_GROUP = 16
SSM_GROUPS = 32
SSM_WIDTH = SSM_GROUP * SSM_GROUPS
SSM_STATE = 64
SSM_PAIRS = SSM_GROUPS // 2
LN_EPS = 1e-5

VMEM_LIMIT = 56 * 1024 * 1024


def _const_spec(shape):
    zeros = (0,) * len(shape)
    return pl.BlockSpec(shape, lambda *_: zeros, pipeline_mode=pl.Buffered(1))


def _cast_slice_specs(side, n_steps, step_of):
    in_specs, out_specs, out_shapes = [], [], []
    for arr, col in side:
        rows = arr.shape[0] // n_steps
        assert rows * n_steps == arr.shape[0] and rows % 16 == 0
        cidx, width = col if col is not None else (0, arr.shape[1])
        in_specs.append(pl.BlockSpec(
            (rows, width), functools.partial(lambda *ids, cidx: (step_of(*ids), cidx), cidx=cidx)))
        out_specs.append(pl.BlockSpec((rows, width), lambda *ids: (step_of(*ids), 0)))
        out_shapes.append(jax.ShapeDtypeStruct((arr.shape[0], width), BF16))
    return in_specs, out_specs, out_shapes


def _cast_slices(src_refs, dst_refs):
    for src, dst in zip(src_refs, dst_refs):
        dst[...] = src[...].astype(dst.dtype)


def _phase_major(ref, tm, d):
    if d == 1:
        return ref[...]
    n = tm // d
    return jnp.concatenate([ref[pl.ds(r, n, stride=d), :] for r in range(d)], axis=0)


def _block_phase_order(ref, tm):
    dmax = max(DILATIONS)
    tile = ATTN_BLOCK // dmax
    return jnp.concatenate([ref[pl.ds(b0 + r, tile, stride=dmax), :]
                            for b0 in range(0, tm, ATTN_BLOCK) for r in range(dmax)], axis=0)


def _proj_kernel(*refs, tm, n_slabs, n_side):
    refs = list(refs)
    x_refs = [refs.pop(0) for _ in range(n_slabs)]
    cos_ref, sin_ref, w32_ref = [refs.pop(0) for _ in range(3)]
    side_in = [refs.pop(0) for _ in range(n_side)]
    o_refs = [refs.pop(0) for _ in range(N_GROUPS)]
    u_ref = refs.pop(0)
    side_out = [refs.pop(0) for _ in range(n_side)]
    w_ref, = refs

    @pl.when((pl.program_id(0) == 0) & (pl.program_id(1) == 0))
    def _():
        for c0 in range(0, w_ref.shape[1], ATTN_WIDTH):
            w_ref[:, c0:c0 + ATTN_WIDTH] = w32_ref[:, c0:c0 + ATTN_WIDTH].astype(BF16)

    _cast_slices(side_in, side_out)

    lane = lax.broadcasted_iota(jnp.int32, (1, LANES), 1)
    first_half = (lane % HEAD_DIM) < (HEAD_DIM // 2)
    head_a = lane < HEAD_DIM

    for gi, d in enumerate(DILATIONS):
        n = tm // d
        xg = jnp.concatenate(
            [_phase_major(xr, tm, d).astype(BF16) for xr in x_refs], axis=1)
        cos = _phase_major(cos_ref, tm, d)
        sin = _phase_major(sin_ref, tm, d)
        if gi == N_GROUPS - 1:
            x_nat = jnp.concatenate([xr[...].astype(BF16) for xr in x_refs], axis=1)
            u_ref[...] = jnp.dot(x_nat, w_ref[:, 3 * QKV_WIDTH:], preferred_element_type=F32)
        for part in range(3):
            c0 = part * ATTN_WIDTH
            w0 = part * QKV_WIDTH + gi * ATTN_WIDTH
            lhs, rope_cos, rope_sin = xg, cos, sin
            if gi == 0 and part == 0:
                lhs = jnp.concatenate(
                    [_block_phase_order(xr, tm).astype(BF16) for xr in x_refs], axis=1)
                rope_cos = _block_phase_order(cos_ref, tm)
                rope_sin = _block_phase_order(sin_ref, tm)
            acc = jnp.dot(lhs, w_ref[:, w0:w0 + ATTN_WIDTH], preferred_element_type=F32)
            for c in range(ATTN_WIDTH // LANES):
                t = acc[:, c * LANES:(c + 1) * LANES]
                if part < 2:
                    rot = jnp.where(first_half,
                                    pltpu.roll(t, LANES - HEAD_DIM // 2, 1),
                                    pltpu.roll(t, HEAD_DIM // 2, 1))
                    t = t * rope_cos + rot * rope_sin
                if part == 0:
                    t = t * Q_SCALE
                if part == 1:
                    pieces = [(K_COL, t)]
                else:
                    col_a, col_b = (QA_COL, QB_COL) if part == 0 else (VA_COL, VB_COL)
                    pieces = [(col_a, jnp.where(head_a, t, 0.0)), (col_b, jnp.where(head_a, 0.0, t))]
                for col0, piece in pieces:
                    piece = piece.astype(BF16)
                    col = col0 + c * LANES
                    if d == 1:
                        o_refs[gi][:, col:col + LANES] = piece
                    else:
                        for r in range(d):
                            o_refs[gi][r, :, col:col + LANES] = piece[r * n:(r + 1) * n]


def _project(x, cos_t, sin_t, w_in, side, tm=512):
    b, s, dm = x.shape
    n_slabs = dm // LANES
    n_proj = 3 * QKV_WIDTH + SSM_WIDTH
    grid = (b, s // tm)
    side_in, side_out, side_shapes = _cast_slice_specs(
        side, b * (s // tm), lambda bi, i: bi * (s // tm) + i)
    x_specs = [pl.BlockSpec((None, tm, LANES), functools.partial(lambda bi, i, c: (bi, i, c), c=c))
               for c in range(n_slabs)]
    tab_spec = pl.BlockSpec((tm, LANES), lambda bi, i: (i, 0))
    out_shapes = []
    out_specs = []
    for d in DILATIONS:
        if d == 1:
            out_shapes.append(jax.ShapeDtypeStruct((b, s, ATTN_IN_WIDTH), BF16))
            out_specs.append(pl.BlockSpec((None, tm, ATTN_IN_WIDTH), lambda bi, i: (bi, i, 0)))
        else:
            out_shapes.append(jax.ShapeDtypeStruct((b, d, s // d, ATTN_IN_WIDTH), BF16))
            out_specs.append(pl.BlockSpec((None, d, tm // d, ATTN_IN_WIDTH),
                                          lambda bi, i: (bi, 0, i, 0)))
    out_shapes.append(jax.ShapeDtypeStruct((b, s, SSM_WIDTH), F32))
    out_specs.append(pl.BlockSpec((None, tm, SSM_WIDTH), lambda bi, i: (bi, i, 0)))
    outs = pl.pallas_call(
        functools.partial(_proj_kernel, tm=tm, n_slabs=n_slabs, n_side=len(side)),
        grid=grid,
        in_specs=x_specs + [tab_spec, tab_spec,
                            pl.BlockSpec((dm, n_proj), lambda bi, i: (0, 0),
                                         pipeline_mode=pl.Buffered(1))] + side_in,
        out_specs=out_specs + side_out,
        out_shape=out_shapes + side_shapes,
        scratch_shapes=[pltpu.VMEM((dm, n_proj), BF16)],
        compiler_params=pltpu.CompilerParams(
            dimension_semantics=("arbitrary", "arbitrary"), vmem_limit_bytes=VMEM_LIMIT),
        name="in_proj",
    )(*([x] * n_slabs), cos_t, sin_t, w_in, *[arr for arr, _ in side])
    n_main = N_GROUPS + 1
    return outs[:n_main], outs[n_main:]


def _attn_bias():
    blk = ATTN_BLOCK
    c = np.arange(2 * blk)[None, :]

    def tables(position):
        a = np.tile(position, 2)[:, None]
        in_window = np.where(c < blk, c >= a, c - blk <= a)
        windowed = np.where(in_window, 0.0, NEG_INF).astype(np.float32)
        causal = np.where(c[:, :blk] <= a, 0.0, NEG_INF).astype(np.float32)
        return windowed, causal

    return tables(np.arange(blk)), tables(_PHASE_ORDER)


_PHASE_ORDER = np.arange(ATTN_BLOCK).reshape(-1, max(DILATIONS)).T.reshape(-1)


def _attn_kernel(*refs, s, lags):
    n_in = 5 * N_GROUPS
    qkv = refs[:n_in]
    mask_w_ref, mask_c_ref, pmask_w_ref, pmask_c_ref, ones_ref = refs[n_in:n_in + 5]
    o_ref = refs[n_in + 5]
    acc_s, m_s, l_s, merged_s = refs[n_in + 6:]
    blk = ATTN_BLOCK
    n_blocks = s // blk
    dmax = max(DILATIONS)
    tile = blk // dmax

    lane = lax.broadcasted_iota(jnp.int32, (1, LANES), 1)
    head_a = lane < HEAD_DIM

    def window(ref, i, nblk):
        start = i if i % nblk == 0 else i - 1
        return ref[start * blk:(i + 1) * blk, :]

    def scores(gi, i, nblk):
        qa_ref, qb_ref, k_ref = qkv[5 * gi:5 * gi + 3]
        if gi == 0:
            mask = pmask_c_ref[...] if i % nblk == 0 else pmask_w_ref[...]
        else:
            mask = mask_c_ref[...] if i % nblk == 0 else mask_w_ref[...]
        q2 = jnp.concatenate([qa_ref[i * blk:(i + 1) * blk, :], qb_ref[i * blk:(i + 1) * blk, :]],
                             axis=0)
        return lax.dot_general(q2, window(k_ref, i, nblk), (((1,), (1,)), ((), ())),
                               preferred_element_type=F32) + mask

    def softmax(sc):
        m = jnp.max(sc, axis=1, keepdims=True)
        return m, jnp.exp2((sc - m).astype(BF16))

    def finish(gi, d, i, nblk, m, p):
        va, vb = (window(ref, i, nblk) for ref in qkv[5 * gi + 3:5 * gi + 5])
        w = va.shape[0]
        v_ext = jnp.concatenate(
            [jnp.concatenate([va, ones_ref[0, :w, :]], axis=1),
             jnp.concatenate([vb, ones_ref[1, :w, :]], axis=1)], axis=0)
        p2 = jnp.concatenate([p[:blk], p[blk:]], axis=1)
        res = jnp.dot(p2, v_ext, preferred_element_type=F32)
        acc, l = res[:, :LANES], res[:, LANES:]
        m = jnp.where(head_a, m[:blk], m[blk:])
        rows = slice(i * blk, (i + 1) * blk)
        if d > 1:
            acc_s[gi - 1, rows, :] = acc
            l_s[gi - 1, rows, :] = l
            m_s[gi - 1, rows, :] = m
            return

        def parked(ref, g):
            dg = DILATIONS[g + 1]
            tiles = []
            for r in range(dmax):
                start = (r % dg) * (s // dg) + (blk // dg) * i + r // dg
                idx = pl.ds(start, tile, stride=dmax // dg) if dg < dmax else pl.ds(start, tile)
                tiles.append(ref[g, idx, :])
            return jnp.concatenate(tiles, axis=0)

        parts = [(m, l, acc)] + [(parked(m_s, g), parked(l_s, g), parked(acc_s, g))
                                 for g in range(N_GROUPS - 1)]
        m_all = functools.reduce(jnp.maximum, [pm for pm, _, _ in parts])
        num = jnp.zeros((blk, LANES), F32)
        den = jnp.zeros((blk, LANES), F32)
        for pm, pl_, pacc in parts:
            wgt = jnp.exp2(pm - m_all)
            num = num + wgt * pacc
            den = den + wgt * pl_
        merged_s[i] = num / den
        natural = [merged_s[i, pl.ds((a0 % dmax) * tile + a0 // dmax, tile, stride=tile), :]
                   for a0 in range(0, blk, tile)]
        o_ref[rows, :] = jnp.concatenate(natural, axis=0).astype(o_ref.dtype)

    assert DILATIONS[0] == 1
    block = lambda gi, i: (gi, DILATIONS[gi], i, (s // DILATIONS[gi]) // blk)
    ready = [0] * n_blocks
    dilated = []
    for gi in reversed(range(1, N_GROUPS)):
        nblk = (s // DILATIONS[gi]) // blk
        order = sorted(range(n_blocks), key=lambda b: (b % nblk, b // nblk))
        for pos, b in enumerate(order):
            dilated.append(block(gi, b))
            for i in range(n_blocks):
                if i * nblk // n_blocks == b % nblk:
                    ready[i] = max(ready[i], len(dilated))
    work, done = [], 0
    for i in range(n_blocks):
        take = max(ready[i] - done, 1 if done < len(dilated) else 0)
        work += dilated[done:done + take]
        done += take
        work.append(block(0, i))
    work += dilated[done:]
    lag_softmax, lag_values = lags
    sc_q, p_q = {}, {}
    for step in range(len(work) + lag_values):
        if step < len(work):
            gi, d, i, nblk = work[step]
            sc_q[step] = scores(gi, i, nblk)
        if 0 <= step - lag_softmax < len(work):
            p_q[step - lag_softmax] = softmax(sc_q.pop(step - lag_softmax))
        if step >= lag_values:
            gi, d, i, nblk = work[step - lag_values]
            finish(gi, d, i, nblk, *p_q.pop(step - lag_values))


def _attention(qkv_groups, b, s, lags=(1, 4)):
    n_hp = ATTN_WIDTH // LANES
    in_specs, args = [], []
    for gi, d in enumerate(DILATIONS):
        arr = qkv_groups[gi].reshape(b, s, ATTN_IN_WIDTH)
        for col0 in (QA_COL, QB_COL, K_COL, VA_COL, VB_COL):
            in_specs.append(pl.BlockSpec(
                (None, s, LANES),
                functools.partial(lambda bi, hp, c: (bi, 0, c + hp), c=col0 // LANES)))
            args.append(arr)
    (mask_w, mask_c), (pmask_w, pmask_c) = _attn_bias()
    head_ones = np.stack([np.arange(LANES) < HEAD_DIM, np.arange(LANES) >= HEAD_DIM])
    head_ones = jnp.asarray(np.broadcast_to(head_ones[:, None, :], (2, 2 * ATTN_BLOCK, LANES)),
                            dtype=BF16)
    masks = [mask_w, mask_c, pmask_w, pmask_c, head_ones]
    return pl.pallas_call(
        functools.partial(_attn_kernel, s=s, lags=lags),
        grid=(b, n_hp),
        in_specs=in_specs + [_const_spec(t.shape) for t in masks],
        out_specs=pl.BlockSpec((None, s, LANES), lambda bi, hp: (bi, 0, hp)),
        out_shape=jax.ShapeDtypeStruct((b, s, ATTN_WIDTH), BF16),
        scratch_shapes=[pltpu.VMEM((N_GROUPS - 1, s, LANES), F32)] * 3
                       + [pltpu.VMEM((s // ATTN_BLOCK, ATTN_BLOCK, LANES), F32)],
        compiler_params=pltpu.CompilerParams(
            dimension_semantics=("parallel", "parallel"), vmem_limit_bytes=VMEM_LIMIT),
        name="attention",
    )(*args, *masks)


def _gelu_tanh(x):
    return 0.5 * x * (1.0 + jnp.tanh(math.sqrt(2.0 / math.pi) * (x + 0.044715 * (x * x * x))))


def _ssm_kernel(*refs, tc, bsz, n_side):
    u_ref, bm_ref, are_ref, aim_ref, cm_ref, dskip_ref = refs[:6]
    o_ref = refs[6 + n_side]
    st_a, st_b, ub_s, y_s, h_ref = refs[7 + 2 * n_side:]
    _cast_slices(refs[6:6 + n_side], refs[7 + n_side:7 + 2 * n_side])
    rows = tc * bsz
    half = SSM_PAIRS // 2
    pairs_per_slab = LANES // (2 * SSM_GROUP)
    n_slabs = SSM_WIDTH // LANES
    splits = 2
    n_slices = half * splits
    steps = tc // n_slices
    srows = rows // splits

    @pl.when(pl.program_id(0) == 0)
    def _():
        h_ref[...] = jnp.zeros_like(h_ref)

    for c in range(n_slabs):
        cols = slice(c * LANES, (c + 1) * LANES)
        for b in range(bsz):
            y_s[c, pl.ds(b, tc, stride=bsz), :] = u_ref[b, :, cols]
        u = y_s[c]
        ub_s[c] = u.astype(BF16)
        y_s[c] = dskip_ref[:, cols] * u

    def project_in(buf, j, jl, rs):
        bu = jnp.dot(ub_s[j // pairs_per_slab, rs, :], bm_ref[j], preferred_element_type=F32)
        buf[0, jl, rs, :] = bu[:, :LANES]
        buf[1, jl, rs, :] = bu[:, LANES:]

    def project_out(buf, j, jl, rs):
        st = jnp.concatenate([buf[0, jl, rs, :].astype(BF16), buf[1, jl, rs, :].astype(BF16)],
                             axis=1)
        y_s[j // pairs_per_slab, rs, :] += jnp.dot(st, cm_ref[j], preferred_element_type=F32)

    def scan(buf, j0, other):
        a_re = [are_ref[j0 + i] for i in range(half)]
        a_im = [aim_ref[j0 + i] for i in range(half)]

        def body(it, carry):
            h_re, h_im = carry
            for k in range(steps):
                r0 = pl.multiple_of((it * steps + k) * bsz, bsz)
                new_re, new_im = [], []
                for i in range(half):
                    n_re = a_re[i] * h_re[i] - a_im[i] * h_im[i] + buf[0, i, pl.ds(r0, bsz), :]
                    n_im = a_re[i] * h_im[i] + a_im[i] * h_re[i] + buf[1, i, pl.ds(r0, bsz), :]
                    buf[0, i, pl.ds(r0, bsz), :] = n_re
                    buf[1, i, pl.ds(r0, bsz), :] = n_im
                    new_re.append(n_re)
                    new_im.append(n_im)
                h_re, h_im = tuple(new_re), tuple(new_im)
            other(it // splits, pl.ds((it % splits) * srows, srows))
            return h_re, h_im

        carry = (tuple(h_ref[0, j0 + i] for i in range(half)),
                 tuple(h_ref[1, j0 + i] for i in range(half)))
        for it in range(n_slices):
            carry = body(it, carry)
        h_re, h_im = carry
        for i in range(half):
            h_ref[0, j0 + i] = h_re[i]
            h_ref[1, j0 + i] = h_im[i]

    for jl in range(half):
        project_in(st_a, jl, jl, slice(None))
    scan(st_a, 0, lambda jl, rs: project_in(st_b, half + jl, jl, rs))
    scan(st_b, half, lambda jl, rs: project_out(st_a, jl, jl, rs))
    for jl in range(half):
        project_out(st_b, half + jl, jl, slice(None))

    for c in range(n_slabs):
        for b in range(bsz):
            y = y_s[c, pl.ds(b, tc, stride=bsz), :]
            o_ref[b, :, c * LANES:(c + 1) * LANES] = _gelu_tanh(y).astype(o_ref.dtype)


def _ssm(u, bm, a_re, a_im, cm, d_skip, side, tc=128):
    bsz, s, _ = u.shape
    rows = tc * bsz
    half = SSM_PAIRS // 2
    chunk_spec = pl.BlockSpec((bsz, tc, SSM_WIDTH), lambda i: (0, i, 0))
    side_in, side_out, side_shapes = _cast_slice_specs(side, s // tc, lambda i: i)
    outs = pl.pallas_call(
        functools.partial(_ssm_kernel, tc=tc, bsz=bsz, n_side=len(side)),
        grid=(s // tc,),
        in_specs=[chunk_spec,
                  _const_spec(bm.shape), _const_spec(a_re.shape), _const_spec(a_im.shape),
                  _const_spec(cm.shape), _const_spec(d_skip.shape)] + side_in,
        out_specs=[chunk_spec] + side_out,
        out_shape=[jax.ShapeDtypeStruct((bsz, s, SSM_WIDTH), BF16)] + side_shapes,
        scratch_shapes=[pltpu.VMEM((2, half, rows, LANES), F32),
                        pltpu.VMEM((2, half, rows, LANES), F32),
                        pltpu.VMEM((SSM_WIDTH // LANES, rows, LANES), BF16),
                        pltpu.VMEM((SSM_WIDTH // LANES, rows, LANES), F32),
                        pltpu.VMEM((2, SSM_PAIRS, bsz, LANES), F32)],
        compiler_params=pltpu.CompilerParams(
            dimension_semantics=("arbitrary",), vmem_limit_bytes=VMEM_LIMIT),
        name="ssm",
    )(u, bm, a_re, a_im, cm, d_skip, *[arr for arr, _ in side])
    return outs[0], outs[1:]


def _ssm_matrices(a_re, a_im, log_dt, b_re, b_im, c_re, c_im, bsz):
    g, p, h = SSM_GROUPS, SSM_STATE, SSM_GROUP
    lam_re, lam_im = a_re.astype(F32), a_im.astype(F32)
    dt = jnp.exp(log_dt.astype(F32))[:, None]
    mag = jnp.exp(lam_re * dt)
    abar_re, abar_im = mag * jnp.cos(lam_im * dt), mag * jnp.sin(lam_im * dt)
    num_re, num_im = abar_re - 1.0, abar_im
    den = lam_re * lam_re + lam_im * lam_im
    coef_re = ((num_re * lam_re + num_im * lam_im) / den)[..., None]
    coef_im = ((num_im * lam_re - num_re * lam_im) / den)[..., None]
    b_re, b_im = b_re.astype(F32), b_im.astype(F32)
    bbar_re = coef_re * b_re - coef_im * b_im
    bbar_im = coef_re * b_im + coef_im * b_re

    groups_per_slab = LANES // h
    pairs_per_slab = groups_per_slab // 2
    n_slabs = g // groups_per_slab
    eye = jnp.eye(groups_per_slab, dtype=F32)

    def in_matrix(bpart):
        bt = jnp.transpose(bpart, (0, 2, 1)).reshape(n_slabs, groups_per_slab, h, p)
        dense = bt[:, :, :, None, :] * eye[None, :, None, :, None]
        dense = dense.reshape(n_slabs, LANES, pairs_per_slab, 2 * p)
        return jnp.transpose(dense, (0, 2, 1, 3)).reshape(SSM_PAIRS, LANES, 2 * p)

    def out_matrix(cpart):
        ct = jnp.transpose(cpart, (0, 2, 1)).reshape(n_slabs, groups_per_slab, p, h)
        dense = ct[:, :, :, None, :] * eye[None, :, None, :, None]
        return dense.reshape(SSM_PAIRS, 2 * p, LANES)

    bm = jnp.concatenate([in_matrix(bbar_re), in_matrix(bbar_im)], axis=2)
    cm = jnp.concatenate([out_matrix(c_re.astype(F32)), out_matrix(-c_im.astype(F32))], axis=1)

    def bcast(apart):
        return jnp.broadcast_to(apart.reshape(SSM_PAIRS, 1, 2 * p), (SSM_PAIRS, bsz, 2 * p))

    return bm.astype(BF16), bcast(abar_re), bcast(abar_im), cm.astype(BF16)


def _layer_norm(z, g, b):
    mu = jnp.mean(z, axis=-1, keepdims=True)
    zc = z - mu
    var = jnp.mean(zc * zc, axis=-1, keepdims=True)
    return zc * lax.rsqrt(var + LN_EPS) * g + b


def _mix_ffn_kernel(x_ref, attn_ref, g_ref, wga_ref, wgs_ref, bgate_ref, wattn_ref, wglu_ref,
                    wssm_ref, wout_ref, ln1g_ref, ln1b_ref, wfg_ref, wfu_ref, wfd_ref,
                    ln2g_ref, ln2b_ref, o_ref, *, alpha, sub):
    dm = x_ref.shape[1]
    tiles = [slice(r0, r0 + sub) for r0 in range(0, x_ref.shape[0], sub)]
    dot = functools.partial(jnp.dot, preferred_element_type=F32)
    y_s = [None] * len(tiles)
    for t, rows in enumerate(tiles):
        glu = dot(g_ref[rows, :], wglu_ref[...])
        y_s[t] = (glu[:, :SSM_WIDTH] * jax.nn.sigmoid(glu[:, SSM_WIDTH:])).astype(BF16)
    gated_ssm = [None] * len(tiles)
    gate_attn = [None] * len(tiles)
    for t, rows in enumerate(tiles):
        xb = x_ref[rows, :].astype(BF16)
        gate_attn[t] = jax.nn.sigmoid(dot(xb, wga_ref[...]) + bgate_ref[:, :dm])
        gate_ssm = jax.nn.sigmoid(dot(xb, wgs_ref[...]) + bgate_ref[:, dm:])
        gated_ssm[t] = gate_ssm * dot(y_s[t], wssm_ref[...])
    mixed = [None] * len(tiles)
    for t, rows in enumerate(tiles):
        y_attn = dot(attn_ref[rows, :], wattn_ref[...])
        mixed[t] = (gate_attn[t] * y_attn + gated_ssm[t]).astype(BF16)
    h = [None] * len(tiles)
    for t, rows in enumerate(tiles):
        mix_out = dot(mixed[t], wout_ref[...])
        h[t] = _layer_norm(alpha * x_ref[rows, :] + mix_out, ln1g_ref[...], ln1b_ref[...])
    d_ff = wfg_ref.shape[1]
    step = 3 * 256
    chunks = [slice(c0, min(c0 + step, d_ff)) for c0 in range(0, d_ff, step)]
    hb = [ht.astype(BF16) for ht in h]
    ff = [None] * len(tiles)
    act = {}
    for ci, cols in enumerate(chunks + [None]):
        for t in range(len(tiles)):
            if cols is not None:
                gate = dot(hb[t], wfg_ref[:, cols])
                up = dot(hb[t], wfu_ref[:, cols])
                act[t, ci] = ((gate * jax.nn.sigmoid(gate)) * up).astype(BF16)
            if ci > 0:
                part = dot(act.pop((t, ci - 1)), wfd_ref[chunks[ci - 1], :])
                ff[t] = part if ff[t] is None else ff[t] + part
    for t, rows in enumerate(tiles):
        o_ref[rows, :] = _layer_norm(alpha * h[t] + ff[t], ln2g_ref[...], ln2b_ref[...])


def _mix_ffn(x2, attn2, g2, consts, alpha, tm=512, sub=256):
    rows, dm = x2.shape
    row_spec = lambda w: pl.BlockSpec((tm, w), lambda i: (i, 0))
    return pl.pallas_call(
        functools.partial(_mix_ffn_kernel, alpha=alpha, sub=sub),
        grid=(rows // tm,),
        in_specs=[row_spec(dm), row_spec(ATTN_WIDTH), row_spec(SSM_WIDTH)]
                 + [_const_spec(c.shape) for c in consts],
        out_specs=row_spec(dm),
        out_shape=jax.ShapeDtypeStruct((rows, dm), F32),
        compiler_params=pltpu.CompilerParams(
            dimension_semantics=("parallel",), vmem_limit_bytes=VMEM_LIMIT),
        name="mix_ffn",
    )(x2, attn2, g2, *consts)


def _rope_tables(s):
    half = HEAD_DIM // 2
    f32 = np.float32
    pos = np.arange(s, dtype=f32)
    inv_freq = f32(ROPE_THETA) ** (-np.arange(half, dtype=f32) / f32(half))
    ang = pos[:, None] * inv_freq[None, :]
    cos, sin = np.cos(ang), np.sin(ang)
    reps = LANES // HEAD_DIM
    cos_t = np.concatenate([cos, cos] * reps, axis=1)
    sin_t = np.concatenate([-sin, sin] * reps, axis=1)
    return cos_t.astype(f32), sin_t.astype(f32)


def kernel(x, w_in, b_gate, w_attn_br, w_ssm_br, w_out, ssm_a_re, ssm_a_im, ssm_log_dt, ssm_b_re, ssm_b_im, ssm_c_re, ssm_c_im, ssm_d, w_glu, ln1_g, ln1_b, w_ff_gate, w_ff_up, w_ff_down, ln2_g, ln2_b):
    bsz, s, dm = x.shape
    depth = w_in.shape[0]
    alpha = (2.0 * depth) ** 0.25
    assert all(w // d == ATTN_BLOCK for w, d in zip(WINDOWS, DILATIONS))
    assert s % (ATTN_BLOCK * max(DILATIONS)) == 0
    cos_t, sin_t = _rope_tables(s)
    assert w_in.shape[2] == 3 * QKV_WIDTH + SSM_WIDTH + 2 * dm

    gate_col0 = (3 * QKV_WIDTH + SSM_WIDTH) // dm

    for layer in range(depth):
        outs, (wga, wgs, wattn, wglu, wssm, wout) = _project(
            x, cos_t, sin_t, w_in[layer],
            [(w_in[layer], (gate_col0, dm)), (w_in[layer], (gate_col0 + 1, dm)),
             (w_attn_br[layer], None), (w_glu[layer], None), (w_ssm_br[layer], None),
             (w_out[layer], None)])
        attn = _attention(outs[:N_GROUPS], bsz, s)

        bm, a_re, a_im, cm = _ssm_matrices(
            ssm_a_re[layer], ssm_a_im[layer], ssm_log_dt[layer], ssm_b_re[layer],
            ssm_b_im[layer], ssm_c_re[layer], ssm_c_im[layer], bsz)
        g, (wff_gate, wff_up, wff_down) = _ssm(
            outs[N_GROUPS], bm, a_re, a_im, cm, ssm_d[layer].reshape(1, SSM_WIDTH).astype(F32),
            [(w_ff_gate[layer], None), (w_ff_up[layer], None), (w_ff_down[layer], None)])

        row = lambda v: v[layer].reshape(1, -1).astype(F32)
        x2 = _mix_ffn(
            x.reshape(bsz * s, dm), attn.reshape(bsz * s, ATTN_WIDTH),
            g.reshape(bsz * s, SSM_WIDTH),
            [wga, wgs, row(b_gate), wattn, wglu, wssm, wout, row(ln1_g), row(ln1_b),
             wff_gate, wff_up, wff_down, row(ln2_g), row(ln2_b)],
            alpha)
        x = x2.reshape(bsz, s, dm)
    return x
```
